```python
import jax, jax.numpy as jnp
from jax import lax
import numpy as np

D_MODEL = 2048
BATCH = 8
SEQ = 4096
DEPTH = 4
DEC_BATCH = 32
DEC_SEQ = 16
PAST_LEN = 2048

CHUNK = 64
Q_BLOCK = 128
N_HEADS = 16
Q_LORA = 512
KV_LORA = 512
D_NOPE = 128
D_ROPE = 64
D_V = 128
ROPE_BASE = 10000.0
ATTN_SCALE = (D_NOPE + D_ROPE) ** -0.5
POOL_WINDOWS = (2, 4, 8, 16)
N_POOL_GROUPS = len(POOL_WINDOWS)
POOL_GROUP_DIM = D_MODEL // N_POOL_GROUPS
POOL_HIST = max(POOL_WINDOWS) - 1
D_FF = 5632
CONV_WIDTH = 3
N_MLA_LAYERS = (DEPTH + 1) // 2
N_POOL_LAYERS = DEPTH // 2
EPS = 1e-6

kernel_name = "hybrid_mla_pool_convffn_stream_step"


def rms_norm(x, g):
    xf = x.astype(jnp.float32)
    y = xf * lax.rsqrt(jnp.mean(xf * xf, axis=-1, keepdims=True) + EPS)
    return (y * g.astype(jnp.float32)).astype(x.dtype)


def rope(x, pos):
    half = D_ROPE // 2
    inv = ROPE_BASE ** (-jnp.arange(half, dtype=jnp.float32) / half)
    ang = pos[:, None] * inv[None, :]
    shape = (1, pos.shape[0]) + (1,) * (x.ndim - 3) + (half,)
    c = jnp.cos(ang).reshape(shape)
    s = jnp.sin(ang).reshape(shape)
    xf = x.astype(jnp.float32)
    x1, x2 = xf[..., :half], xf[..., half:]
    return jnp.concatenate([x1 * c - x2 * s, x1 * s + x2 * c], axis=-1).astype(x.dtype)


def chunk_mask(qpos, kpos):
    return (kpos // CHUNK)[None, :] <= (qpos // CHUNK)[:, None]


def masked_softmax(s, mask):
    s = s.astype(jnp.float32) * ATTN_SCALE
    s = jnp.where(mask[None, None], s, -jnp.inf)
    return jax.nn.softmax(s, axis=-1)


def mla_project(h, pos, w_dq, q_norm, w_uq, w_dkv, kv_norm):
    B, S, _ = h.shape
    cq = rms_norm(h @ w_dq, q_norm)
    q = (cq @ w_uq).reshape(B, S, N_HEADS, D_NOPE + D_ROPE)
    q_nope = q[..., :D_NOPE]
    q_rope = rope(q[..., D_NOPE:], pos)
    kv = h @ w_dkv
    ckv = rms_norm(kv[..., :KV_LORA], kv_norm)
    kr = rope(kv[..., KV_LORA:], pos)
    return q_nope, q_rope, ckv, kr


def mla_prompt(h, w_dq, q_norm, w_uq, w_dkv, kv_norm, w_uk, w_uv, w_o):
    B, S, _ = h.shape
    pos_i = jnp.arange(S, dtype=jnp.int32)
    q_nope, q_rope, ckv, kr = mla_project(h, pos_i.astype(jnp.float32), w_dq, q_norm, w_uq, w_dkv, kv_norm)
    k_nope = jnp.einsum('bkc,chd->bkhd', ckv, w_uk)
    v = jnp.einsum('bkc,chd->bkhd', ckv, w_uv)
    outs = []
    for j in range(S // Q_BLOCK):
        q0, q1 = j * Q_BLOCK, (j + 1) * Q_BLOCK
        s = (jnp.einsum('bqhd,bkhd->bhqk', q_nope[:, q0:q1], k_nope[:, :q1])
             + jnp.einsum('bqhr,bkr->bhqk', q_rope[:, q0:q1], kr[:, :q1]))
        p = masked_softmax(s, chunk_mask(pos_i[q0:q1], pos_i[:q1])).astype(v.dtype)
        outs.append(jnp.einsum('bhqk,bkhd->bqhd', p, v[:, :q1]))
    o = jnp.concatenate(outs, axis=1).reshape(B, S, N_HEADS * D_V)
    return o @ w_o, ckv, kr


def mla_sample(h, cache_ckv, cache_kr, w_dq, q_norm, w_uq, w_dkv, kv_norm, w_uk, w_uv, w_o):
    B, S, _ = h.shape
    P = cache_ckv.shape[1]
    qpos = P + jnp.arange(S, dtype=jnp.int32)
    kpos = jnp.arange(P + S, dtype=jnp.int32)
    q_nope, q_rope, ckv, kr = mla_project(h, qpos.astype(jnp.float32), w_dq, q_norm, w_uq, w_dkv, kv_norm)
    ckv_all = jnp.concatenate([cache_ckv, ckv], axis=1)
    kr_all = jnp.concatenate([cache_kr, kr], axis=1)
    q_lat = jnp.einsum('bqhd,chd->bqhc', q_nope, w_uk)
    s = (jnp.einsum('bqhc,bkc->bhqk', q_lat, ckv_all)
         + jnp.einsum('bqhr,bkr->bhqk', q_rope, kr_all))
    p = masked_softmax(s, chunk_mask(qpos, kpos)).astype(ckv_all.dtype)
    o_lat = jnp.einsum('bhqk,bkc->bqhc', p, ckv_all)
    o = jnp.einsum('bqhc,chd->bqhd', o_lat, w_uv).reshape(B, S, N_HEADS * D_V)
    return o @ w_o, ckv, kr


def pool_mix(h, hist, pos0, w_pool, scale):
    B, S, D = h.shape
    Hh = hist.shape[1]
    xcat = jnp.concatenate([hist, h], axis=1)
    T = Hh + S
    pos = (pos0 - Hh) + jnp.arange(T, dtype=jnp.float32)
    xg = xcat.astype(jnp.float32).reshape(B, T, N_POOL_GROUPS, POOL_GROUP_DIM)
    cs = jnp.cumsum(xg, axis=1)
    means = []
    for g, w in enumerate(POOL_WINDOWS):
        csg = cs[:, :, g]
        lag = jnp.pad(csg, ((0, 0), (w, 0), (0, 0)))[:, :T]
        cnt = jnp.minimum(jnp.float32(w), pos + 1.0)
        means.append(((csg - lag) / cnt[None, :, None])[:, Hh:])
    mean = jnp.stack(means, axis=2)
    d = (mean - xg[:, Hh:]).astype(h.dtype)
    y = jnp.einsum('bsgc,gcd->bsgd', d, w_pool).reshape(B, S, D) * scale
    return y, xcat[:, T - POOL_HIST:]


def conv_ffn(h, hist, w_up, conv_w, conv_b, w_down):
    S = h.shape[1]
    up = h @ w_up
    upad = jnp.concatenate([hist, up], axis=1)
    c = conv_b
    for k in range(CONV_WIDTH):
        c = c + conv_w[k] * upad[:, k:k + S]
    gate, val = c[..., :D_FF], c[..., D_FF:]
    out = (jax.nn.silu(gate) * val) @ w_down
    return out, upad[:, upad.shape[1] - (CONV_WIDTH - 1):]


def setup_inputs(seed: int = 0) -> dict:
    key = jax.random.key(seed)
    ks = jax.random.split(key, 21)
    f32 = jnp.float32

    def nrm(k, shape, scale=1.0):
        return jax.random.normal(k, shape, f32) * scale

    NM, NP = N_MLA_LAYERS, N_POOL_LAYERS
    return {
        "x_prompt": nrm(ks[0], (BATCH, SEQ, D_MODEL)),
        "x_sample": nrm(ks[1], (DEC_BATCH, DEC_SEQ, D_MODEL)),
        "cache_ckv": nrm(ks[2], (NM, DEC_BATCH, PAST_LEN, KV_LORA)),
        "cache_krope": nrm(ks[3], (NM, DEC_BATCH, PAST_LEN, D_ROPE)),
        "state_pool": nrm(ks[4], (NP, DEC_BATCH, POOL_HIST, D_MODEL)),
        "state_conv": nrm(ks[5], (DEPTH, DEC_BATCH, CONV_WIDTH - 1, 2 * D_FF)),
        "norm_g": 1.0 + nrm(ks[6], (DEPTH, 4, D_MODEL), 0.05),
        "mla_w_dq": nrm(ks[7], (NM, D_MODEL, Q_LORA), D_MODEL ** -0.5),
        "mla_q_norm": 1.0 + nrm(ks[8], (NM, Q_LORA), 0.05),
        "mla_w_uq": nrm(ks[9], (NM, Q_LORA, N_HEADS * (D_NOPE + D_ROPE)), Q_LORA ** -0.5),
        "mla_w_dkv": nrm(ks[10], (NM, D_MODEL, KV_LORA + D_ROPE), D_MODEL ** -0.5),
        "mla_kv_norm": 1.0 + nrm(ks[11], (NM, KV_LORA), 0.05),
        "mla_w_uk": nrm(ks[12], (NM, KV_LORA, N_HEADS, D_NOPE), KV_LORA ** -0.5),
        "mla_w_uv": nrm(ks[13], (NM, KV_LORA, N_HEADS, D_V), KV_LORA ** -0.5),
        "mla_w_o": nrm(ks[14], (NM, N_HEADS * D_V, D_MODEL), (N_HEADS * D_V) ** -0.5),
        "pool_w": nrm(ks[15], (NP, N_POOL_GROUPS, POOL_GROUP_DIM, POOL_GROUP_DIM), POOL_GROUP_DIM ** -0.5),
        "pool_scale": 1.0 + nrm(ks[16], (NP, D_MODEL), 0.1),
        "ffn_w_up": nrm(ks[17], (DEPTH, D_MODEL, 2 * D_FF), D_MODEL ** -0.5),
        "ffn_conv_w": nrm(ks[18], (DEPTH, CONV_WIDTH, 2 * D_FF), CONV_WIDTH ** -0.5),
        "ffn_conv_b": nrm(ks[19], (DEPTH, 2 * D_FF), 0.01),
        "ffn_w_down": nrm(ks[20], (DEPTH, D_FF, D_MODEL), D_FF ** -0.5),
    }


def reference(x_prompt, x_sample, cache_ckv, cache_krope, state_pool, state_conv, norm_g,
              mla_w_dq, mla_q_norm, mla_w_uq, mla_w_dkv, mla_kv_norm, mla_w_uk, mla_w_uv, mla_w_o,
              pool_w, pool_scale, ffn_w_up, ffn_conv_w, ffn_conv_b, ffn_w_down):
    xp, xs = x_prompt, x_sample
    B = xp.shape[0]
    ckv_p, kr_p, ckv_s, kr_s = [], [], [], []
    pool_p, pool_s, conv_p, conv_s = [], [], [], []
    for i in range(DEPTH):
        g = norm_g[i]
        hp = rms_norm(xp, g[0])
        hs = rms_norm(xs, g[0])
        j = i // 2
        if i % 2 == 0:
            w = (mla_w_dq[j], mla_q_norm[j], mla_w_uq[j], mla_w_dkv[j], mla_kv_norm[j],
                 mla_w_uk[j], mla_w_uv[j], mla_w_o[j])
            mp, c_p, r_p = mla_prompt(hp, *w)
            ms, c_s, r_s = mla_sample(hs, cache_ckv[j], cache_krope[j], *w)
            ckv_p.append(c_p); kr_p.append(r_p); ckv_s.append(c_s); kr_s.append(r_s)
        else:
            mp, st_p = pool_mix(hp, hp[:, :0], 0, pool_w[j], pool_scale[j])
            ms, st_s = pool_mix(hs, state_pool[j], PAST_LEN, pool_w[j], pool_scale[j])
            pool_p.append(st_p); pool_s.append(st_s)
        xp = xp + rms_norm(mp, g[1])
        xs = xs + rms_norm(ms, g[1])
        hp = rms_norm(xp, g[2])
        hs = rms_norm(xs, g[2])
        zero_hist = jnp.zeros((B, CONV_WIDTH - 1, 2 * D_FF), hp.dtype)
        fp, cv_p = conv_ffn(hp, zero_hist, ffn_w_up[i], ffn_conv_w[i], ffn_conv_b[i], ffn_w_down[i])
        fs, cv_s = conv_ffn(hs, state_conv[i], ffn_w_up[i], ffn_conv_w[i], ffn_conv_b[i], ffn_w_down[i])
        conv_p.append(cv_p); conv_s.append(cv_s)
        xp = xp + rms_norm(fp, g[3])
        xs = xs + rms_norm(fs, g[3])
    return (xp, xs,
            jnp.stack(ckv_p), jnp.stack(kr_p), jnp.stack(pool_p), jnp.stack(conv_p),
            jnp.stack(ckv_s), jnp.stack(kr_s), jnp.stack(pool_s), jnp.stack(conv_s))
```

```python
import functools

import numpy as np
import jax
import jax.numpy as jnp
from jax import lax
from jax.experimental import pallas as pl
from jax.experimental.pallas import tpu as pltpu

F32 = jnp.float32
BF16 = jnp.bfloat16

CHUNK = 64
N_HEADS = 16
Q_LORA = 512
KV_LORA = 512
D_NOPE = 128
D_ROPE = 64
D_V = 128
ROPE_BASE = 10000.0
ATTN_SCALE = (D_NOPE + D_ROPE) ** -0.5
POOL_WINDOWS = (2, 4, 8, 16)
POOL_HIST = max(POOL_WINDOWS) - 1
CONV_WIDTH = 3
EPS = 1e-6

LANES = 128
SUBLANES = 8
HEAD_PAD = 2 * LANES
POOL_HALO = 16
VMEM_LIMIT = 56 * 1024 * 1024
NEG_BIG = -1e30


def _cparams(sem):
    return pltpu.CompilerParams(dimension_semantics=sem, vmem_limit_bytes=VMEM_LIMIT)


def _rms(xf, g):
    ms = jnp.mean(xf * xf, axis=-1, keepdims=True)
    return xf * lax.rsqrt(ms + EPS) * g


def _dot(a, b):
    return jnp.dot(a, b, preferred_element_type=F32)


def _dot_nt(a, b):
    return lax.dot_general(a, b, (((1,), (1,)), ((), ())), preferred_element_type=F32)


def _rope_cols(r, cos_t, sin_t):
    return r * cos_t + pltpu.roll(r, D_ROPE, axis=1) * sin_t


def _proj_kernel(x_ref, g_ref, w1_ref, qn_ref, kvn_ref, wuq_ref, cos_ref, sin_ref, *rest, with_kv):
    if with_kv:
        wuk_ref, wuv_ref, q_ref, ckv_ref, kr_ref, k_ref, v_ref = rest
    else:
        q_ref, ckv_ref, kr_ref = rest
    h = _rms(x_ref[...], g_ref[...]).astype(BF16)
    y = _dot(h, w1_ref[...])
    cq = _rms(y[:, :Q_LORA], qn_ref[...]).astype(BF16)
    ckv = _rms(y[:, Q_LORA:Q_LORA + KV_LORA], kvn_ref[...])
    ckv_ref[...] = ckv
    cos_t = cos_ref[...]
    sin_t = sin_ref[...]
    kr = _rope_cols(y[:, Q_LORA + KV_LORA:], cos_t, sin_t)
    kr_ref[...] = kr[:, :D_ROPE]
    q = _dot(cq, wuq_ref[...]) * ATTN_SCALE
    for hd in range(N_HEADS):
        b0 = hd * HEAD_PAD
        q_ref[:, b0:b0 + LANES] = q[:, b0:b0 + LANES].astype(BF16)
        q_ref[:, b0 + LANES:b0 + HEAD_PAD] = _rope_cols(
            q[:, b0 + LANES:b0 + HEAD_PAD], cos_t, sin_t).astype(BF16)
    if with_kv:
        ckv_b = ckv.astype(BF16)
        kr_b = kr.astype(BF16)
        kn = _dot(ckv_b, wuk_ref[...])
        for hd in range(N_HEADS):
            b0 = hd * HEAD_PAD
            k_ref[:, b0:b0 + LANES] = kn[:, hd * D_NOPE:(hd + 1) * D_NOPE].astype(BF16)
            k_ref[:, b0 + LANES:b0 + HEAD_PAD] = kr_b
        v_ref[...] = _dot(ckv_b, wuv_ref[...]).astype(BF16)


def _mla_project(x2d, g, w1, qn, kvn, wuq, cos_t, sin_t, wuk=None, wuv=None, *, bm):
    m, d = x2d.shape
    with_kv = wuk is not None
    nt = cos_t.shape[0] // bm
    row = lambda i: (i, 0)
    const = lambda i: (0, 0)
    in_specs = [
        pl.BlockSpec((bm, d), row),
        pl.BlockSpec((1, d), const),
        pl.BlockSpec(w1.shape, const),
        pl.BlockSpec((1, Q_LORA), const),
        pl.BlockSpec((1, KV_LORA), const),
        pl.BlockSpec(wuq.shape, const),
        pl.BlockSpec((bm, LANES), lambda i: (i % nt, 0)),
        pl.BlockSpec((bm, LANES), lambda i: (i % nt, 0)),
    ]
    args = [x2d, g, w1, qn, kvn, wuq, cos_t, sin_t]
    out_shape = [
        jax.ShapeDtypeStruct((m, N_HEADS * HEAD_PAD), BF16),
        jax.ShapeDtypeStruct((m, KV_LORA), F32),
        jax.ShapeDtypeStruct((m, D_ROPE), F32),
    ]
    out_specs = [
        pl.BlockSpec((bm, N_HEADS * HEAD_PAD), row),
        pl.BlockSpec((bm, KV_LORA), row),
        pl.BlockSpec((bm, D_ROPE), row),
    ]
    if with_kv:
        in_specs += [pl.BlockSpec(wuk.shape, const), pl.BlockSpec(wuv.shape, const)]
        args += [wuk, wuv]
        out_shape += [
            jax.ShapeDtypeStruct((m, N_HEADS * HEAD_PAD), BF16),
            jax.ShapeDtypeStruct((m, N_HEADS * D_V), BF16),
        ]
        out_specs += [
            pl.BlockSpec((bm, N_HEADS * HEAD_PAD), row),
            pl.BlockSpec((bm, N_HEADS * D_V), row),
        ]
    return pl.pallas_call(
        functools.partial(_proj_kernel, with_kv=with_kv),
        grid=(m // bm,),
        in_specs=in_specs,
        out_specs=out_specs,
        out_shape=out_shape,
        compiler_params=_cparams(("arbitrary",)),
        name="mla_project",
    )(*args)


def _attn_kernel(q_ref, k_ref, v_ref, o_ref, m_scr, l_scr, acc_scr, *, blk):
    qi = pl.program_id(2)
    q = q_ref[0]
    m_scr[...] = jnp.full(m_scr.shape, NEG_BIG, F32)
    l_scr[...] = jnp.zeros(l_scr.shape, F32)
    acc_scr[...] = jnp.zeros(acc_scr.shape, F32)

    def step(ki, masked):
        start = pl.multiple_of(ki * blk, blk)
        k = k_ref[0, pl.ds(start, blk), :]
        v = v_ref[0, pl.ds(start, blk), :]
        s = _dot_nt(q, k)
        if masked:
            rc = lax.broadcasted_iota(jnp.int32, s.shape, 0) // CHUNK
            cc = lax.broadcasted_iota(jnp.int32, s.shape, 1) // CHUNK
            s = jnp.where(cc <= rc, s, NEG_BIG)
        m_prev = m_scr[...]
        m_new = jnp.maximum(m_prev, jnp.max(s, axis=1, keepdims=True))
        alpha = jnp.exp(m_prev - m_new)
        p = jnp.exp(s - m_new)
        l_scr[...] = alpha * l_scr[...] + jnp.sum(p, axis=1, keepdims=True)
        acc_scr[...] = alpha * acc_scr[...] + _dot(p.astype(BF16), v)
        m_scr[...] = m_new

    def body(ki, carry):
        step(ki, False)
        return carry

    lax.fori_loop(0, qi, body, 0)
    step(qi, True)
    o_ref[0] = (acc_scr[...] / l_scr[...]).astype(o_ref.dtype)


def _attention(q, k, v, *, blk):
    b, s, _ = q.shape
    return pl.pallas_call(
        functools.partial(_attn_kernel, blk=blk),
        grid=(b, N_HEADS, s // blk),
        in_specs=[
            pl.BlockSpec((1, blk, HEAD_PAD), lambda bi, h, qi: (bi, qi, h)),
            pl.BlockSpec((1, s, HEAD_PAD), lambda bi, h, qi: (bi, 0, h)),
            pl.BlockSpec((1, s, D_V), lambda bi, h, qi: (bi, 0, h)),
        ],
        out_specs=pl.BlockSpec((1, blk, D_V), lambda bi, h, qi: (bi, qi, h)),
        out_shape=jax.ShapeDtypeStruct((b, s, N_HEADS * D_V), BF16),
        scratch_shapes=[
            pltpu.VMEM((blk, 1), F32),
            pltpu.VMEM((blk, 1), F32),
            pltpu.VMEM((blk, D_V), F32),
        ],
        compiler_params=_cparams(("arbitrary", "arbitrary", "arbitrary")),
        name="prompt_attention",
    )(q, k, v)


def _qlat_kernel(q_ref, wukt_ref, qlat_ref, qrope_ref):
    q = q_ref[...]
    qlat_ref[0] = _dot(q[:, :D_NOPE], wukt_ref[0]).astype(BF16)
    qrope_ref[0] = q[:, LANES:]


def _q_latent(q, wukt):
    m = q.shape[0]
    return pl.pallas_call(
        _qlat_kernel,
        grid=(N_HEADS,),
        in_specs=[
            pl.BlockSpec((m, HEAD_PAD), lambda h: (0, h)),
            pl.BlockSpec((1, D_NOPE, KV_LORA), lambda h: (h, 0, 0)),
        ],
        out_specs=[
            pl.BlockSpec((1, m, KV_LORA), lambda h: (h, 0, 0)),
            pl.BlockSpec((1, m, LANES), lambda h: (h, 0, 0)),
        ],
        out_shape=[
            jax.ShapeDtypeStruct((N_HEADS, m, KV_LORA), BF16),
            jax.ShapeDtypeStruct((N_HEADS, m, LANES), BF16),
        ],
        compiler_params=_cparams(("arbitrary",)),
        name="sample_q_latent",
    )(q, wukt)


def _sattn_kernel(qlat_ref, qrope_ref, cckv_ref, ckr_ref, nckv_ref, nkr_ref, o_ref, *, past, ns):
    rows = N_HEADS * ns
    ql = qlat_ref[...].reshape(rows, KV_LORA)
    qr = qrope_ref[...].reshape(rows, LANES)[:, :D_ROPE]
    cc = cckv_ref[0].astype(BF16)
    ck = ckr_ref[0].astype(BF16)
    nc = nckv_ref[...].astype(BF16)
    nk = nkr_ref[...].astype(BF16)
    s1 = _dot_nt(ql, cc) + _dot_nt(qr, ck)
    s2 = _dot_nt(ql, nc) + _dot_nt(qr, nk)
    qt = lax.broadcasted_iota(jnp.int32, s2.shape, 0) % ns
    kt = lax.broadcasted_iota(jnp.int32, s2.shape, 1)
    s2 = jnp.where((past + kt) // CHUNK <= (past + qt) // CHUNK, s2, NEG_BIG)
    m = jnp.maximum(jnp.max(s1, axis=1, keepdims=True), jnp.max(s2, axis=1, keepdims=True))
    p1 = jnp.exp(s1 - m)
    p2 = jnp.exp(s2 - m)
    l = jnp.sum(p1, axis=1, keepdims=True) + jnp.sum(p2, axis=1, keepdims=True)
    o = (_dot(p1.astype(BF16), cc) + _dot(p2.astype(BF16), nc)) / l
    o_ref[...] = o.astype(BF16).reshape(N_HEADS, ns, KV_LORA)


def _sample_attention(qlat, qrope, cache_ckv, cache_kr, new_ckv, new_kr, *, ns):
    nb, past, _ = cache_ckv.shape
    m = nb * ns
    return pl.pallas_call(
        functools.partial(_sattn_kernel, past=past, ns=ns),
        grid=(nb,),
        in_specs=[
            pl.BlockSpec((N_HEADS, ns, KV_LORA), lambda b: (0, b, 0)),
            pl.BlockSpec((N_HEADS, ns, LANES), lambda b: (0, b, 0)),
            pl.BlockSpec((1, past, KV_LORA), lambda b: (b, 0, 0)),
            pl.BlockSpec((1, past, D_ROPE), lambda b: (b, 0, 0)),
            pl.BlockSpec((ns, KV_LORA), lambda b: (b, 0)),
            pl.BlockSpec((ns, D_ROPE), lambda b: (b, 0)),
        ],
        out_specs=pl.BlockSpec((N_HEADS, ns, KV_LORA), lambda b: (0, b, 0)),
        out_shape=jax.ShapeDtypeStruct((N_HEADS, m, KV_LORA), BF16),
        compiler_params=_cparams(("arbitrary",)),
        name="sample_attention",
    )(qlat, qrope, cache_ckv, cache_kr, new_ckv, new_kr)


def _ouv_kernel(olat_ref, wuv_ref, o_ref):
    o_ref[...] = _dot(olat_ref[0], wuv_ref[...]).astype(BF16)


def _o_from_latent(olat, wuv):
    m = olat.shape[1]
    return pl.pallas_call(
        _ouv_kernel,
        grid=(N_HEADS,),
        in_specs=[
            pl.BlockSpec((1, m, KV_LORA), lambda h: (h, 0, 0)),
            pl.BlockSpec((KV_LORA, D_V), lambda h: (0, h)),
        ],
        out_specs=pl.BlockSpec((m, D_V), lambda h: (0, h)),
        out_shape=jax.ShapeDtypeStruct((m, N_HEADS * D_V), BF16),
        compiler_params=_cparams(("arbitrary",)),
        name="sample_o_from_latent",
    )(olat, wuv)


def _post_kernel(o_ref, w_ref, x_ref, g1_ref, g2_ref, xo_ref, h2_ref):
    xn = x_ref[...] + _rms(_dot(o_ref[...], w_ref[...]), g1_ref[...])
    xo_ref[...] = xn
    h2_ref[...] = _rms(xn, g2_ref[...]).astype(BF16)


def _attn_out(o2d, w_o, x2d, g1, g2, *, bm):
    m, d = x2d.shape
    row = lambda i: (i, 0)
    const = lambda i: (0, 0)
    return pl.pallas_call(
        _post_kernel,
        grid=(m // bm,),
        in_specs=[
            pl.BlockSpec((bm, o2d.shape[1]), row),
            pl.BlockSpec(w_o.shape, const),
            pl.BlockSpec((bm, d), row),
            pl.BlockSpec((1, d), const),
            pl.BlockSpec((1, d), const),
        ],
        out_specs=[pl.BlockSpec((bm, d), row), pl.BlockSpec((bm, d), row)],
        out_shape=[jax.ShapeDtypeStruct((m, d), F32), jax.ShapeDtypeStruct((m, d), BF16)],
        compiler_params=_cparams(("arbitrary",)),
        name="attn_out_post",
    )(o2d, w_o, x2d, g1, g2)


def _pool_kernel(x_ref, halo_ref, g0_ref, w_ref, sc_ref, g1_ref, g2_ref,
                 xo_ref, h2_ref, tail_ref, *, bm, blocks_per_seq, pos0, halo_normed):
    i = pl.program_id(0)
    x = x_ref[...]
    g0 = g0_ref[...]
    h = _rms(x, g0)
    halo = halo_ref[...]
    if not halo_normed:
        halo = _rms(halo, g0)
        halo = jnp.where(i % blocks_per_seq == 0, 0.0, halo)
    cat = jnp.concatenate([halo, h], axis=0)
    t = (i % blocks_per_seq) * bm + lax.broadcasted_iota(jnp.int32, (bm, 1), 0)
    posf = (t + pos0).astype(F32)
    gd = x.shape[1] // len(POOL_WINDOWS)
    ys = []
    for g, w in enumerate(POOL_WINDOWS):
        c = cat[:, g * gd:(g + 1) * gd]
        acc = c
        span = 1
        while span < w:
            acc = acc + pltpu.roll(acc, span, axis=0)
            span *= 2
        cnt = jnp.minimum(jnp.float32(w), posf + 1.0)
        mean = acc[POOL_HALO:] / cnt
        dlt = (mean - h[:, g * gd:(g + 1) * gd]).astype(BF16)
        ys.append(_dot(dlt, w_ref[g]))
    y = jnp.concatenate(ys, axis=1) * sc_ref[...]
    xn = x + _rms(y, g1_ref[...])
    xo_ref[...] = xn
    h2_ref[...] = _rms(xn, g2_ref[...]).astype(BF16)
    tail_ref[0] = h[bm - POOL_HALO:]


def _pool_layer(x2d, halo_src, g0, w_pool, scale, g1, g2, *, bm, rows_per_seq, pos0, halo_normed):
    m, d = x2d.shape
    bps = rows_per_seq // bm
    nseq = m // rows_per_seq
    row = lambda i: (i, 0)
    const = lambda i: (0, 0)
    if halo_normed:
        halo_map = lambda i: (i, 0)
    else:
        hb = bm // POOL_HALO
        halo_map = lambda i: (jnp.maximum(i * hb - 1, 0), 0)
    return pl.pallas_call(
        functools.partial(_pool_kernel, bm=bm, blocks_per_seq=bps, pos0=pos0, halo_normed=halo_normed),
        grid=(m // bm,),
        in_specs=[
            pl.BlockSpec((bm, d), row),
            pl.BlockSpec((POOL_HALO, d), halo_map),
            pl.BlockSpec((1, d), const),
            pl.BlockSpec(w_pool.shape, lambda i: (0, 0, 0)),
            pl.BlockSpec((1, d), const),
            pl.BlockSpec((1, d), const),
            pl.BlockSpec((1, d), const),
        ],
        out_specs=[
            pl.BlockSpec((bm, d), row),
            pl.BlockSpec((bm, d), row),
            pl.BlockSpec((1, POOL_HALO, d), lambda i: (i // bps, 0, 0)),
        ],
        out_shape=[
            jax.ShapeDtypeStruct((m, d), F32),
            jax.ShapeDtypeStruct((m, d), BF16),
            jax.ShapeDtypeStruct((nseq, POOL_HALO, d), F32),
        ],
        compiler_params=_cparams(("arbitrary",)),
        name="pool_layer",
    )(x2d, halo_src, g0, w_pool, scale, g1, g2)


def _conv3(u, prev1, prev2, cw_ref, cb_ref):
    return cb_ref[...] + cw_ref[0:1, :] * prev2 + cw_ref[1:2, :] * prev1 + cw_ref[2:3, :] * u


def _ffn_tail(j, nf, act, wd_ref, x_ref, g3_ref, xo_ref, acc_scr):
    part = _dot(act.astype(BF16), wd_ref[...])

    @pl.when(j == 0)
    def _():
        acc_scr[...] = part

    @pl.when(j > 0)
    def _():
        acc_scr[...] += part

    @pl.when(j == nf - 1)
    def _():
        xo_ref[...] = x_ref[...] + _rms(acc_scr[...], g3_ref[...])


def _ffn_prompt_kernel(h_ref, wg_ref, wv_ref, cwg_ref, cwv_ref, cbg_ref, cbv_ref, wd_ref, x_ref, g3_ref,
                       xo_ref, tg_ref, tv_ref, acc_scr, cg_scr, cv_scr, *, bm, blocks_per_seq, nf):
    i = pl.program_id(0)
    j = pl.program_id(1)
    h = h_ref[...]
    first = i % blocks_per_seq == 0

    def branch(w_ref, cw_ref, cb_ref, carry_scr, tail_ref):
        u = _dot(h, w_ref[...])
        prev8 = jnp.where(first, 0.0, carry_scr[j])
        c = _conv3(u, pltpu.roll(u, 1, axis=0), pltpu.roll(u, 2, axis=0), cw_ref, cb_ref)
        head = jnp.concatenate([prev8, u[:SUBLANES]], axis=0)
        c_head = _conv3(head, pltpu.roll(head, 1, axis=0), pltpu.roll(head, 2, axis=0),
                        cw_ref, cb_ref)[SUBLANES:]
        c = jnp.concatenate([c_head, c[SUBLANES:]], axis=0)
        last8 = u[bm - SUBLANES:]
        carry_scr[j] = last8
        tail_ref[0] = last8
        return c

    gate = branch(wg_ref, cwg_ref, cbg_ref, cg_scr, tg_ref)
    val = branch(wv_ref, cwv_ref, cbv_ref, cv_scr, tv_ref)
    act = gate * jax.nn.sigmoid(gate) * val
    _ffn_tail(j, nf, act, wd_ref, x_ref, g3_ref, xo_ref, acc_scr)


def _ffn_sample_kernel(h_ref, wg_ref, wv_ref, cwg_ref, cwv_ref, cbg_ref, cbv_ref, wd_ref, x_ref, g3_ref,
                       h1g_ref, h2g_ref, h1v_ref, h2v_ref, xo_ref, ug_ref, uv_ref, acc_scr, *, ns, nf):
    j = pl.program_id(0)
    h = h_ref[...]
    t = lax.broadcasted_iota(jnp.int32, (h.shape[0], 1), 0) % ns

    def branch(w_ref, cw_ref, cb_ref, h1_ref, h2_ref, u_ref):
        u = _dot(h, w_ref[...])
        u_ref[...] = u
        prev1 = jnp.where(t < 1, h1_ref[...], pltpu.roll(u, 1, axis=0))
        prev2 = jnp.where(t < 2, h2_ref[...], pltpu.roll(u, 2, axis=0))
        return _conv3(u, prev1, prev2, cw_ref, cb_ref)

    gate = branch(wg_ref, cwg_ref, cbg_ref, h1g_ref, h2g_ref, ug_ref)
    val = branch(wv_ref, cwv_ref, cbv_ref, h1v_ref, h2v_ref, uv_ref)
    act = gate * jax.nn.sigmoid(gate) * val
    _ffn_tail(j, nf, act, wd_ref, x_ref, g3_ref, xo_ref, acc_scr)


def _ffn_prompt(h2d, w_up, cw, cb, w_down, x2d, g3, *, bm, bf, rows_per_seq):
    m, d = x2d.shape
    dff = w_down.shape[0]
    nf = dff // bf
    bps = rows_per_seq // bm
    nseq = m // rows_per_seq
    xo, tg, tv = pl.pallas_call(
        functools.partial(_ffn_prompt_kernel, bm=bm, blocks_per_seq=bps, nf=nf),
        grid=(m // bm, nf),
        in_specs=[
            pl.BlockSpec((bm, d), lambda i, j: (i, 0)),
            pl.BlockSpec((d, bf), lambda i, j: (0, j)),
            pl.BlockSpec((d, bf), lambda i, j: (0, nf + j)),
            pl.BlockSpec((CONV_WIDTH, bf), lambda i, j: (0, j)),
            pl.BlockSpec((CONV_WIDTH, bf), lambda i, j: (0, nf + j)),
            pl.BlockSpec((1, bf), lambda i, j: (0, j)),
            pl.BlockSpec((1, bf), lambda i, j: (0, nf + j)),
            pl.BlockSpec((bf, d), lambda i, j: (j, 0)),
            pl.BlockSpec((bm, d), lambda i, j: (i, 0)),
            pl.BlockSpec((1, d), lambda i, j: (0, 0)),
        ],
        out_specs=[
            pl.BlockSpec((bm, d), lambda i, j: (i, 0)),
            pl.BlockSpec((1, SUBLANES, bf), lambda i, j: (i, 0, j)),
            pl.BlockSpec((1, SUBLANES, bf), lambda i, j: (i, 0, j)),
        ],
        out_shape=[
            jax.ShapeDtypeStruct((m, d), F32),
            jax.ShapeDtypeStruct((m // bm, SUBLANES, dff), F32),
            jax.ShapeDtypeStruct((m // bm, SUBLANES, dff), F32),
        ],
        scratch_shapes=[
            pltpu.VMEM((bm, d), F32),
            pltpu.VMEM((nf, SUBLANES, bf), F32),
            pltpu.VMEM((nf, SUBLANES, bf), F32),
        ],
        compiler_params=_cparams(("arbitrary", "arbitrary")),
        name="ffn_prompt",
    )(h2d, w_up, w_up, cw, cw, cb, cb, w_down, x2d, g3)
    k = CONV_WIDTH - 1
    tg = tg.reshape(nseq, bps, SUBLANES, dff)[:, bps - 1, SUBLANES - k:]
    tv = tv.reshape(nseq, bps, SUBLANES, dff)[:, bps - 1, SUBLANES - k:]
    return xo, jnp.concatenate([tg, tv], axis=-1)


def _ffn_sample(h2d, w_up, cw, cb, w_down, x2d, g3, state, *, bf, ns):
    m, d = x2d.shape
    dff = w_down.shape[0]
    nf = dff // bf
    nb = m // ns
    k = CONV_WIDTH - 1
    h1 = jnp.zeros((nb, ns, 2 * dff), F32).at[:, 0].set(state[:, k - 1]).reshape(m, 2 * dff)
    h2 = jnp.zeros((nb, ns, 2 * dff), F32).at[:, :k].set(state).reshape(m, 2 * dff)
    full = lambda j: (0, 0)
    colg = lambda j: (0, j)
    colv = lambda j: (0, nf + j)
    xo, ug, uv = pl.pallas_call(
        functools.partial(_ffn_sample_kernel, ns=ns, nf=nf),
        grid=(nf,),
        in_specs=[
            pl.BlockSpec((m, d), full),
            pl.BlockSpec((d, bf), colg),
            pl.BlockSpec((d, bf), colv),
            pl.BlockSpec((CONV_WIDTH, bf), colg),
            pl.BlockSpec((CONV_WIDTH, bf), colv),
            pl.BlockSpec((1, bf), colg),
            pl.BlockSpec((1, bf), colv),
            pl.BlockSpec((bf, d), lambda j: (j, 0)),
            pl.BlockSpec((m, d), full),
            pl.BlockSpec((1, d), full),
            pl.BlockSpec((m, bf), colg),
            pl.BlockSpec((m, bf), colg),
            pl.BlockSpec((m, bf), colv),
            pl.BlockSpec((m, bf), colv),
        ],
        out_specs=[
            pl.BlockSpec((m, d), full),
            pl.BlockSpec((m, bf), colg),
            pl.BlockSpec((m, bf), colg),
        ],
        out_shape=[
            jax.ShapeDtypeStruct((m, d), F32),
            jax.ShapeDtypeStruct((m, dff), F32),
            jax.ShapeDtypeStruct((m, dff), F32),
        ],
        scratch_shapes=[pltpu.VMEM((m, d), F32)],
        compiler_params=_cparams(("arbitrary",)),
        name="ffn_sample",
    )(h2d, w_up, w_up, cw, cw, cb, cb, w_down, x2d, g3, h1, h2, h1, h2)
    u = jnp.concatenate([ug, uv], axis=-1).reshape(nb, ns, 2 * dff)
    return xo, u[:, ns - k:]


def _rope_tables(pos, reps):
    half = D_ROPE // 2
    inv = ROPE_BASE ** (-jnp.arange(half, dtype=F32) / half)
    ang = pos[:, None] * inv[None, :]
    c, s = jnp.cos(ang), jnp.sin(ang)
    z = jnp.zeros((pos.shape[0], LANES - D_ROPE), F32)
    cos_t = jnp.concatenate([c, c, z], axis=1)
    sin_t = jnp.concatenate([s, s, z], axis=1)
    return jnp.tile(cos_t, (reps, 1)), jnp.tile(sin_t, (reps, 1))


def _signed_partner(w):
    half = D_ROPE // 2
    return jnp.concatenate([-w[..., half:], w[..., :half]], axis=-1)


def _stage_mla_weights(w_dq, w_uq, w_dkv, w_uk, w_uv, w_o):
    w_kr = w_dkv[:, KV_LORA:]
    w1 = jnp.concatenate([w_dq, w_dkv[:, :KV_LORA], w_kr, _signed_partner(w_kr)], axis=1).astype(BF16)
    wq = w_uq.reshape(Q_LORA, N_HEADS, D_NOPE + D_ROPE)
    wq_rope = wq[..., D_NOPE:]
    wuq = jnp.concatenate([wq[..., :D_NOPE], wq_rope, _signed_partner(wq_rope)], axis=-1)
    wuq = wuq.reshape(Q_LORA, N_HEADS * HEAD_PAD).astype(BF16)
    wuk = w_uk.reshape(KV_LORA, N_HEADS * D_NOPE).astype(BF16)
    wuv = w_uv.reshape(KV_LORA, N_HEADS * D_V).astype(BF16)
    wukt = jnp.transpose(w_uk, (1, 2, 0)).astype(BF16)
    return w1, wuq, wuk, wuv, wukt, w_o.astype(BF16)


def _block_rows(m, target):
    bm = min(m, target)
    while m % bm:
        bm //= 2
    return bm


def kernel(x_prompt, x_sample, cache_ckv, cache_krope, state_pool, state_conv, norm_g, mla_w_dq, mla_q_norm, mla_w_uq, mla_w_dkv, mla_kv_norm, mla_w_uk, mla_w_uv, mla_w_o, pool_w, pool_scale, ffn_w_up, ffn_conv_w, ffn_conv_b, ffn_w_down):
    nbp, sp, d = x_prompt.shape
    nbs, ss, _ = x_sample.shape
    past = cache_ckv.shape[2]
    depth = norm_g.shape[0]
    mp, ms = nbp * sp, nbs * ss
    xp = x_prompt.reshape(mp, d)
    xs = x_sample.reshape(ms, d)

    bm_proj = _block_rows(sp, 256)
    bm_post = _block_rows(sp, 512)
    bm_ffn = _block_rows(sp, 512)
    bm_pool = _block_rows(sp, 256)
    blk_attn = _block_rows(sp, 512)
    bf = 512
    bms = _block_rows(ms, 256)

    cos_p, sin_p = _rope_tables(jnp.arange(sp, dtype=jnp.int32).astype(F32), 1)
    cos_s, sin_s = _rope_tables((past + jnp.arange(ss, dtype=jnp.int32)).astype(F32), max(bms // ss, 1))

    outs = {k: [] for k in ("ckv_p", "kr_p", "pool_p", "conv_p", "ckv_s", "kr_s", "pool_s", "conv_s")}
    for i in range(depth):
        g = norm_g[i].reshape(4, 1, d)
        j = i // 2
        if i % 2 == 0:
            w1, wuq, wuk, wuv, wukt, wo = _stage_mla_weights(
                mla_w_dq[j], mla_w_uq[j], mla_w_dkv[j], mla_w_uk[j], mla_w_uv[j], mla_w_o[j])
            qn = mla_q_norm[j].reshape(1, Q_LORA)
            kvn = mla_kv_norm[j].reshape(1, KV_LORA)
            q, ckv, kr, k, v = _mla_project(xp, g[0], w1, qn, kvn, wuq, cos_p, sin_p, wuk, wuv, bm=bm_proj)
            o = _attention(q.reshape(nbp, sp, -1), k.reshape(nbp, sp, -1), v.reshape(nbp, sp, -1),
                           blk=blk_attn)
            xp, hp = _attn_out(o.reshape(mp, -1), wo, xp, g[1], g[2], bm=bm_post)
            outs["ckv_p"].append(ckv.reshape(nbp, sp, KV_LORA))
            outs["kr_p"].append(kr.reshape(nbp, sp, D_ROPE))
            qs, ckv_s, kr_s = _mla_project(xs, g[0], w1, qn, kvn, wuq, cos_s, sin_s, bm=bms)
            qlat, qrope = _q_latent(qs, wukt)
            olat = _sample_attention(qlat, qrope, cache_ckv[j], cache_krope[j], ckv_s, kr_s, ns=ss)
            os_ = _o_from_latent(olat, wuv)
            xs, hs = _attn_out(os_, wo, xs, g[1], g[2], bm=bms)
            outs["ckv_s"].append(ckv_s.reshape(nbs, ss, KV_LORA))
            outs["kr_s"].append(kr_s.reshape(nbs, ss, D_ROPE))
        else:
            wp = pool_w[j].astype(BF16)
            sc = pool_scale[j].reshape(1, d)
            xp_new, hp, tail_p = _pool_layer(xp, xp, g[0], wp, sc, g[1], g[2], bm=bm_pool,
                                             rows_per_seq=sp, pos0=0, halo_normed=False)
            xp = xp_new
            outs["pool_p"].append(tail_p[:, POOL_HALO - POOL_HIST:])
            hist = jnp.pad(state_pool[j], ((0, 0), (POOL_HALO - POOL_HIST, 0), (0, 0))).reshape(-1, d)
            xs, hs, tail_s = _pool_layer(xs, hist, g[0], wp, sc, g[1], g[2], bm=ss,
                                         rows_per_seq=ss, pos0=past, halo_normed=True)
            outs["pool_s"].append(tail_s[:, POOL_HALO - POOL_HIST:])
        w_up = ffn_w_up[i].astype(BF16)
        w_down = ffn_w_down[i].astype(BF16)
        cw = ffn_conv_w[i]
        cb = ffn_conv_b[i].reshape(1, -1)
        xp, cv_p = _ffn_prompt(hp, w_up, cw, cb, w_down, xp, g[3], bm=bm_ffn, bf=bf, rows_per_seq=sp)
        xs, cv_s = _ffn_sample(hs, w_up, cw, cb, w_down, xs, g[3], state_conv[i], bf=bf, ns=ss)
        outs["conv_p"].append(cv_p)
        outs["conv_s"].append(cv_s)
    st = lambda k: jnp.stack(outs[k])
    return (xp.reshape(nbp, sp, d), xs.reshape(nbs, ss, d),
            st("ckv_p"), st("kr_p"), st("pool_p"), st("conv_p"),
            st("ckv_s"), st("kr_s"), st("pool_s"), st("conv_s"))
```

```python
import functools

import numpy as np
import jax
import jax.numpy as jnp
from jax import lax
from jax.experimental import pallas as pl
from jax.experimental.pallas import tpu as pltpu

F32 = jnp.float32
BF16 = jnp.bfloat16

CHUNK = 64
N_HEADS = 16
Q_LORA = 512
KV_LORA = 512
D_NOPE = 128
D_ROPE = 64
D_V = 128
ROPE_BASE = 10000.0
ATTN_SCALE = (D_NOPE + D_ROPE) ** -0.5
Q_SCALE = ATTN_SCALE * float(np.log2(np.e))
POOL_WINDOWS = (2, 4, 8, 16)
POOL_HIST = max(POOL_WINDOWS) - 1
CONV_WIDTH = 3
EPS = 1e-6

LANES = 128
SUBLANES = 8
HEAD_PAD = 2 * LANES
POOL_HALO = 16
VMEM_LIMIT = 56 * 1024 * 1024
NEG_BIG = -1e30


def _cparams(sem):
    return pltpu.CompilerParams(dimension_semantics=sem, vmem_limit_bytes=VMEM_LIMIT)


def _rms(xf, g):
    ms = jnp.mean(xf * xf, axis=-1, keepdims=True)
    return xf * lax.rsqrt(ms + EPS) * g


def _dot(a, b):
    return jnp.dot(a, b, preferred_element_type=F32)


def _dot_nt(a, b):
    return lax.dot_general(a, b, (((1,), (1,)), ((), ())), preferred_element_type=F32)


def _rope_cols(r, cos_t, sin_t):
    return r * cos_t + pltpu.roll(r, D_ROPE, axis=1) * sin_t


def _proj_kernel(x_ref, g_ref, w1_ref, qn_ref, kvn_ref, wuq_ref, cos_ref, sin_ref, *rest, with_kv):
    if with_kv:
        wuk_ref, wuv_ref, q_ref, ckv_ref, kr_ref, k_ref, v_ref = rest
    else:
        q_ref, ckv_ref, kr_ref = rest
    h = _rms(x_ref[...], g_ref[...]).astype(BF16)
    y = _dot(h, w1_ref[...])
    cq = _rms(y[:, :Q_LORA], qn_ref[...]).astype(BF16)
    ckv = _rms(y[:, Q_LORA:Q_LORA + KV_LORA], kvn_ref[...])
    ckv_ref[...] = ckv
    cos_t = cos_ref[...]
    sin_t = sin_ref[...]
    kr = _rope_cols(y[:, Q_LORA + KV_LORA:], cos_t, sin_t)
    kr_ref[...] = kr[:, :D_ROPE]
    q = _dot(cq, wuq_ref[...]) * Q_SCALE
    for hd in range(N_HEADS):
        b0 = hd * HEAD_PAD
        q_ref[:, b0:b0 + LANES] = q[:, b0:b0 + LANES].astype(BF16)
        q_ref[:, b0 + LANES:b0 + HEAD_PAD] = _rope_cols(
            q[:, b0 + LANES:b0 + HEAD_PAD], cos_t, sin_t).astype(BF16)
    if with_kv:
        ckv_b = ckv.astype(BF16)
        kr_b = kr.astype(BF16)
        kn = _dot(ckv_b, wuk_ref[...])
        for hd in range(N_HEADS):
            b0 = hd * HEAD_PAD
            k_ref[:, b0:b0 + LANES] = kn[:, hd * D_NOPE:(hd + 1) * D_NOPE].astype(BF16)
            k_ref[:, b0 + LANES:b0 + HEAD_PAD] = kr_b
        v_ref[...] = _dot(ckv_b, wuv_ref[...]).astype(BF16)


def _mla_project(x2d, g, w1, qn, kvn, wuq, cos_t, sin_t, wuk=None, wuv=None, *, bm):
    m, d = x2d.shape
    with_kv = wuk is not None
    nt = cos_t.shape[0] // bm
    row = lambda i: (i, 0)
    const = lambda i: (0, 0)
    in_specs = [
        pl.BlockSpec((bm, d), row),
        pl.BlockSpec((1, d), const),
        pl.BlockSpec(w1.shape, const),
        pl.BlockSpec((1, Q_LORA), const),
        pl.BlockSpec((1, KV_LORA), const),
        pl.BlockSpec(wuq.shape, const),
        pl.BlockSpec((bm, LANES), lambda i: (i % nt, 0)),
        pl.BlockSpec((bm, LANES), lambda i: (i % nt, 0)),
    ]
    args = [x2d, g, w1, qn, kvn, wuq, cos_t, sin_t]
    out_shape = [
        jax.ShapeDtypeStruct((m, N_HEADS * HEAD_PAD), BF16),
        jax.ShapeDtypeStruct((m, KV_LORA), F32),
        jax.ShapeDtypeStruct((m, D_ROPE), F32),
    ]
    out_specs = [
        pl.BlockSpec((bm, N_HEADS * HEAD_PAD), row),
        pl.BlockSpec((bm, KV_LORA), row),
        pl.BlockSpec((bm, D_ROPE), row),
    ]
    if with_kv:
        in_specs += [pl.BlockSpec(wuk.shape, const), pl.BlockSpec(wuv.shape, const)]
        args += [wuk, wuv]
        out_shape += [
            jax.ShapeDtypeStruct((m, N_HEADS * HEAD_PAD), BF16),
            jax.ShapeDtypeStruct((m, N_HEADS * D_V), BF16),
        ]
        out_specs += [
            pl.BlockSpec((bm, N_HEADS * HEAD_PAD), row),
            pl.BlockSpec((bm, N_HEADS * D_V), row),
        ]
    return pl.pallas_call(
        functools.partial(_proj_kernel, with_kv=with_kv),
        grid=(m // bm,),
        in_specs=in_specs,
        out_specs=out_specs,
        out_shape=out_shape,
        compiler_params=_cparams(("arbitrary",)),
        name="mla_project",
    )(*args)


def _attn_kernel(q_ref, k_ref, v_ref, o_ref, m_scr, l_scr, acc_scr, *, blk, bk, sub):
    qi = pl.program_id(2)
    m_scr[...] = jnp.full(m_scr.shape, NEG_BIG, F32)
    l_scr[...] = jnp.zeros(l_scr.shape, F32)
    acc_scr[...] = jnp.zeros(acc_scr.shape, F32)

    def step(start, diag):
        for c in range(blk // sub):
            cs = slice(c * sub, (c + 1) * sub)
            nk = bk if diag is None else min(bk, (c + 1) * sub - diag)
            if nk <= 0:
                continue
            k = k_ref[0, pl.ds(start, nk), :]
            v = v_ref[0, pl.ds(start, nk), :]
            st = _dot_nt(k, q_ref[0, cs, :])
            if diag is not None and diag + nk > c * sub + CHUNK:
                kc = (lax.broadcasted_iota(jnp.int32, st.shape, 0) + diag) // CHUNK
                qc = (lax.broadcasted_iota(jnp.int32, st.shape, 1) + c * sub) // CHUNK
                st = jnp.where(kc <= qc, st, NEG_BIG)
            m_prev = m_scr[:, cs]
            m_new = jnp.maximum(m_prev, jnp.max(st, axis=0, keepdims=True))
            alpha = jnp.exp2(m_prev - m_new)
            p = jnp.exp2(st - m_new)
            l_scr[:, cs] = alpha * l_scr[:, cs] + jnp.sum(p, axis=0, keepdims=True)
            pv = lax.dot_general(v, p.astype(BF16), (((0,), (0,)), ((), ())),
                                 preferred_element_type=F32)
            acc_scr[:, cs] = alpha * acc_scr[:, cs] + pv
            m_scr[:, cs] = m_new

    def body(ki, carry):
        step(pl.multiple_of(ki * bk, bk), None)
        return carry

    per_q = blk // bk
    lax.fori_loop(0, qi * per_q, body, 0)
    for d in range(per_q):
        step(pl.multiple_of(qi * blk + d * bk, bk), d * bk)
    o_ref[0] = jnp.transpose(acc_scr[...] / l_scr[...]).astype(o_ref.dtype)


def _attention(q, k, v, *, blk, bk):
    b, s, _ = q.shape
    return pl.pallas_call(
        functools.partial(_attn_kernel, blk=blk, bk=bk, sub=min(bk, HEAD_PAD)),
        grid=(b, N_HEADS, s // blk),
        in_specs=[
            pl.BlockSpec((1, blk, HEAD_PAD), lambda bi, h, qi: (bi, qi, h)),
            pl.BlockSpec((1, s, HEAD_PAD), lambda bi, h, qi: (bi, 0, h)),
            pl.BlockSpec((1, s, D_V), lambda bi, h, qi: (bi, 0, h)),
        ],
        out_specs=pl.BlockSpec((1, blk, D_V), lambda bi, h, qi: (bi, qi, h)),
        out_shape=jax.ShapeDtypeStruct((b, s, N_HEADS * D_V), BF16),
        scratch_shapes=[
            pltpu.VMEM((1, blk), F32),
            pltpu.VMEM((1, blk), F32),
            pltpu.VMEM((D_V, blk), F32),
        ],
        compiler_params=_cparams(("arbitrary", "arbitrary", "arbitrary")),
        name="prompt_attention",
    )(q, k, v)


def _qlat_kernel(q_ref, wukt_ref, qlat_ref, qrope_ref):
    q = q_ref[...]
    qlat_ref[0] = _dot(q[:, :D_NOPE], wukt_ref[0]).astype(BF16)
    qrope_ref[0] = q[:, LANES:]


def _q_latent(q, wukt):
    m = q.shape[0]
    return pl.pallas_call(
        _qlat_kernel,
        grid=(N_HEADS,),
        in_specs=[
            pl.BlockSpec((m, HEAD_PAD), lambda h: (0, h)),
            pl.BlockSpec((1, D_NOPE, KV_LORA), lambda h: (h, 0, 0)),
        ],
        out_specs=[
            pl.BlockSpec((1, m, KV_LORA), lambda h: (h, 0, 0)),
            pl.BlockSpec((1, m, LANES), lambda h: (h, 0, 0)),
        ],
        out_shape=[
            jax.ShapeDtypeStruct((N_HEADS, m, KV_LORA), BF16),
            jax.ShapeDtypeStruct((N_HEADS, m, LANES), BF16),
        ],
        compiler_params=_cparams(("arbitrary",)),
        name="sample_q_latent",
    )(q, wukt)


def _sattn_kernel(qlat_ref, qrope_ref, cckv_ref, ckr_ref, nckv_ref, nkr_ref, o_ref, *, past, ns):
    rows = N_HEADS * ns
    ql = qlat_ref[...].reshape(rows, KV_LORA)
    qr = qrope_ref[...].reshape(rows, LANES)[:, :D_ROPE]
    cc = cckv_ref[0].astype(BF16)
    ck = ckr_ref[0].astype(BF16)
    nc = nckv_ref[...].astype(BF16)
    nk = nkr_ref[...].astype(BF16)
    s1 = _dot_nt(ql, cc) + _dot_nt(qr, ck)
    s2 = _dot_nt(ql, nc) + _dot_nt(qr, nk)
    qt = lax.broadcasted_iota(jnp.int32, s2.shape, 0) % ns
    kt = lax.broadcasted_iota(jnp.int32, s2.shape, 1)
    s2 = jnp.where((past + kt) // CHUNK <= (past + qt) // CHUNK, s2, NEG_BIG)
    m = jnp.maximum(jnp.max(s1, axis=1, keepdims=True), jnp.max(s2, axis=1, keepdims=True))
    p1 = jnp.exp2(s1 - m)
    p2 = jnp.exp2(s2 - m)
    l = jnp.sum(p1, axis=1, keepdims=True) + jnp.sum(p2, axis=1, keepdims=True)
    o = (_dot(p1.astype(BF16), cc) + _dot(p2.astype(BF16), nc)) / l
    o_ref[...] = o.astype(BF16).reshape(N_HEADS, ns, KV_LORA)


def _sample_attention(qlat, qrope, cache_ckv, cache_kr, new_ckv, new_kr, *, ns):
    nb, past, _ = cache_ckv.shape
    m = nb * ns
    return pl.pallas_call(
        functools.partial(_sattn_kernel, past=past, ns=ns),
        grid=(nb,),
        in_specs=[
            pl.BlockSpec((N_HEADS, ns, KV_LORA), lambda b: (0, b, 0)),
            pl.BlockSpec((N_HEADS, ns, LANES), lambda b: (0, b, 0)),
            pl.BlockSpec((1, past, KV_LORA), lambda b: (b, 0, 0)),
            pl.BlockSpec((1, past, D_ROPE), lambda b: (b, 0, 0)),
            pl.BlockSpec((ns, KV_LORA), lambda b: (b, 0)),
            pl.BlockSpec((ns, D_ROPE), lambda b: (b, 0)),
        ],
        out_specs=pl.BlockSpec((N_HEADS, ns, KV_LORA), lambda b: (0, b, 0)),
        out_shape=jax.ShapeDtypeStruct((N_HEADS, m, KV_LORA), BF16),
        compiler_params=_cparams(("arbitrary",)),
        name="sample_attention",
    )(qlat, qrope, cache_ckv, cache_kr, new_ckv, new_kr)


def _ouv_kernel(olat_ref, wuv_ref, o_ref):
    o_ref[...] = _dot(olat_ref[0], wuv_ref[...]).astype(BF16)


def _o_from_latent(olat, wuv):
    m = olat.shape[1]
    return pl.pallas_call(
        _ouv_kernel,
        grid=(N_HEADS,),
        in_specs=[
            pl.BlockSpec((1, m, KV_LORA), lambda h: (h, 0, 0)),
            pl.BlockSpec((KV_LORA, D_V), lambda h: (0, h)),
        ],
        out_specs=pl.BlockSpec((m, D_V), lambda h: (0, h)),
        out_shape=jax.ShapeDtypeStruct((m, N_HEADS * D_V), BF16),
        compiler_params=_cparams(("arbitrary",)),
        name="sample_o_from_latent",
    )(olat, wuv)


def _post_kernel(o_ref, w_ref, x_ref, g1_ref, g2_ref, xo_ref, h2_ref):
    xn = x_ref[...] + _rms(_dot(o_ref[...], w_ref[...]), g1_ref[...])
    xo_ref[...] = xn
    h2_ref[...] = _rms(xn, g2_ref[...]).astype(BF16)


def _attn_out(o2d, w_o, x2d, g1, g2, *, bm):
    m, d = x2d.shape
    row = lambda i: (i, 0)
    const = lambda i: (0, 0)
    return pl.pallas_call(
        _post_kernel,
        grid=(m // bm,),
        in_specs=[
            pl.BlockSpec((bm, o2d.shape[1]), row),
            pl.BlockSpec(w_o.shape, const),
            pl.BlockSpec((bm, d), row),
            pl.BlockSpec((1, d), const),
            pl.BlockSpec((1, d), const),
        ],
        out_specs=[pl.BlockSpec((bm, d), row), pl.BlockSpec((bm, d), row)],
        out_shape=[jax.ShapeDtypeStruct((m, d), F32), jax.ShapeDtypeStruct((m, d), BF16)],
        compiler_params=_cparams(("arbitrary",)),
        name="attn_out_post",
    )(o2d, w_o, x2d, g1, g2)


def _pool_kernel(x_ref, halo_ref, g0_ref, w_ref, sc_ref, g1_ref, g2_ref,
                 xo_ref, h2_ref, tail_ref, *, bm, blocks_per_seq, pos0, halo_normed):
    i = pl.program_id(0)
    x = x_ref[...]
    g0 = g0_ref[...]
    h = _rms(x, g0)
    halo = halo_ref[...]
    if not halo_normed:
        halo = _rms(halo, g0)
        halo = jnp.where(i % blocks_per_seq == 0, 0.0, halo)
    cat = jnp.concatenate([halo, h], axis=0)
    t = (i % blocks_per_seq) * bm + lax.broadcasted_iota(jnp.int32, (bm, 1), 0)
    posf = (t + pos0).astype(F32)
    gd = x.shape[1] // len(POOL_WINDOWS)
    ys = []
    for g, w in enumerate(POOL_WINDOWS):
        c = cat[:, g * gd:(g + 1) * gd]
        acc = c
        span = 1
        while span < w:
            acc = acc + pltpu.roll(acc, span, axis=0)
            span *= 2
        cnt = jnp.minimum(jnp.float32(w), posf + 1.0)
        mean = acc[POOL_HALO:] / cnt
        dlt = (mean - h[:, g * gd:(g + 1) * gd]).astype(BF16)
        ys.append(_dot(dlt, w_ref[g]))
    y = jnp.concatenate(ys, axis=1) * sc_ref[...]
    xn = x + _rms(y, g1_ref[...])
    xo_ref[...] = xn
    h2_ref[...] = _rms(xn, g2_ref[...]).astype(BF16)
    tail_ref[0] = h[bm - POOL_HALO:]


def _pool_layer(x2d, halo_src, g0, w_pool, scale, g1, g2, *, bm, rows_per_seq, pos0, halo_normed):
    m, d = x2d.shape
    bps = rows_per_seq // bm
    nseq = m // rows_per_seq
    row = lambda i: (i, 0)
    const = lambda i: (0, 0)
    if halo_normed:
        halo_map = lambda i: (i, 0)
    else:
        hb = bm // POOL_HALO
        halo_map = lambda i: (jnp.maximum(i * hb - 1, 0), 0)
    return pl.pallas_call(
        functools.partial(_pool_kernel, bm=bm, blocks_per_seq=bps, pos0=pos0, halo_normed=halo_normed),
        grid=(m // bm,),
        in_specs=[
            pl.BlockSpec((bm, d), row),
            pl.BlockSpec((POOL_HALO, d), halo_map),
            pl.BlockSpec((1, d), const),
            pl.BlockSpec(w_pool.shape, lambda i: (0, 0, 0)),
            pl.BlockSpec((1, d), const),
            pl.BlockSpec((1, d), const),
            pl.BlockSpec((1, d), const),
        ],
        out_specs=[
            pl.BlockSpec((bm, d), row),
            pl.BlockSpec((bm, d), row),
            pl.BlockSpec((1, POOL_HALO, d), lambda i: (i // bps, 0, 0)),
        ],
        out_shape=[
            jax.ShapeDtypeStruct((m, d), F32),
            jax.ShapeDtypeStruct((m, d), BF16),
            jax.ShapeDtypeStruct((nseq, POOL_HALO, d), F32),
        ],
        compiler_params=_cparams(("arbitrary",)),
        name="pool_layer",
    )(x2d, halo_src, g0, w_pool, scale, g1, g2)


def _conv3(u, prev1, prev2, cw_ref, cb_ref):
    return cb_ref[...] + cw_ref[0:1, :] * prev2 + cw_ref[1:2, :] * prev1 + cw_ref[2:3, :] * u


def _ffn_tail(j, nf, act, wd_ref, x_ref, g3_ref, xo_ref, acc_scr):
    part = _dot(act.astype(BF16), wd_ref[...])

    @pl.when(j == 0)
    def _():
        acc_scr[...] = part

    @pl.when(j > 0)
    def _():
        acc_scr[...] += part

    @pl.when(j == nf - 1)
    def _():
        xo_ref[...] = x_ref[...] + _rms(acc_scr[...], g3_ref[...])


def _ffn_prompt_kernel(h_ref, wg_ref, wv_ref, cwg_ref, cwv_ref, cbg_ref, cbv_ref, wd_ref, x_ref, g3_ref,
                       xo_ref, tg_ref, tv_ref, acc_scr, cg_scr, cv_scr, *, bm, blocks_per_seq, nf):
    i = pl.program_id(0)
    j = pl.program_id(1)
    h = h_ref[...]
    first = i % blocks_per_seq == 0

    def branch(w_ref, cw_ref, cb_ref, carry_scr, tail_ref):
        u = _dot(h, w_ref[...])
        prev8 = jnp.where(first, 0.0, carry_scr[j])
        c = _conv3(u, pltpu.roll(u, 1, axis=0), pltpu.roll(u, 2, axis=0), cw_ref, cb_ref)
        head = jnp.concatenate([prev8, u[:SUBLANES]], axis=0)
        c_head = _conv3(head, pltpu.roll(head, 1, axis=0), pltpu.roll(head, 2, axis=0),
                        cw_ref, cb_ref)[SUBLANES:]
        c = jnp.concatenate([c_head, c[SUBLANES:]], axis=0)
        last8 = u[bm - SUBLANES:]
        carry_scr[j] = last8
        tail_ref[0] = last8
        return c

    gate = branch(wg_ref, cwg_ref, cbg_ref, cg_scr, tg_ref)
    val = branch(wv_ref, cwv_ref, cbv_ref, cv_scr, tv_ref)
    act = gate * jax.nn.sigmoid(gate) * val
    _ffn_tail(j, nf, act, wd_ref, x_ref, g3_ref, xo_ref, acc_scr)


def _ffn_sample_kernel(h_ref, wg_ref, wv_ref, cwg_ref, cwv_ref, cbg_ref, cbv_ref, wd_ref, x_ref, g3_ref,
                       h1g_ref, h2g_ref, h1v_ref, h2v_ref, xo_ref, ug_ref, uv_ref, acc_scr, *, ns, nf):
    j = pl.program_id(0)
    h = h_ref[...]
    t = lax.broadcasted_iota(jnp.int32, (h.shape[0], 1), 0) % ns

    def branch(w_ref, cw_ref, cb_ref, h1_ref, h2_ref, u_ref):
        u = _dot(h, w_ref[...])
        u_ref[...] = u
        prev1 = jnp.where(t < 1, h1_ref[...], pltpu.roll(u, 1, axis=0))
        prev2 = jnp.where(t < 2, h2_ref[...], pltpu.roll(u, 2, axis=0))
        return _conv3(u, prev1, prev2, cw_ref, cb_ref)

    gate = branch(wg_ref, cwg_ref, cbg_ref, h1g_ref, h2g_ref, ug_ref)
    val = branch(wv_ref, cwv_ref, cbv_ref, h1v_ref, h2v_ref, uv_ref)
    act = gate * jax.nn.sigmoid(gate) * val
    _ffn_tail(j, nf, act, wd_ref, x_ref, g3_ref, xo_ref, acc_scr)


def _ffn_prompt(h2d, w_up, cw, cb, w_down, x2d, g3, *, bm, bf, rows_per_seq):
    m, d = x2d.shape
    dff = w_down.shape[0]
    nf = dff // bf
    bps = rows_per_seq // bm
    nseq = m // rows_per_seq
    xo, tg, tv = pl.pallas_call(
        functools.partial(_ffn_prompt_kernel, bm=bm, blocks_per_seq=bps, nf=nf),
        grid=(m // bm, nf),
        in_specs=[
            pl.BlockSpec((bm, d), lambda i, j: (i, 0)),
            pl.BlockSpec((d, bf), lambda i, j: (0, j)),
            pl.BlockSpec((d, bf), lambda i, j: (0, nf + j)),
            pl.BlockSpec((CONV_WIDTH, bf), lambda i, j: (0, j)),
            pl.BlockSpec((CONV_WIDTH, bf), lambda i, j: (0, nf + j)),
            pl.BlockSpec((1, bf), lambda i, j: (0, j)),
            pl.BlockSpec((1, bf), lambda i, j: (0, nf + j)),
            pl.BlockSpec((bf, d), lambda i, j: (j, 0)),
            pl.BlockSpec((bm, d), lambda i, j: (i, 0)),
            pl.BlockSpec((1, d), lambda i, j: (0, 0)),
        ],
        out_specs=[
            pl.BlockSpec((bm, d), lambda i, j: (i, 0)),
            pl.BlockSpec((1, SUBLANES, bf), lambda i, j: (i, 0, j)),
            pl.BlockSpec((1, SUBLANES, bf), lambda i, j: (i, 0, j)),
        ],
        out_shape=[
            jax.ShapeDtypeStruct((m, d), F32),
            jax.ShapeDtypeStruct((m // bm, SUBLANES, dff), F32),
            jax.ShapeDtypeStruct((m // bm, SUBLANES, dff), F32),
        ],
        scratch_shapes=[
            pltpu.VMEM((bm, d), F32),
            pltpu.VMEM((nf, SUBLANES, bf), F32),
            pltpu.VMEM((nf, SUBLANES, bf), F32),
        ],
        compiler_params=_cparams(("arbitrary", "arbitrary")),
        name="ffn_prompt",
    )(h2d, w_up, w_up, cw, cw, cb, cb, w_down, x2d, g3)
    k = CONV_WIDTH - 1
    tg = tg.reshape(nseq, bps, SUBLANES, dff)[:, bps - 1, SUBLANES - k:]
    tv = tv.reshape(nseq, bps, SUBLANES, dff)[:, bps - 1, SUBLANES - k:]
    return xo, jnp.concatenate([tg, tv], axis=-1)


def _ffn_sample(h2d, w_up, cw, cb, w_down, x2d, g3, state, *, bf, ns):
    m, d = x2d.shape
    dff = w_down.shape[0]
    nf = dff // bf
    nb = m // ns
    k = CONV_WIDTH - 1
    h1 = jnp.zeros((nb, ns, 2 * dff), F32).at[:, 0].set(state[:, k - 1]).reshape(m, 2 * dff)
    h2 = jnp.zeros((nb, ns, 2 * dff), F32).at[:, :k].set(state).reshape(m, 2 * dff)
    full = lambda j: (0, 0)
    colg = lambda j: (0, j)
    colv = lambda j: (0, nf + j)
    xo, ug, uv = pl.pallas_call(
        functools.partial(_ffn_sample_kernel, ns=ns, nf=nf),
        grid=(nf,),
        in_specs=[
            pl.BlockSpec((m, d), full),
            pl.BlockSpec((d, bf), colg),
            pl.BlockSpec((d, bf), colv),
            pl.BlockSpec((CONV_WIDTH, bf), colg),
            pl.BlockSpec((CONV_WIDTH, bf), colv),
            pl.BlockSpec((1, bf), colg),
            pl.BlockSpec((1, bf), colv),
            pl.BlockSpec((bf, d), lambda j: (j, 0)),
            pl.BlockSpec((m, d), full),
            pl.BlockSpec((1, d), full),
            pl.BlockSpec((m, bf), colg),
            pl.BlockSpec((m, bf), colg),
            pl.BlockSpec((m, bf), colv),
            pl.BlockSpec((m, bf), colv),
        ],
        out_specs=[
            pl.BlockSpec((m, d), full),
            pl.BlockSpec((m, bf), colg),
            pl.BlockSpec((m, bf), colg),
        ],
        out_shape=[
            jax.ShapeDtypeStruct((m, d), F32),
            jax.ShapeDtypeStruct((m, dff), F32),
            jax.ShapeDtypeStruct((m, dff), F32),
        ],
        scratch_shapes=[pltpu.VMEM((m, d), F32)],
        compiler_params=_cparams(("arbitrary",)),
        name="ffn_sample",
    )(h2d, w_up, w_up, cw, cw, cb, cb, w_down, x2d, g3, h1, h2, h1, h2)
    u = jnp.concatenate([ug, uv], axis=-1).reshape(nb, ns, 2 * dff)
    return xo, u[:, ns - k:]


def _rope_tables(pos, reps):
    half = D_ROPE // 2
    inv = ROPE_BASE ** (-jnp.arange(half, dtype=F32) / half)
    ang = pos[:, None] * inv[None, :]
    c, s = jnp.cos(ang), jnp.sin(ang)
    z = jnp.zeros((pos.shape[0], LANES - D_ROPE), F32)
    cos_t = jnp.concatenate([c, c, z], axis=1)
    sin_t = jnp.concatenate([s, s, z], axis=1)
    return jnp.tile(cos_t, (reps, 1)), jnp.tile(sin_t, (reps, 1))


def _signed_partner(w):
    half = D_ROPE // 2
    return jnp.concatenate([-w[..., half:], w[..., :half]], axis=-1)


def _stage_mla_weights(w_dq, w_uq, w_dkv, w_uk, w_uv, w_o):
    w_kr = w_dkv[:, KV_LORA:]
    w1 = jnp.concatenate([w_dq, w_dkv[:, :KV_LORA], w_kr, _signed_partner(w_kr)], axis=1).astype(BF16)
    wq = w_uq.reshape(Q_LORA, N_HEADS, D_NOPE + D_ROPE)
    wq_rope = wq[..., D_NOPE:]
    wuq = jnp.concatenate([wq[..., :D_NOPE], wq_rope, _signed_partner(wq_rope)], axis=-1)
    wuq = wuq.reshape(Q_LORA, N_HEADS * HEAD_PAD).astype(BF16)
    wuk = w_uk.reshape(KV_LORA, N_HEADS * D_NOPE).astype(BF16)
    wuv = w_uv.reshape(KV_LORA, N_HEADS * D_V).astype(BF16)
    wukt = jnp.transpose(w_uk, (1, 2, 0)).astype(BF16)
    return w1, wuq, wuk, wuv, wukt, w_o.astype(BF16)


def _block_rows(m, target):
    bm = min(m, target)
    while m % bm:
        bm //= 2
    return bm


def kernel(x_prompt, x_sample, cache_ckv, cache_krope, state_pool, state_conv, norm_g, mla_w_dq, mla_q_norm, mla_w_uq, mla_w_dkv, mla_kv_norm, mla_w_uk, mla_w_uv, mla_w_o, pool_w, pool_scale, ffn_w_up, ffn_conv_w, ffn_conv_b, ffn_w_down):
    nbp, sp, d = x_prompt.shape
    nbs, ss, _ = x_sample.shape
    past = cache_ckv.shape[2]
    depth = norm_g.shape[0]
    mp, ms = nbp * sp, nbs * ss
    xp = x_prompt.reshape(mp, d)
    xs = x_sample.reshape(ms, d)

    bm_proj = _block_rows(sp, 256)
    bm_post = _block_rows(sp, 512)
    bm_ffn = _block_rows(sp, 512)
    bm_pool = _block_rows(sp, 256)
    blk_attn = _block_rows(sp, 1024)
    bk_attn = _block_rows(blk_attn, 512)
    bf = 512
    bms = _block_rows(ms, 256)

    cos_p, sin_p = _rope_tables(jnp.arange(sp, dtype=jnp.int32).astype(F32), 1)
    cos_s, sin_s = _rope_tables((past + jnp.arange(ss, dtype=jnp.int32)).astype(F32), max(bms // ss, 1))

    outs = {k: [] for k in ("ckv_p", "kr_p", "pool_p", "conv_p", "ckv_s", "kr_s", "pool_s", "conv_s")}
    for i in range(depth):
        g = norm_g[i].reshape(4, 1, d)
        j = i // 2
        if i % 2 == 0:
            w1, wuq, wuk, wuv, wukt, wo = _stage_mla_weights(
                mla_w_dq[j], mla_w_uq[j], mla_w_dkv[j], mla_w_uk[j], mla_w_uv[j], mla_w_o[j])
            qn = mla_q_norm[j].reshape(1, Q_LORA)
            kvn = mla_kv_norm[j].reshape(1, KV_LORA)
            q, ckv, kr, k, v = _mla_project(xp, g[0], w1, qn, kvn, wuq, cos_p, sin_p, wuk, wuv, bm=bm_proj)
            o = _attention(q.reshape(nbp, sp, -1), k.reshape(nbp, sp, -1), v.reshape(nbp, sp, -1),
                           blk=blk_attn, bk=bk_attn)
            xp, hp = _attn_out(o.reshape(mp, -1), wo, xp, g[1], g[2], bm=bm_post)
            outs["ckv_p"].append(ckv.reshape(nbp, sp, KV_LORA))
            outs["kr_p"].append(kr.reshape(nbp, sp, D_ROPE))
            qs, ckv_s, kr_s = _mla_project(xs, g[0], w1, qn, kvn, wuq, cos_s, sin_s, bm=bms)
            qlat, qrope = _q_latent(qs, wukt)
            olat = _sample_attention(qlat, qrope, cache_ckv[j], cache_krope[j], ckv_s, kr_s, ns=ss)
            os_ = _o_from_latent(olat, wuv)
            xs, hs = _attn_out(os_, wo, xs, g[1], g[2], bm=bms)
            outs["ckv_s"].append(ckv_s.reshape(nbs, ss, KV_LORA))
            outs["kr_s"].append(kr_s.reshape(nbs, ss, D_ROPE))
        else:
            wp = pool_w[j].astype(BF16)
            sc = pool_scale[j].reshape(1, d)
            xp_new, hp, tail_p = _pool_layer(xp, xp, g[0], wp, sc, g[1], g[2], bm=bm_pool,
                                             rows_per_seq=sp, pos0=0, halo_normed=False)
            xp = xp_new
            outs["pool_p"].append(tail_p[:, POOL_HALO - POOL_HIST:])
            hist = jnp.pad(state_pool[j], ((0, 0), (POOL_HALO - POOL_HIST, 0), (0, 0))).reshape(-1, d)
            xs, hs, tail_s = _pool_layer(xs, hist, g[0], wp, sc, g[1], g[2], bm=ss,
                                         rows_per_seq=ss, pos0=past, halo_normed=True)
            outs["pool_s"].append(tail_s[:, POOL_HALO - POOL_HIST:])
        w_up = ffn_w_up[i].astype(BF16)
        w_down = ffn_w_down[i].astype(BF16)
        cw = ffn_conv_w[i]
        cb = ffn_conv_b[i].reshape(1, -1)
        xp, cv_p = _ffn_prompt(hp, w_up, cw, cb, w_down, xp, g[3], bm=bm_ffn, bf=bf, rows_per_seq=sp)
        xs, cv_s = _ffn_sample(hs, w_up, cw, cb, w_down, xs, g[3], state_conv[i], bf=bf, ns=ss)
        outs["conv_p"].append(cv_p)
        outs["conv_s"].append(cv_s)
    st = lambda k: jnp.stack(outs[k])
    return (xp.reshape(nbp, sp, d), xs.reshape(nbs, ss, d),
            st("ckv_p"), st("kr_p"), st("pool_p"), st("conv_p"),
            st("ckv_s"), st("kr_s"), st("pool_s"), st("conv_s"))
```

```python
import functools

import numpy as np
import jax
import jax.numpy as jnp
from jax import lax
from jax.experimental import pallas as pl
from jax.experimental.pallas import tpu as pltpu

F32 = jnp.float32
BF16 = jnp.bfloat16

CHUNK = 64
N_HEADS = 16
Q_LORA = 512
KV_LORA = 512
D_NOPE = 128
D_ROPE = 64
D_V = 128
ROPE_BASE = 10000.0
ATTN_SCALE = (D_NOPE + D_ROPE) ** -0.5
Q_SCALE = ATTN_SCALE * float(np.log2(np.e))
POOL_WINDOWS = (2, 4, 8, 16)
POOL_HIST = max(POOL_WINDOWS) - 1
CONV_WIDTH = 3
EPS = 1e-6

LANES = 128
SUBLANES = 8
HEAD_PAD = 2 * LANES
POOL_HALO = 16
VMEM_LIMIT = 56 * 1024 * 1024
NEG_BIG = -1e30


def _cparams(sem):
    return pltpu.CompilerParams(dimension_semantics=sem, vmem_limit_bytes=VMEM_LIMIT)


def _rms(xf, g):
    ms = jnp.mean(xf * xf, axis=-1, keepdims=True)
    return xf * lax.rsqrt(ms + EPS) * g


def _dot(a, b):
    return jnp.dot(a, b, preferred_element_type=F32)


def _dot_nt(a, b):
    return lax.dot_general(a, b, (((1,), (1,)), ((), ())), preferred_element_type=F32)


def _rope_cols(r, cos_t, sin_t):
    return r * cos_t + pltpu.roll(r, D_ROPE, axis=1) * sin_t


def _proj_kernel(x_ref, g_ref, w1_ref, qn_ref, kvn_ref, wuq_ref, cos_ref, sin_ref, *rest, with_kv):
    if with_kv:
        wuk_ref, wuv_ref, q_ref, ckv_ref, kr_ref, k_ref, v_ref = rest
    else:
        q_ref, ckv_ref, kr_ref = rest
    h = _rms(x_ref[...], g_ref[...]).astype(BF16)
    y = _dot(h, w1_ref[...])
    cq = _rms(y[:, :Q_LORA], qn_ref[...]).astype(BF16)
    ckv = _rms(y[:, Q_LORA:Q_LORA + KV_LORA], kvn_ref[...])
    ckv_ref[...] = ckv
    cos_t = cos_ref[...]
    sin_t = sin_ref[...]
    kr = _rope_cols(y[:, Q_LORA + KV_LORA:], cos_t, sin_t)
    kr_ref[...] = kr[:, :D_ROPE]
    q = _dot(cq, wuq_ref[...]) * Q_SCALE
    for hd in range(N_HEADS):
        b0 = hd * HEAD_PAD
        q_ref[:, b0:b0 + LANES] = q[:, b0:b0 + LANES].astype(BF16)
        q_ref[:, b0 + LANES:b0 + HEAD_PAD] = _rope_cols(
            q[:, b0 + LANES:b0 + HEAD_PAD], cos_t, sin_t).astype(BF16)
    if with_kv:
        ckv_b = ckv.astype(BF16)
        kr_b = kr.astype(BF16)
        kn = _dot(ckv_b, wuk_ref[...])
        for hd in range(N_HEADS):
            b0 = hd * HEAD_PAD
            k_ref[:, b0:b0 + LANES] = kn[:, hd * D_NOPE:(hd + 1) * D_NOPE].astype(BF16)
            k_ref[:, b0 + LANES:b0 + HEAD_PAD] = kr_b
        v_ref[...] = _dot(ckv_b, wuv_ref[...]).astype(BF16)


def _mla_project(x2d, g, w1, qn, kvn, wuq, cos_t, sin_t, wuk=None, wuv=None, *, bm):
    m, d = x2d.shape
    with_kv = wuk is not None
    nt = cos_t.shape[0] // bm
    row = lambda i: (i, 0)
    const = lambda i: (0, 0)
    in_specs = [
        pl.BlockSpec((bm, d), row),
        pl.BlockSpec((1, d), const),
        pl.BlockSpec(w1.shape, const),
        pl.BlockSpec((1, Q_LORA), const),
        pl.BlockSpec((1, KV_LORA), const),
        pl.BlockSpec(wuq.shape, const),
        pl.BlockSpec((bm, LANES), lambda i: (i % nt, 0)),
        pl.BlockSpec((bm, LANES), lambda i: (i % nt, 0)),
    ]
    args = [x2d, g, w1, qn, kvn, wuq, cos_t, sin_t]
    out_shape = [
        jax.ShapeDtypeStruct((m, N_HEADS * HEAD_PAD), BF16),
        jax.ShapeDtypeStruct((m, KV_LORA), F32),
        jax.ShapeDtypeStruct((m, D_ROPE), F32),
    ]
    out_specs = [
        pl.BlockSpec((bm, N_HEADS * HEAD_PAD), row),
        pl.BlockSpec((bm, KV_LORA), row),
        pl.BlockSpec((bm, D_ROPE), row),
    ]
    if with_kv:
        in_specs += [pl.BlockSpec(wuk.shape, const), pl.BlockSpec(wuv.shape, const)]
        args += [wuk, wuv]
        out_shape += [
            jax.ShapeDtypeStruct((m, N_HEADS * HEAD_PAD), BF16),
            jax.ShapeDtypeStruct((m, N_HEADS * D_V), BF16),
        ]
        out_specs += [
            pl.BlockSpec((bm, N_HEADS * HEAD_PAD), row),
            pl.BlockSpec((bm, N_HEADS * D_V), row),
        ]
    return pl.pallas_call(
        functools.partial(_proj_kernel, with_kv=with_kv),
        grid=(m // bm,),
        in_specs=in_specs,
        out_specs=out_specs,
        out_shape=out_shape,
        compiler_params=_cparams(("arbitrary",)),
        name="mla_project",
    )(*args)


def _attn_kernel(q_ref, k_ref, v_ref, o_ref, m_scr, l_scr, acc_scr, *, blk, bk, sub):
    qi = pl.program_id(2)
    m_scr[...] = jnp.full(m_scr.shape, NEG_BIG, F32)
    l_scr[...] = jnp.zeros(l_scr.shape, F32)
    acc_scr[...] = jnp.zeros(acc_scr.shape, F32)

    def step(start, diag):
        scores = []
        for c in range(blk // sub):
            nk = bk if diag is None else min(bk, (c + 1) * sub - diag)
            if nk <= 0:
                continue
            k = k_ref[0, pl.ds(start, nk), :]
            st = _dot_nt(k, q_ref[0, c * sub:(c + 1) * sub, :])
            if diag is not None and diag + nk > c * sub + CHUNK:
                kc = (lax.broadcasted_iota(jnp.int32, st.shape, 0) + diag) // CHUNK
                qc = (lax.broadcasted_iota(jnp.int32, st.shape, 1) + c * sub) // CHUNK
                st = jnp.where(kc <= qc, st, NEG_BIG)
            scores.append((c, nk, st))
        for c, nk, st in scores:
            cs = slice(c * sub, (c + 1) * sub)
            v = v_ref[0, pl.ds(start, nk), :]
            m_prev = m_scr[:, cs]
            m_new = jnp.maximum(m_prev, jnp.max(st, axis=0, keepdims=True))
            alpha = jnp.exp2(m_prev - m_new)
            p = jnp.exp2(st - m_new)
            l_scr[:, cs] = alpha * l_scr[:, cs] + jnp.sum(p, axis=0, keepdims=True)
            pv = lax.dot_general(v, p.astype(BF16), (((0,), (0,)), ((), ())),
                                 preferred_element_type=F32)
            acc_scr[:, cs] = alpha * acc_scr[:, cs] + pv
            m_scr[:, cs] = m_new

    def body(ki, carry):
        step(pl.multiple_of(ki * bk, bk), None)
        return carry

    per_q = blk // bk
    lax.fori_loop(0, qi * per_q, body, 0)
    for d in range(per_q):
        step(pl.multiple_of(qi * blk + d * bk, bk), d * bk)
    o_ref[0] = jnp.transpose(acc_scr[...] / l_scr[...]).astype(o_ref.dtype)


def _attention(q, k, v, *, blk, bk):
    b, s, _ = q.shape
    return pl.pallas_call(
        functools.partial(_attn_kernel, blk=blk, bk=bk, sub=min(bk, HEAD_PAD)),
        grid=(b, N_HEADS, s // blk),
        in_specs=[
            pl.BlockSpec((1, blk, HEAD_PAD), lambda bi, h, qi: (bi, qi, h)),
            pl.BlockSpec((1, s, HEAD_PAD), lambda bi, h, qi: (bi, 0, h)),
            pl.BlockSpec((1, s, D_V), lambda bi, h, qi: (bi, 0, h)),
        ],
        out_specs=pl.BlockSpec((1, blk, D_V), lambda bi, h, qi: (bi, qi, h)),
        out_shape=jax.ShapeDtypeStruct((b, s, N_HEADS * D_V), BF16),
        scratch_shapes=[
            pltpu.VMEM((1, blk), F32),
            pltpu.VMEM((1, blk), F32),
            pltpu.VMEM((D_V, blk), F32),
        ],
        compiler_params=_cparams(("arbitrary", "arbitrary", "arbitrary")),
        name="prompt_attention",
    )(q, k, v)


def _qlat_kernel(q_ref, wukt_ref, qlat_ref, qrope_ref):
    q = q_ref[...]
    qlat_ref[0] = _dot(q[:, :D_NOPE], wukt_ref[0]).astype(BF16)
    qrope_ref[0] = q[:, LANES:]


def _q_latent(q, wukt):
    m = q.shape[0]
    return pl.pallas_call(
        _qlat_kernel,
        grid=(N_HEADS,),
        in_specs=[
            pl.BlockSpec((m, HEAD_PAD), lambda h: (0, h)),
            pl.BlockSpec((1, D_NOPE, KV_LORA), lambda h: (h, 0, 0)),
        ],
        out_specs=[
            pl.BlockSpec((1, m, KV_LORA), lambda h: (h, 0, 0)),
            pl.BlockSpec((1, m, LANES), lambda h: (h, 0, 0)),
        ],
        out_shape=[
            jax.ShapeDtypeStruct((N_HEADS, m, KV_LORA), BF16),
            jax.ShapeDtypeStruct((N_HEADS, m, LANES), BF16),
        ],
        compiler_params=_cparams(("arbitrary",)),
        name="sample_q_latent",
    )(q, wukt)


def _sattn_kernel(qlat_ref, qrope_ref, cckv_ref, ckr_ref, nckv_ref, nkr_ref, o_ref, *, past, ns):
    rows = N_HEADS * ns
    ql = qlat_ref[...].reshape(rows, KV_LORA)
    qr = qrope_ref[...].reshape(rows, LANES)[:, :D_ROPE]
    cc = cckv_ref[0].astype(BF16)
    ck = ckr_ref[0].astype(BF16)
    nc = nckv_ref[...].astype(BF16)
    nk = nkr_ref[...].astype(BF16)
    s1 = _dot_nt(ql, cc) + _dot_nt(qr, ck)
    s2 = _dot_nt(ql, nc) + _dot_nt(qr, nk)
    qt = lax.broadcasted_iota(jnp.int32, s2.shape, 0) % ns
    kt = lax.broadcasted_iota(jnp.int32, s2.shape, 1)
    s2 = jnp.where((past + kt) // CHUNK <= (past + qt) // CHUNK, s2, NEG_BIG)
    m = jnp.maximum(jnp.max(s1, axis=1, keepdims=True), jnp.max(s2, axis=1, keepdims=True))
    p1 = jnp.exp2(s1 - m)
    p2 = jnp.exp2(s2 - m)
    l = jnp.sum(p1, axis=1, keepdims=True) + jnp.sum(p2, axis=1, keepdims=True)
    o = (_dot(p1.astype(BF16), cc) + _dot(p2.astype(BF16), nc)) / l
    o_ref[...] = o.astype(BF16).reshape(N_HEADS, ns, KV_LORA)


def _sample_attention(qlat, qrope, cache_ckv, cache_kr, new_ckv, new_kr, *, ns):
    nb, past, _ = cache_ckv.shape
    m = nb * ns
    return pl.pallas_call(
        functools.partial(_sattn_kernel, past=past, ns=ns),
        grid=(nb,),
        in_specs=[
            pl.BlockSpec((N_HEADS, ns, KV_LORA), lambda b: (0, b, 0)),
            pl.BlockSpec((N_HEADS, ns, LANES), lambda b: (0, b, 0)),
            pl.BlockSpec((1, past, KV_LORA), lambda b: (b, 0, 0)),
            pl.BlockSpec((1, past, D_ROPE), lambda b: (b, 0, 0)),
            pl.BlockSpec((ns, KV_LORA), lambda b: (b, 0)),
            pl.BlockSpec((ns, D_ROPE), lambda b: (b, 0)),
        ],
        out_specs=pl.BlockSpec((N_HEADS, ns, KV_LORA), lambda b: (0, b, 0)),
        out_shape=jax.ShapeDtypeStruct((N_HEADS, m, KV_LORA), BF16),
        compiler_params=_cparams(("arbitrary",)),
        name="sample_attention",
    )(qlat, qrope, cache_ckv, cache_kr, new_ckv, new_kr)


def _ouv_kernel(olat_ref, wuv_ref, o_ref):
    o_ref[...] = _dot(olat_ref[0], wuv_ref[...]).astype(BF16)


def _o_from_latent(olat, wuv):
    m = olat.shape[1]
    return pl.pallas_call(
        _ouv_kernel,
        grid=(N_HEADS,),
        in_specs=[
            pl.BlockSpec((1, m, KV_LORA), lambda h: (h, 0, 0)),
            pl.BlockSpec((KV_LORA, D_V), lambda h: (0, h)),
        ],
        out_specs=pl.BlockSpec((m, D_V), lambda h: (0, h)),
        out_shape=jax.ShapeDtypeStruct((m, N_HEADS * D_V), BF16),
        compiler_params=_cparams(("arbitrary",)),
        name="sample_o_from_latent",
    )(olat, wuv)


def _post_kernel(o_ref, w_ref, x_ref, g1_ref, g2_ref, xo_ref, h2_ref):
    xn = x_ref[...] + _rms(_dot(o_ref[...], w_ref[...]), g1_ref[...])
    xo_ref[...] = xn
    h2_ref[...] = _rms(xn, g2_ref[...]).astype(BF16)


def _attn_out(o2d, w_o, x2d, g1, g2, *, bm):
    m, d = x2d.shape
    row = lambda i: (i, 0)
    const = lambda i: (0, 0)
    return pl.pallas_call(
        _post_kernel,
        grid=(m // bm,),
        in_specs=[
            pl.BlockSpec((bm, o2d.shape[1]), row),
            pl.BlockSpec(w_o.shape, const),
            pl.BlockSpec((bm, d), row),
            pl.BlockSpec((1, d), const),
            pl.BlockSpec((1, d), const),
        ],
        out_specs=[pl.BlockSpec((bm, d), row), pl.BlockSpec((bm, d), row)],
        out_shape=[jax.ShapeDtypeStruct((m, d), F32), jax.ShapeDtypeStruct((m, d), BF16)],
        compiler_params=_cparams(("arbitrary",)),
        name="attn_out_post",
    )(o2d, w_o, x2d, g1, g2)


def _pool_kernel(x_ref, halo_ref, g0_ref, w_ref, sc_ref, g1_ref, g2_ref,
                 xo_ref, h2_ref, tail_ref, *, bm, blocks_per_seq, pos0, halo_normed):
    i = pl.program_id(0)
    x = x_ref[...]
    g0 = g0_ref[...]
    h = _rms(x, g0)
    halo = halo_ref[...]
    if not halo_normed:
        halo = _rms(halo, g0)
        halo = jnp.where(i % blocks_per_seq == 0, 0.0, halo)
    cat = jnp.concatenate([halo, h], axis=0)
    t = (i % blocks_per_seq) * bm + lax.broadcasted_iota(jnp.int32, (bm, 1), 0)
    posf = (t + pos0).astype(F32)
    gd = x.shape[1] // len(POOL_WINDOWS)
    ys = []
    for g, w in enumerate(POOL_WINDOWS):
        c = cat[:, g * gd:(g + 1) * gd]
        acc = c
        span = 1
        while span < w:
            acc = acc + pltpu.roll(acc, span, axis=0)
            span *= 2
        cnt = jnp.minimum(jnp.float32(w), posf + 1.0)
        mean = acc[POOL_HALO:] / cnt
        dlt = (mean - h[:, g * gd:(g + 1) * gd]).astype(BF16)
        ys.append(_dot(dlt, w_ref[g]))
    y = jnp.concatenate(ys, axis=1) * sc_ref[...]
    xn = x + _rms(y, g1_ref[...])
    xo_ref[...] = xn
    h2_ref[...] = _rms(xn, g2_ref[...]).astype(BF16)
    tail_ref[0] = h[bm - POOL_HALO:]


def _pool_layer(x2d, halo_src, g0, w_pool, scale, g1, g2, *, bm, rows_per_seq, pos0, halo_normed):
    m, d = x2d.shape
    bps = rows_per_seq // bm
    nseq = m // rows_per_seq
    row = lambda i: (i, 0)
    const = lambda i: (0, 0)
    if halo_normed:
        halo_map = lambda i: (i, 0)
    else:
        hb = bm // POOL_HALO
        halo_map = lambda i: (jnp.maximum(i * hb - 1, 0), 0)
    return pl.pallas_call(
        functools.partial(_pool_kernel, bm=bm, blocks_per_seq=bps, pos0=pos0, halo_normed=halo_normed),
        grid=(m // bm,),
        in_specs=[
            pl.BlockSpec((bm, d), row),
            pl.BlockSpec((POOL_HALO, d), halo_map),
            pl.BlockSpec((1, d), const),
            pl.BlockSpec(w_pool.shape, lambda i: (0, 0, 0)),
            pl.BlockSpec((1, d), const),
            pl.BlockSpec((1, d), const),
            pl.BlockSpec((1, d), const),
        ],
        out_specs=[
            pl.BlockSpec((bm, d), row),
            pl.BlockSpec((bm, d), row),
            pl.BlockSpec((1, POOL_HALO, d), lambda i: (i // bps, 0, 0)),
        ],
        out_shape=[
            jax.ShapeDtypeStruct((m, d), F32),
            jax.ShapeDtypeStruct((m, d), BF16),
            jax.ShapeDtypeStruct((nseq, POOL_HALO, d), F32),
        ],
        compiler_params=_cparams(("arbitrary",)),
        name="pool_layer",
    )(x2d, halo_src, g0, w_pool, scale, g1, g2)


def _conv3(u, prev1, prev2, cw_ref, cb_ref):
    return cb_ref[...] + cw_ref[0:1, :] * prev2 + cw_ref[1:2, :] * prev1 + cw_ref[2:3, :] * u


def _ffn_tail(j, nf, act, wd_ref, x_ref, g3_ref, xo_ref, acc_scr):
    part = _dot(act.astype(BF16), wd_ref[...])

    @pl.when(j == 0)
    def _():
        acc_scr[...] = part

    @pl.when(j > 0)
    def _():
        acc_scr[...] += part

    @pl.when(j == nf - 1)
    def _():
        xo_ref[...] = x_ref[...] + _rms(acc_scr[...], g3_ref[...])


def _ffn_prompt_kernel(h_ref, wg_ref, wv_ref, cwg_ref, cwv_ref, cbg_ref, cbv_ref, wd_ref,
                       f_ref, tg_ref, tv_ref, cg_scr, cv_scr, *, bm, blocks_per_seq):
    i = pl.program_id(0)
    j = pl.program_id(1)
    h = h_ref[...]
    first = i % blocks_per_seq == 0

    def branch(w_ref, cw_ref, cb_ref, carry_scr, tail_ref):
        u = _dot(h, w_ref[...])
        prev8 = jnp.where(first, 0.0, carry_scr[j])
        c = _conv3(u, pltpu.roll(u, 1, axis=0), pltpu.roll(u, 2, axis=0), cw_ref, cb_ref)
        head = jnp.concatenate([prev8, u[:SUBLANES]], axis=0)
        c_head = _conv3(head, pltpu.roll(head, 1, axis=0), pltpu.roll(head, 2, axis=0),
                        cw_ref, cb_ref)[SUBLANES:]
        c = jnp.concatenate([c_head, c[SUBLANES:]], axis=0)
        last8 = u[bm - SUBLANES:]
        carry_scr[j] = last8
        tail_ref[0] = last8
        return c

    gate = branch(wg_ref, cwg_ref, cbg_ref, cg_scr, tg_ref)
    val = branch(wv_ref, cwv_ref, cbv_ref, cv_scr, tv_ref)
    act = gate * jax.nn.sigmoid(gate) * val
    part = _dot(act.astype(BF16), wd_ref[...])

    @pl.when(j == 0)
    def _():
        f_ref[...] = part

    @pl.when(j > 0)
    def _():
        f_ref[...] += part


def _resid_kernel(f_ref, x_ref, g_ref, xo_ref):
    xo_ref[...] = x_ref[...] + _rms(f_ref[...], g_ref[...])


def _residual_norm(f2d, x2d, g, *, bm):
    m, d = x2d.shape
    row = lambda i: (i, 0)
    return pl.pallas_call(
        _resid_kernel,
        grid=(m // bm,),
        in_specs=[pl.BlockSpec((bm, d), row), pl.BlockSpec((bm, d), row), pl.BlockSpec((1, d), lambda i: (0, 0))],
        out_specs=pl.BlockSpec((bm, d), row),
        out_shape=jax.ShapeDtypeStruct((m, d), F32),
        compiler_params=_cparams(("arbitrary",)),
        name="residual_norm",
    )(f2d, x2d, g)


def _ffn_sample_kernel(h_ref, wg_ref, wv_ref, cwg_ref, cwv_ref, cbg_ref, cbv_ref, wd_ref, x_ref, g3_ref,
                       h1g_ref, h2g_ref, h1v_ref, h2v_ref, xo_ref, ug_ref, uv_ref, acc_scr, *, ns, nf):
    j = pl.program_id(0)
    h = h_ref[...]
    t = lax.broadcasted_iota(jnp.int32, (h.shape[0], 1), 0) % ns

    def branch(w_ref, cw_ref, cb_ref, h1_ref, h2_ref, u_ref):
        u = _dot(h, w_ref[...])
        u_ref[...] = u
        prev1 = jnp.where(t < 1, h1_ref[...], pltpu.roll(u, 1, axis=0))
        prev2 = jnp.where(t < 2, h2_ref[...], pltpu.roll(u, 2, axis=0))
        return _conv3(u, prev1, prev2, cw_ref, cb_ref)

    gate = branch(wg_ref, cwg_ref, cbg_ref, h1g_ref, h2g_ref, ug_ref)
    val = branch(wv_ref, cwv_ref, cbv_ref, h1v_ref, h2v_ref, uv_ref)
    act = gate * jax.nn.sigmoid(gate) * val
    _ffn_tail(j, nf, act, wd_ref, x_ref, g3_ref, xo_ref, acc_scr)


def _ffn_prompt(h2d, w_up, cw, cb, w_down, x2d, g3, *, bm, bf, rows_per_seq):
    m, d = x2d.shape
    dff = w_down.shape[0]
    nf = dff // bf
    bps = rows_per_seq // bm
    nseq = m // rows_per_seq
    f, tg, tv = pl.pallas_call(
        functools.partial(_ffn_prompt_kernel, bm=bm, blocks_per_seq=bps),
        grid=(m // bm, nf),
        in_specs=[
            pl.BlockSpec((bm, d), lambda i, j: (i, 0)),
            pl.BlockSpec((d, bf), lambda i, j: (0, j)),
            pl.BlockSpec((d, bf), lambda i, j: (0, nf + j)),
            pl.BlockSpec((CONV_WIDTH, bf), lambda i, j: (0, j)),
            pl.BlockSpec((CONV_WIDTH, bf), lambda i, j: (0, nf + j)),
            pl.BlockSpec((1, bf), lambda i, j: (0, j)),
            pl.BlockSpec((1, bf), lambda i, j: (0, nf + j)),
            pl.BlockSpec((bf, d), lambda i, j: (j, 0)),
        ],
        out_specs=[
            pl.BlockSpec((bm, d), lambda i, j: (i, 0)),
            pl.BlockSpec((1, SUBLANES, bf), lambda i, j: (i, 0, j)),
            pl.BlockSpec((1, SUBLANES, bf), lambda i, j: (i, 0, j)),
        ],
        out_shape=[
            jax.ShapeDtypeStruct((m, d), F32),
            jax.ShapeDtypeStruct((m // bm, SUBLANES, dff), F32),
            jax.ShapeDtypeStruct((m // bm, SUBLANES, dff), F32),
        ],
        scratch_shapes=[
            pltpu.VMEM((nf, SUBLANES, bf), F32),
            pltpu.VMEM((nf, SUBLANES, bf), F32),
        ],
        compiler_params=_cparams(("arbitrary", "arbitrary")),
        name="ffn_prompt",
    )(h2d, w_up, w_up, cw, cw, cb, cb, w_down)
    xo = _residual_norm(f, x2d, g3, bm=min(bm, 512))
    k = CONV_WIDTH - 1
    tg = tg.reshape(nseq, bps, SUBLANES, dff)[:, bps - 1, SUBLANES - k:]
    tv = tv.reshape(nseq, bps, SUBLANES, dff)[:, bps - 1, SUBLANES - k:]
    return xo, jnp.concatenate([tg, tv], axis=-1)


def _ffn_sample(h2d, w_up, cw, cb, w_down, x2d, g3, state, *, bf, ns):
    m, d = x2d.shape
    dff = w_down.shape[0]
    nf = dff // bf
    nb = m // ns
    k = CONV_WIDTH - 1
    h1 = jnp.zeros((nb, ns, 2 * dff), F32).at[:, 0].set(state[:, k - 1]).reshape(m, 2 * dff)
    h2 = jnp.zeros((nb, ns, 2 * dff), F32).at[:, :k].set(state).reshape(m, 2 * dff)
    full = lambda j: (0, 0)
    colg = lambda j: (0, j)
    colv = lambda j: (0, nf + j)
    xo, ug, uv = pl.pallas_call(
        functools.partial(_ffn_sample_kernel, ns=ns, nf=nf),
        grid=(nf,),
        in_specs=[
            pl.BlockSpec((m, d), full),
            pl.BlockSpec((d, bf), colg),
            pl.BlockSpec((d, bf), colv),
            pl.BlockSpec((CONV_WIDTH, bf), colg),
            pl.BlockSpec((CONV_WIDTH, bf), colv),
            pl.BlockSpec((1, bf), colg),
            pl.BlockSpec((1, bf), colv),
            pl.BlockSpec((bf, d), lambda j: (j, 0)),
            pl.BlockSpec((m, d), full),
            pl.BlockSpec((1, d), full),
            pl.BlockSpec((m, bf), colg),
            pl.BlockSpec((m, bf), colg),
            pl.BlockSpec((m, bf), colv),
            pl.BlockSpec((m, bf), colv),
        ],
        out_specs=[
            pl.BlockSpec((m, d), full),
            pl.BlockSpec((m, bf), colg),
            pl.BlockSpec((m, bf), colg),
        ],
        out_shape=[
            jax.ShapeDtypeStruct((m, d), F32),
            jax.ShapeDtypeStruct((m, dff), F32),
            jax.ShapeDtypeStruct((m, dff), F32),
        ],
        scratch_shapes=[pltpu.VMEM((m, d), F32)],
        compiler_params=_cparams(("arbitrary",)),
        name="ffn_sample",
    )(h2d, w_up, w_up, cw, cw, cb, cb, w_down, x2d, g3, h1, h2, h1, h2)
    u = jnp.concatenate([ug, uv], axis=-1).reshape(nb, ns, 2 * dff)
    return xo, u[:, ns - k:]


def _rope_tables(pos, reps):
    half = D_ROPE // 2
    inv = ROPE_BASE ** (-jnp.arange(half, dtype=F32) / half)
    ang = pos[:, None] * inv[None, :]
    c, s = jnp.cos(ang), jnp.sin(ang)
    z = jnp.zeros((pos.shape[0], LANES - D_ROPE), F32)
    cos_t = jnp.concatenate([c, c, z], axis=1)
    sin_t = jnp.concatenate([s, s, z], axis=1)
    return jnp.tile(cos_t, (reps, 1)), jnp.tile(sin_t, (reps, 1))


def _signed_partner(w):
    half = D_ROPE // 2
    return jnp.concatenate([-w[..., half:], w[..., :half]], axis=-1)


def _stage_mla_weights(w_dq, w_uq, w_dkv, w_uk, w_uv, w_o):
    w_kr = w_dkv[:, KV_LORA:]
    w1 = jnp.concatenate([w_dq, w_dkv[:, :KV_LORA], w_kr, _signed_partner(w_kr)], axis=1).astype(BF16)
    wq = w_uq.reshape(Q_LORA, N_HEADS, D_NOPE + D_ROPE)
    wq_rope = wq[..., D_NOPE:]
    wuq = jnp.concatenate([wq[..., :D_NOPE], wq_rope, _signed_partner(wq_rope)], axis=-1)
    wuq = wuq.reshape(Q_LORA, N_HEADS * HEAD_PAD).astype(BF16)
    wuk = w_uk.reshape(KV_LORA, N_HEADS * D_NOPE).astype(BF16)
    wuv = w_uv.reshape(KV_LORA, N_HEADS * D_V).astype(BF16)
    wukt = jnp.transpose(w_uk, (1, 2, 0)).astype(BF16)
    return w1, wuq, wuk, wuv, wukt, w_o.astype(BF16)


def _block_rows(m, target):
    bm = min(m, target)
    while m % bm:
        bm //= 2
    return bm


def kernel(x_prompt, x_sample, cache_ckv, cache_krope, state_pool, state_conv, norm_g, mla_w_dq, mla_q_norm, mla_w_uq, mla_w_dkv, mla_kv_norm, mla_w_uk, mla_w_uv, mla_w_o, pool_w, pool_scale, ffn_w_up, ffn_conv_w, ffn_conv_b, ffn_w_down):
    nbp, sp, d = x_prompt.shape
    nbs, ss, _ = x_sample.shape
    past = cache_ckv.shape[2]
    depth = norm_g.shape[0]
    mp, ms = nbp * sp, nbs * ss
    xp = x_prompt.reshape(mp, d)
    xs = x_sample.reshape(ms, d)

    bm_proj = _block_rows(sp, 256)
    bm_post = _block_rows(sp, 512)
    bm_ffn = _block_rows(sp, 1024)
    bm_pool = _block_rows(sp, 256)
    blk_attn = _block_rows(sp, 2048)
    bk_attn = _block_rows(blk_attn, 512)
    bf = 512
    bms = _block_rows(ms, 256)

    cos_p, sin_p = _rope_tables(jnp.arange(sp, dtype=jnp.int32).astype(F32), 1)
    cos_s, sin_s = _rope_tables((past + jnp.arange(ss, dtype=jnp.int32)).astype(F32), max(bms // ss, 1))

    outs = {k: [] for k in ("ckv_p", "kr_p", "pool_p", "conv_p", "ckv_s", "kr_s", "pool_s", "conv_s")}
    for i in range(depth):
        g = norm_g[i].reshape(4, 1, d)
        j = i // 2
        if i % 2 == 0:
            w1, wuq, wuk, wuv, wukt, wo = _stage_mla_weights(
                mla_w_dq[j], mla_w_uq[j], mla_w_dkv[j], mla_w_uk[j], mla_w_uv[j], mla_w_o[j])
            qn = mla_q_norm[j].reshape(1, Q_LORA)
            kvn = mla_kv_norm[j].reshape(1, KV_LORA)
            q, ckv, kr, k, v = _mla_project(xp, g[0], w1, qn, kvn, wuq, cos_p, sin_p, wuk, wuv, bm=bm_proj)
            o = _attention(q.reshape(nbp, sp, -1), k.reshape(nbp, sp, -1), v.reshape(nbp, sp, -1),
                           blk=blk_attn, bk=bk_attn)
            xp, hp = _attn_out(o.reshape(mp, -1), wo, xp, g[1], g[2], bm=bm_post)
            outs["ckv_p"].append(ckv.reshape(nbp, sp, KV_LORA))
            outs["kr_p"].append(kr.reshape(nbp, sp, D_ROPE))
            qs, ckv_s, kr_s = _mla_project(xs, g[0], w1, qn, kvn, wuq, cos_s, sin_s, bm=bms)
            qlat, qrope = _q_latent(qs, wukt)
            olat = _sample_attention(qlat, qrope, cache_ckv[j], cache_krope[j], ckv_s, kr_s, ns=ss)
            os_ = _o_from_latent(olat, wuv)
            xs, hs = _attn_out(os_, wo, xs, g[1], g[2], bm=bms)
            outs["ckv_s"].append(ckv_s.reshape(nbs, ss, KV_LORA))
            outs["kr_s"].append(kr_s.reshape(nbs, ss, D_ROPE))
        else:
            wp = pool_w[j].astype(BF16)
            sc = pool_scale[j].reshape(1, d)
            xp_new, hp, tail_p = _pool_layer(xp, xp, g[0], wp, sc, g[1], g[2], bm=bm_pool,
                                             rows_per_seq=sp, pos0=0, halo_normed=False)
            xp = xp_new
            outs["pool_p"].append(tail_p[:, POOL_HALO - POOL_HIST:])
            hist = jnp.pad(state_pool[j], ((0, 0), (POOL_HALO - POOL_HIST, 0), (0, 0))).reshape(-1, d)
            xs, hs, tail_s = _pool_layer(xs, hist, g[0], wp, sc, g[1], g[2], bm=ss,
                                         rows_per_seq=ss, pos0=past, halo_normed=True)
            outs["pool_s"].append(tail_s[:, POOL_HALO - POOL_HIST:])
        w_up = ffn_w_up[i].astype(BF16)
        w_down = ffn_w_down[i].astype(BF16)
        cw = ffn_conv_w[i]
        cb = ffn_conv_b[i].reshape(1, -1)
        xp, cv_p = _ffn_prompt(hp, w_up, cw, cb, w_down, xp, g[3], bm=bm_ffn, bf=bf, rows_per_seq=sp)
        xs, cv_s = _ffn_sample(hs, w_up, cw, cb, w_down, xs, g[3], state_conv[i], bf=bf, ns=ss)
        outs["conv_p"].append(cv_p)
        outs["conv_s"].append(cv_s)
    st = lambda k: jnp.stack(outs[k])
    return (xp.reshape(nbp, sp, d), xs.reshape(nbs, ss, d),
            st("ckv_p"), st("kr_p"), st("pool_p"), st("conv_p"),
            st("ckv_s"), st("kr_s"), st("pool_s"), st("conv_s"))
```

```python
import functools

import numpy as np
import jax
import jax.numpy as jnp
from jax import lax
from jax.experimental import pallas as pl
from jax.experimental.pallas import tpu as pltpu

F32 = jnp.float32
BF16 = jnp.bfloat16

CHUNK = 64
N_HEADS = 16
Q_LORA = 512
KV_LORA = 512
D_NOPE = 128
D_ROPE = 64
D_V = 128
ROPE_BASE = 10000.0
ATTN_SCALE = (D_NOPE + D_ROPE) ** -0.5
Q_SCALE = ATTN_SCALE * float(np.log2(np.e))
POOL_WINDOWS = (2, 4, 8, 16)
POOL_HIST = max(POOL_WINDOWS) - 1
CONV_WIDTH = 3
EPS = 1e-6

LANES = 128
SUBLANES = 8
HEAD_PAD = 2 * LANES
POOL_HALO = 16
VMEM_LIMIT = 56 * 1024 * 1024
NEG_BIG = -1e30


def _cparams(sem):
    return pltpu.CompilerParams(dimension_semantics=sem, vmem_limit_bytes=VMEM_LIMIT)


def _rms(xf, g):
    ms = jnp.mean(xf * xf, axis=-1, keepdims=True)
    return xf * lax.rsqrt(ms + EPS) * g


def _dot(a, b):
    return jnp.dot(a, b, preferred_element_type=F32)


def _dot_nt(a, b):
    return lax.dot_general(a, b, (((1,), (1,)), ((), ())), preferred_element_type=F32)


def _rope_cols(r, cos_t, sin_t):
    return r * cos_t + pltpu.roll(r, D_ROPE, axis=1) * sin_t


def _proj_kernel(x_ref, g_ref, w1_ref, qn_ref, kvn_ref, wuq_ref, cos_ref, sin_ref, *rest, with_kv):
    if with_kv:
        wuk_ref, wuv_ref, q_ref, ckv_ref, kr_ref, k_ref, v_ref = rest
    else:
        q_ref, ckv_ref, kr_ref = rest
    h = _rms(x_ref[...], g_ref[...]).astype(BF16)
    y = _dot(h, w1_ref[...])
    cq = _rms(y[:, :Q_LORA], qn_ref[...]).astype(BF16)
    ckv = _rms(y[:, Q_LORA:Q_LORA + KV_LORA], kvn_ref[...])
    ckv_ref[...] = ckv
    cos_t = cos_ref[...]
    sin_t = sin_ref[...]
    kr = _rope_cols(y[:, Q_LORA + KV_LORA:], cos_t, sin_t)
    kr_ref[...] = kr[:, :D_ROPE]
    q = _dot(cq, wuq_ref[...]) * Q_SCALE
    for hd in range(N_HEADS):
        b0 = hd * HEAD_PAD
        q_ref[:, b0:b0 + LANES] = q[:, b0:b0 + LANES].astype(BF16)
        q_ref[:, b0 + LANES:b0 + HEAD_PAD] = _rope_cols(
            q[:, b0 + LANES:b0 + HEAD_PAD], cos_t, sin_t).astype(BF16)
    if with_kv:
        ckv_b = ckv.astype(BF16)
        kr_b = kr.astype(BF16)
        kn = _dot(ckv_b, wuk_ref[...])
        for hd in range(N_HEADS):
            b0 = hd * HEAD_PAD
            k_ref[:, b0:b0 + LANES] = kn[:, hd * D_NOPE:(hd + 1) * D_NOPE].astype(BF16)
            k_ref[:, b0 + LANES:b0 + HEAD_PAD] = kr_b
        v_ref[...] = _dot(ckv_b, wuv_ref[...]).astype(BF16)


def _mla_project(x2d, g, w1, qn, kvn, wuq, cos_t, sin_t, wuk=None, wuv=None, *, bm):
    m, d = x2d.shape
    with_kv = wuk is not None
    nt = cos_t.shape[0] // bm
    row = lambda i: (i, 0)
    const = lambda i: (0, 0)
    in_specs = [
        pl.BlockSpec((bm, d), row),
        pl.BlockSpec((1, d), const),
        pl.BlockSpec(w1.shape, const),
        pl.BlockSpec((1, Q_LORA), const),
        pl.BlockSpec((1, KV_LORA), const),
        pl.BlockSpec(wuq.shape, const),
        pl.BlockSpec((bm, LANES), lambda i: (i % nt, 0)),
        pl.BlockSpec((bm, LANES), lambda i: (i % nt, 0)),
    ]
    args = [x2d, g, w1, qn, kvn, wuq, cos_t, sin_t]
    out_shape = [
        jax.ShapeDtypeStruct((m, N_HEADS * HEAD_PAD), BF16),
        jax.ShapeDtypeStruct((m, KV_LORA), F32),
        jax.ShapeDtypeStruct((m, D_ROPE), F32),
    ]
    out_specs = [
        pl.BlockSpec((bm, N_HEADS * HEAD_PAD), row),
        pl.BlockSpec((bm, KV_LORA), row),
        pl.BlockSpec((bm, D_ROPE), row),
    ]
    if with_kv:
        in_specs += [pl.BlockSpec(wuk.shape, const), pl.BlockSpec(wuv.shape, const)]
        args += [wuk, wuv]
        out_shape += [
            jax.ShapeDtypeStruct((m, N_HEADS * HEAD_PAD), BF16),
            jax.ShapeDtypeStruct((m, N_HEADS * D_V), BF16),
        ]
        out_specs += [
            pl.BlockSpec((bm, N_HEADS * HEAD_PAD), row),
            pl.BlockSpec((bm, N_HEADS * D_V), row),
        ]
    return pl.pallas_call(
        functools.partial(_proj_kernel, with_kv=with_kv),
        grid=(m // bm,),
        in_specs=in_specs,
        out_specs=out_specs,
        out_shape=out_shape,
        compiler_params=_cparams(("arbitrary",)),
        name="mla_project",
    )(*args)


def _attn_kernel(q_ref, k_ref, v_ref, o_ref, m_scr, l_scr, acc_scr, *, blk, bk, sub):
    qi = pl.program_id(2)
    m_scr[...] = jnp.full(m_scr.shape, NEG_BIG, F32)
    l_scr[...] = jnp.zeros(l_scr.shape, F32)
    acc_scr[...] = jnp.zeros(acc_scr.shape, F32)

    def step(start, diag):
        scores = []
        for c in range(blk // sub):
            nk = bk if diag is None else min(bk, (c + 1) * sub - diag)
            if nk <= 0:
                continue
            k = k_ref[0, pl.ds(start, nk), :]
            st = _dot_nt(k, q_ref[0, c * sub:(c + 1) * sub, :])
            if diag is not None and diag + nk > c * sub + CHUNK:
                kc = (lax.broadcasted_iota(jnp.int32, st.shape, 0) + diag) // CHUNK
                qc = (lax.broadcasted_iota(jnp.int32, st.shape, 1) + c * sub) // CHUNK
                st = jnp.where(kc <= qc, st, NEG_BIG)
            scores.append((c, nk, st))
        for c, nk, st in scores:
            cs = slice(c * sub, (c + 1) * sub)
            v = v_ref[0, pl.ds(start, nk), :]
            m_prev = m_scr[:, cs]
            m_new = jnp.maximum(m_prev, jnp.max(st, axis=0, keepdims=True))
            alpha = jnp.exp2(m_prev - m_new)
            p = jnp.exp2(st - m_new)
            l_scr[:, cs] = alpha * l_scr[:, cs] + jnp.sum(p, axis=0, keepdims=True)
            pv = lax.dot_general(v, p.astype(BF16), (((0,), (0,)), ((), ())),
                                 preferred_element_type=F32)
            acc_scr[:, cs] = alpha * acc_scr[:, cs] + pv
            m_scr[:, cs] = m_new

    def body(ki, carry):
        step(pl.multiple_of(ki * bk, bk), None)
        return carry

    per_q = blk // bk
    lax.fori_loop(0, qi * per_q, body, 0)
    for d in range(per_q):
        step(pl.multiple_of(qi * blk + d * bk, bk), d * bk)
    o_ref[0] = jnp.transpose(acc_scr[...] / l_scr[...]).astype(o_ref.dtype)


def _attention(q, k, v, *, blk, bk):
    b, s, _ = q.shape
    return pl.pallas_call(
        functools.partial(_attn_kernel, blk=blk, bk=bk, sub=min(bk, HEAD_PAD)),
        grid=(b, N_HEADS, s // blk),
        in_specs=[
            pl.BlockSpec((1, blk, HEAD_PAD), lambda bi, h, qi: (bi, qi, h)),
            pl.BlockSpec((1, s, HEAD_PAD), lambda bi, h, qi: (bi, 0, h)),
            pl.BlockSpec((1, s, D_V), lambda bi, h, qi: (bi, 0, h)),
        ],
        out_specs=pl.BlockSpec((1, blk, D_V), lambda bi, h, qi: (bi, qi, h)),
        out_shape=jax.ShapeDtypeStruct((b, s, N_HEADS * D_V), BF16),
        scratch_shapes=[
            pltpu.VMEM((1, blk), F32),
            pltpu.VMEM((1, blk), F32),
            pltpu.VMEM((D_V, blk), F32),
        ],
        compiler_params=_cparams(("arbitrary", "arbitrary", "arbitrary")),
        name="prompt_attention",
    )(q, k, v)


def _qlat_kernel(q_ref, wukt_ref, qlat_ref, qrope_ref):
    q = q_ref[...]
    qlat_ref[0] = _dot(q[:, :D_NOPE], wukt_ref[0]).astype(BF16)
    qrope_ref[0] = q[:, LANES:]


def _q_latent(q, wukt):
    m = q.shape[0]
    return pl.pallas_call(
        _qlat_kernel,
        grid=(N_HEADS,),
        in_specs=[
            pl.BlockSpec((m, HEAD_PAD), lambda h: (0, h)),
            pl.BlockSpec((1, D_NOPE, KV_LORA), lambda h: (h, 0, 0)),
        ],
        out_specs=[
            pl.BlockSpec((1, m, KV_LORA), lambda h: (h, 0, 0)),
            pl.BlockSpec((1, m, LANES), lambda h: (h, 0, 0)),
        ],
        out_shape=[
            jax.ShapeDtypeStruct((N_HEADS, m, KV_LORA), BF16),
            jax.ShapeDtypeStruct((N_HEADS, m, LANES), BF16),
        ],
        compiler_params=_cparams(("arbitrary",)),
        name="sample_q_latent",
    )(q, wukt)


def _sattn_kernel(qlat_ref, qrope_ref, cckv_ref, ckr_ref, nckv_ref, nkr_ref, o_ref, *, past, ns):
    rows = N_HEADS * ns
    ql = qlat_ref[...].reshape(rows, KV_LORA)
    qr = qrope_ref[...].reshape(rows, LANES)[:, :D_ROPE]
    cc = cckv_ref[0].astype(BF16)
    ck = ckr_ref[0].astype(BF16)
    nc = nckv_ref[...].astype(BF16)
    nk = nkr_ref[...].astype(BF16)
    s1 = _dot_nt(ql, cc) + _dot_nt(qr, ck)
    s2 = _dot_nt(ql, nc) + _dot_nt(qr, nk)
    qt = lax.broadcasted_iota(jnp.int32, s2.shape, 0) % ns
    kt = lax.broadcasted_iota(jnp.int32, s2.shape, 1)
    s2 = jnp.where((past + kt) // CHUNK <= (past + qt) // CHUNK, s2, NEG_BIG)
    m = jnp.maximum(jnp.max(s1, axis=1, keepdims=True), jnp.max(s2, axis=1, keepdims=True))
    p1 = jnp.exp2(s1 - m)
    p2 = jnp.exp2(s2 - m)
    l = jnp.sum(p1, axis=1, keepdims=True) + jnp.sum(p2, axis=1, keepdims=True)
    o = (_dot(p1.astype(BF16), cc) + _dot(p2.astype(BF16), nc)) / l
    o_ref[...] = o.astype(BF16).reshape(N_HEADS, ns, KV_LORA)


def _sample_attention(qlat, qrope, cache_ckv, cache_kr, new_ckv, new_kr, *, ns):
    nb, past, _ = cache_ckv.shape
    m = nb * ns
    return pl.pallas_call(
        functools.partial(_sattn_kernel, past=past, ns=ns),
        grid=(nb,),
        in_specs=[
            pl.BlockSpec((N_HEADS, ns, KV_LORA), lambda b: (0, b, 0)),
            pl.BlockSpec((N_HEADS, ns, LANES), lambda b: (0, b, 0)),
            pl.BlockSpec((1, past, KV_LORA), lambda b: (b, 0, 0)),
            pl.BlockSpec((1, past, D_ROPE), lambda b: (b, 0, 0)),
            pl.BlockSpec((ns, KV_LORA), lambda b: (b, 0)),
            pl.BlockSpec((ns, D_ROPE), lambda b: (b, 0)),
        ],
        out_specs=pl.BlockSpec((N_HEADS, ns, KV_LORA), lambda b: (0, b, 0)),
        out_shape=jax.ShapeDtypeStruct((N_HEADS, m, KV_LORA), BF16),
        compiler_params=_cparams(("arbitrary",)),
        name="sample_attention",
    )(qlat, qrope, cache_ckv, cache_kr, new_ckv, new_kr)


def _ouv_kernel(olat_ref, wuv_ref, o_ref):
    o_ref[...] = _dot(olat_ref[0], wuv_ref[...]).astype(BF16)


def _o_from_latent(olat, wuv):
    m = olat.shape[1]
    return pl.pallas_call(
        _ouv_kernel,
        grid=(N_HEADS,),
        in_specs=[
            pl.BlockSpec((1, m, KV_LORA), lambda h: (h, 0, 0)),
            pl.BlockSpec((KV_LORA, D_V), lambda h: (0, h)),
        ],
        out_specs=pl.BlockSpec((m, D_V), lambda h: (0, h)),
        out_shape=jax.ShapeDtypeStruct((m, N_HEADS * D_V), BF16),
        compiler_params=_cparams(("arbitrary",)),
        name="sample_o_from_latent",
    )(olat, wuv)


def _post_kernel(o_ref, w_ref, x_ref, g1_ref, g2_ref, xo_ref, h2_ref):
    xn = x_ref[...] + _rms(_dot(o_ref[...], w_ref[...]), g1_ref[...])
    xo_ref[...] = xn
    h2_ref[...] = _rms(xn, g2_ref[...]).astype(BF16)


def _attn_out(o2d, w_o, x2d, g1, g2, *, bm):
    m, d = x2d.shape
    row = lambda i: (i, 0)
    const = lambda i: (0, 0)
    return pl.pallas_call(
        _post_kernel,
        grid=(m // bm,),
        in_specs=[
            pl.BlockSpec((bm, o2d.shape[1]), row),
            pl.BlockSpec(w_o.shape, const),
            pl.BlockSpec((bm, d), row),
            pl.BlockSpec((1, d), const),
            pl.BlockSpec((1, d), const),
        ],
        out_specs=[pl.BlockSpec((bm, d), row), pl.BlockSpec((bm, d), row)],
        out_shape=[jax.ShapeDtypeStruct((m, d), F32), jax.ShapeDtypeStruct((m, d), BF16)],
        compiler_params=_cparams(("arbitrary",)),
        name="attn_out_post",
    )(o2d, w_o, x2d, g1, g2)


def _pool_kernel(x_ref, halo_ref, g0_ref, w_ref, sc_ref, g1_ref, g2_ref,
                 xo_ref, h2_ref, tail_ref, *, bm, blocks_per_seq, pos0, halo_normed):
    i = pl.program_id(0)
    x = x_ref[...]
    g0 = g0_ref[...]
    h = _rms(x, g0)
    halo = halo_ref[...]
    if not halo_normed:
        halo = _rms(halo, g0)
        halo = jnp.where(i % blocks_per_seq == 0, 0.0, halo)
    cat = jnp.concatenate([halo, h], axis=0)
    t = (i % blocks_per_seq) * bm + lax.broadcasted_iota(jnp.int32, (bm, 1), 0)
    posf = (t + pos0).astype(F32)
    gd = x.shape[1] // len(POOL_WINDOWS)
    ys = []
    for g, w in enumerate(POOL_WINDOWS):
        c = cat[:, g * gd:(g + 1) * gd]
        acc = c
        span = 1
        while span < w:
            acc = acc + pltpu.roll(acc, span, axis=0)
            span *= 2
        cnt = jnp.minimum(jnp.float32(w), posf + 1.0)
        mean = acc[POOL_HALO:] / cnt
        dlt = (mean - h[:, g * gd:(g + 1) * gd]).astype(BF16)
        ys.append(_dot(dlt, w_ref[g]))
    y = jnp.concatenate(ys, axis=1) * sc_ref[...]
    xn = x + _rms(y, g1_ref[...])
    xo_ref[...] = xn
    h2_ref[...] = _rms(xn, g2_ref[...]).astype(BF16)
    tail_ref[0] = h[bm - POOL_HALO:]


def _pool_layer(x2d, halo_src, g0, w_pool, scale, g1, g2, *, bm, rows_per_seq, pos0, halo_normed):
    m, d = x2d.shape
    bps = rows_per_seq // bm
    nseq = m // rows_per_seq
    row = lambda i: (i, 0)
    const = lambda i: (0, 0)
    if halo_normed:
        halo_map = lambda i: (i, 0)
    else:
        hb = bm // POOL_HALO
        halo_map = lambda i: (jnp.maximum(i * hb - 1, 0), 0)
    return pl.pallas_call(
        functools.partial(_pool_kernel, bm=bm, blocks_per_seq=bps, pos0=pos0, halo_normed=halo_normed),
        grid=(m // bm,),
        in_specs=[
            pl.BlockSpec((bm, d), row),
            pl.BlockSpec((POOL_HALO, d), halo_map),
            pl.BlockSpec((1, d), const),
            pl.BlockSpec(w_pool.shape, lambda i: (0, 0, 0)),
            pl.BlockSpec((1, d), const),
            pl.BlockSpec((1, d), const),
            pl.BlockSpec((1, d), const),
        ],
        out_specs=[
            pl.BlockSpec((bm, d), row),
            pl.BlockSpec((bm, d), row),
            pl.BlockSpec((1, POOL_HALO, d), lambda i: (i // bps, 0, 0)),
        ],
        out_shape=[
            jax.ShapeDtypeStruct((m, d), F32),
            jax.ShapeDtypeStruct((m, d), BF16),
            jax.ShapeDtypeStruct((nseq, POOL_HALO, d), F32),
        ],
        compiler_params=_cparams(("arbitrary",)),
        name="pool_layer",
    )(x2d, halo_src, g0, w_pool, scale, g1, g2)


def _conv3(u, prev1, prev2, cw_ref, cb_ref):
    return cb_ref[...] + cw_ref[0:1, :] * prev2 + cw_ref[1:2, :] * prev1 + cw_ref[2:3, :] * u


def _ffn_tail(j, nf, act, wd_ref, x_ref, g3_ref, xo_ref, acc_scr):
    part = _dot(act.astype(BF16), wd_ref[...])

    @pl.when(j == 0)
    def _():
        acc_scr[...] = part

    @pl.when(j > 0)
    def _():
        acc_scr[...] += part

    @pl.when(j == nf - 1)
    def _():
        xo_ref[...] = x_ref[...] + _rms(acc_scr[...], g3_ref[...])


def _ffn_prompt_kernel(h_ref, wg_ref, wv_ref, cwg_ref, cwv_ref, cbg_ref, cbv_ref, wd_ref,
                       f_ref, tg_ref, tv_ref, cg_scr, cv_scr, act_scr, *, bm, blocks_per_seq, nf):
    t = pl.program_id(0)
    cur = jnp.minimum(t, pl.num_programs(0) - 2)
    i = cur // nf
    j = cur % nf
    prev = jnp.maximum(t - 1, 0)

    @pl.when(t == 0)
    def _():
        act_scr[...] = jnp.zeros(act_scr.shape, BF16)

    @pl.when(prev % nf == 0)
    def _():
        f_ref[...] = jnp.zeros(f_ref.shape, F32)

    f_ref[...] += _dot(act_scr[(t + 1) % 2], wd_ref[...])

    h = h_ref[...]
    first = i % blocks_per_seq == 0

    def branch(w_ref, cw_ref, cb_ref, carry_scr, tail_ref):
        u = _dot(h, w_ref[...])
        prev8 = jnp.where(first, 0.0, carry_scr[j])
        c = _conv3(u, pltpu.roll(u, 1, axis=0), pltpu.roll(u, 2, axis=0), cw_ref, cb_ref)
        head = jnp.concatenate([prev8, u[:SUBLANES]], axis=0)
        c_head = _conv3(head, pltpu.roll(head, 1, axis=0), pltpu.roll(head, 2, axis=0),
                        cw_ref, cb_ref)[SUBLANES:]
        c = jnp.concatenate([c_head, c[SUBLANES:]], axis=0)
        last8 = u[bm - SUBLANES:]
        carry_scr[j] = last8
        tail_ref[0] = last8
        return c

    gate = branch(wg_ref, cwg_ref, cbg_ref, cg_scr, tg_ref)
    val = branch(wv_ref, cwv_ref, cbv_ref, cv_scr, tv_ref)
    act_scr[t % 2] = (gate * jax.nn.sigmoid(gate) * val).astype(BF16)


def _resid_kernel(f_ref, x_ref, g_ref, xo_ref):
    xo_ref[...] = x_ref[...] + _rms(f_ref[...], g_ref[...])


def _residual_norm(f2d, x2d, g, *, bm):
    m, d = x2d.shape
    row = lambda i: (i, 0)
    return pl.pallas_call(
        _resid_kernel,
        grid=(m // bm,),
        in_specs=[pl.BlockSpec((bm, d), row), pl.BlockSpec((bm, d), row), pl.BlockSpec((1, d), lambda i: (0, 0))],
        out_specs=pl.BlockSpec((bm, d), row),
        out_shape=jax.ShapeDtypeStruct((m, d), F32),
        compiler_params=_cparams(("arbitrary",)),
        name="residual_norm",
    )(f2d, x2d, g)


def _ffn_sample_kernel(h_ref, wg_ref, wv_ref, cwg_ref, cwv_ref, cbg_ref, cbv_ref, wd_ref, x_ref, g3_ref,
                       h1g_ref, h2g_ref, h1v_ref, h2v_ref, xo_ref, ug_ref, uv_ref, acc_scr, *, ns, nf):
    j = pl.program_id(0)
    h = h_ref[...]
    t = lax.broadcasted_iota(jnp.int32, (h.shape[0], 1), 0) % ns

    def branch(w_ref, cw_ref, cb_ref, h1_ref, h2_ref, u_ref):
        u = _dot(h, w_ref[...])
        u_ref[...] = u
        prev1 = jnp.where(t < 1, h1_ref[...], pltpu.roll(u, 1, axis=0))
        prev2 = jnp.where(t < 2, h2_ref[...], pltpu.roll(u, 2, axis=0))
        return _conv3(u, prev1, prev2, cw_ref, cb_ref)

    gate = branch(wg_ref, cwg_ref, cbg_ref, h1g_ref, h2g_ref, ug_ref)
    val = branch(wv_ref, cwv_ref, cbv_ref, h1v_ref, h2v_ref, uv_ref)
    act = gate * jax.nn.sigmoid(gate) * val
    _ffn_tail(j, nf, act, wd_ref, x_ref, g3_ref, xo_ref, acc_scr)


def _ffn_prompt(h2d, w_up, cw, cb, w_down, x2d, g3, *, layer, bm, bf, rows_per_seq):
    m, d = x2d.shape
    dff = w_down.shape[1]
    nf = dff // bf
    bps = rows_per_seq // bm
    nseq = m // rows_per_seq
    n = (m // bm) * nf
    cur = lambda t: jnp.minimum(t, n - 1)
    prv = lambda t: jnp.maximum(t - 1, 0)
    f, tg, tv = pl.pallas_call(
        functools.partial(_ffn_prompt_kernel, bm=bm, blocks_per_seq=bps, nf=nf),
        grid=(n + 1,),
        in_specs=[
            pl.BlockSpec((bm, d), lambda t: (cur(t) // nf, 0)),
            pl.BlockSpec((None, d, bf), lambda t: (layer, 0, cur(t) % nf)),
            pl.BlockSpec((None, d, bf), lambda t: (layer, 0, nf + cur(t) % nf)),
            pl.BlockSpec((None, CONV_WIDTH, bf), lambda t: (layer, 0, cur(t) % nf)),
            pl.BlockSpec((None, CONV_WIDTH, bf), lambda t: (layer, 0, nf + cur(t) % nf)),
            pl.BlockSpec((None, 1, bf), lambda t: (layer, 0, cur(t) % nf)),
            pl.BlockSpec((None, 1, bf), lambda t: (layer, 0, nf + cur(t) % nf)),
            pl.BlockSpec((None, bf, d), lambda t: (layer, prv(t) % nf, 0)),
        ],
        out_specs=[
            pl.BlockSpec((bm, d), lambda t: (prv(t) // nf, 0)),
            pl.BlockSpec((1, SUBLANES, bf), lambda t: (cur(t) // nf, 0, cur(t) % nf)),
            pl.BlockSpec((1, SUBLANES, bf), lambda t: (cur(t) // nf, 0, cur(t) % nf)),
        ],
        out_shape=[
            jax.ShapeDtypeStruct((m, d), F32),
            jax.ShapeDtypeStruct((m // bm, SUBLANES, dff), F32),
            jax.ShapeDtypeStruct((m // bm, SUBLANES, dff), F32),
        ],
        scratch_shapes=[
            pltpu.VMEM((nf, SUBLANES, bf), F32),
            pltpu.VMEM((nf, SUBLANES, bf), F32),
            pltpu.VMEM((2, bm, bf), BF16),
        ],
        compiler_params=_cparams(("arbitrary",)),
        name="ffn_prompt",
    )(h2d, w_up, w_up, cw, cw, cb, cb, w_down)
    xo = _residual_norm(f, x2d, g3, bm=min(bm, 512))
    k = CONV_WIDTH - 1
    tg = tg.reshape(nseq, bps, SUBLANES, dff)[:, bps - 1, SUBLANES - k:]
    tv = tv.reshape(nseq, bps, SUBLANES, dff)[:, bps - 1, SUBLANES - k:]
    return xo, jnp.concatenate([tg, tv], axis=-1)


def _ffn_sample(h2d, w_up, cw, cb, w_down, x2d, g3, state, *, layer, bf, ns):
    m, d = x2d.shape
    dff = w_down.shape[1]
    nf = dff // bf
    nb = m // ns
    k = CONV_WIDTH - 1
    h1 = jnp.zeros((nb, ns, 2 * dff), F32).at[:, 0].set(state[:, k - 1]).reshape(m, 2 * dff)
    h2 = jnp.zeros((nb, ns, 2 * dff), F32).at[:, :k].set(state).reshape(m, 2 * dff)
    full = lambda j: (0, 0)
    colg = lambda j: (0, j)
    colv = lambda j: (0, nf + j)
    xo, ug, uv = pl.pallas_call(
        functools.partial(_ffn_sample_kernel, ns=ns, nf=nf),
        grid=(nf,),
        in_specs=[
            pl.BlockSpec((m, d), full),
            pl.BlockSpec((None, d, bf), lambda j: (layer, 0, j)),
            pl.BlockSpec((None, d, bf), lambda j: (layer, 0, nf + j)),
            pl.BlockSpec((None, CONV_WIDTH, bf), lambda j: (layer, 0, j)),
            pl.BlockSpec((None, CONV_WIDTH, bf), lambda j: (layer, 0, nf + j)),
            pl.BlockSpec((None, 1, bf), lambda j: (layer, 0, j)),
            pl.BlockSpec((None, 1, bf), lambda j: (layer, 0, nf + j)),
            pl.BlockSpec((None, bf, d), lambda j: (layer, j, 0)),
            pl.BlockSpec((m, d), full),
            pl.BlockSpec((1, d), full),
            pl.BlockSpec((m, bf), colg),
            pl.BlockSpec((m, bf), colg),
            pl.BlockSpec((m, bf), colv),
            pl.BlockSpec((m, bf), colv),
        ],
        out_specs=[
            pl.BlockSpec((m, d), full),
            pl.BlockSpec((m, bf), colg),
            pl.BlockSpec((m, bf), colg),
        ],
        out_shape=[
            jax.ShapeDtypeStruct((m, d), F32),
            jax.ShapeDtypeStruct((m, dff), F32),
            jax.ShapeDtypeStruct((m, dff), F32),
        ],
        scratch_shapes=[pltpu.VMEM((m, d), F32)],
        compiler_params=_cparams(("arbitrary",)),
        name="ffn_sample",
    )(h2d, w_up, w_up, cw, cw, cb, cb, w_down, x2d, g3, h1, h2, h1, h2)
    u = jnp.concatenate([ug, uv], axis=-1).reshape(nb, ns, 2 * dff)
    return xo, u[:, ns - k:]


def _rope_tables(pos, reps):
    half = D_ROPE // 2
    inv = ROPE_BASE ** (-jnp.arange(half, dtype=F32) / half)
    ang = pos[:, None] * inv[None, :]
    c, s = jnp.cos(ang), jnp.sin(ang)
    z = jnp.zeros((pos.shape[0], LANES - D_ROPE), F32)
    cos_t = jnp.concatenate([c, c, z], axis=1)
    sin_t = jnp.concatenate([s, s, z], axis=1)
    return jnp.tile(cos_t, (reps, 1)), jnp.tile(sin_t, (reps, 1))


def _signed_partner(w):
    half = D_ROPE // 2
    return jnp.concatenate([-w[..., half:], w[..., :half]], axis=-1)


def _stage_mla_weights(w_dq, w_uq, w_dkv, w_uk, w_uv, w_o):
    w_kr = w_dkv[:, KV_LORA:]
    w1 = jnp.concatenate([w_dq, w_dkv[:, :KV_LORA], w_kr, _signed_partner(w_kr)], axis=1).astype(BF16)
    wq = w_uq.reshape(Q_LORA, N_HEADS, D_NOPE + D_ROPE)
    wq_rope = wq[..., D_NOPE:]
    wuq = jnp.concatenate([wq[..., :D_NOPE], wq_rope, _signed_partner(wq_rope)], axis=-1)
    wuq = wuq.reshape(Q_LORA, N_HEADS * HEAD_PAD).astype(BF16)
    wuk = w_uk.reshape(KV_LORA, N_HEADS * D_NOPE).astype(BF16)
    wuv = w_uv.reshape(KV_LORA, N_HEADS * D_V).astype(BF16)
    wukt = jnp.transpose(w_uk, (1, 2, 0)).astype(BF16)
    return w1, wuq, wuk, wuv, wukt, w_o.astype(BF16)


def _block_rows(m, target):
    bm = min(m, target)
    while m % bm:
        bm //= 2
    return bm


def kernel(x_prompt, x_sample, cache_ckv, cache_krope, state_pool, state_conv, norm_g, mla_w_dq, mla_q_norm, mla_w_uq, mla_w_dkv, mla_kv_norm, mla_w_uk, mla_w_uv, mla_w_o, pool_w, pool_scale, ffn_w_up, ffn_conv_w, ffn_conv_b, ffn_w_down):
    nbp, sp, d = x_prompt.shape
    nbs, ss, _ = x_sample.shape
    past = cache_ckv.shape[2]
    depth = norm_g.shape[0]
    mp, ms = nbp * sp, nbs * ss
    xp = x_prompt.reshape(mp, d)
    xs = x_sample.reshape(ms, d)

    bm_proj = _block_rows(sp, 256)
    bm_post = _block_rows(sp, 512)
    bm_ffn = _block_rows(sp, 1024)
    bm_pool = _block_rows(sp, 256)
    blk_attn = _block_rows(sp, 4096)
    bk_attn = _block_rows(blk_attn, 512)
    bf = 512
    bms = _block_rows(ms, 256)

    cos_p, sin_p = _rope_tables(jnp.arange(sp, dtype=jnp.int32).astype(F32), 1)
    cos_s, sin_s = _rope_tables((past + jnp.arange(ss, dtype=jnp.int32)).astype(F32), max(bms // ss, 1))

    w_up = ffn_w_up.astype(BF16)
    w_down = ffn_w_down.astype(BF16)
    cb = ffn_conv_b.reshape(depth, 1, -1)

    outs = {k: [] for k in ("ckv_p", "kr_p", "pool_p", "conv_p", "ckv_s", "kr_s", "pool_s", "conv_s")}
    for i in range(depth):
        g = norm_g[i].reshape(4, 1, d)
        j = i // 2
        if i % 2 == 0:
            w1, wuq, wuk, wuv, wukt, wo = _stage_mla_weights(
                mla_w_dq[j], mla_w_uq[j], mla_w_dkv[j], mla_w_uk[j], mla_w_uv[j], mla_w_o[j])
            qn = mla_q_norm[j].reshape(1, Q_LORA)
            kvn = mla_kv_norm[j].reshape(1, KV_LORA)
            q, ckv, kr, k, v = _mla_project(xp, g[0], w1, qn, kvn, wuq, cos_p, sin_p, wuk, wuv, bm=bm_proj)
            o = _attention(q.reshape(nbp, sp, -1), k.reshape(nbp, sp, -1), v.reshape(nbp, sp, -1),
                           blk=blk_attn, bk=bk_attn)
            xp, hp = _attn_out(o.reshape(mp, -1), wo, xp, g[1], g[2], bm=bm_post)
            outs["ckv_p"].append(ckv.reshape(nbp, sp, KV_LORA))
            outs["kr_p"].append(kr.reshape(nbp, sp, D_ROPE))
            qs, ckv_s, kr_s = _mla_project(xs, g[0], w1, qn, kvn, wuq, cos_s, sin_s, bm=bms)
            qlat, qrope = _q_latent(qs, wukt)
            olat = _sample_attention(qlat, qrope, cache_ckv[j], cache_krope[j], ckv_s, kr_s, ns=ss)
            os_ = _o_from_latent(olat, wuv)
            xs, hs = _attn_out(os_, wo, xs, g[1], g[2], bm=bms)
            outs["ckv_s"].append(ckv_s.reshape(nbs, ss, KV_LORA))
            outs["kr_s"].append(kr_s.reshape(nbs, ss, D_ROPE))
        else:
            wp = pool_w[j].astype(BF16)
            sc = pool_scale[j].reshape(1, d)
            xp_new, hp, tail_p = _pool_layer(xp, xp, g[0], wp, sc, g[1], g[2], bm=bm_pool,
                                             rows_per_seq=sp, pos0=0, halo_normed=False)
            xp = xp_new
            outs["pool_p"].append(tail_p[:, POOL_HALO - POOL_HIST:])
            hist = jnp.pad(state_pool[j], ((0, 0), (POOL_HALO - POOL_HIST, 0), (0, 0))).reshape(-1, d)
            xs, hs, tail_s = _pool_layer(xs, hist, g[0], wp, sc, g[1], g[2], bm=ss,
                                         rows_per_seq=ss, pos0=past, halo_normed=True)
            outs["pool_s"].append(tail_s[:, POOL_HALO - POOL_HIST:])
        xp, cv_p = _ffn_prompt(hp, w_up, ffn_conv_w, cb, w_down, xp, g[3], layer=i, bm=bm_ffn, bf=bf,
                               rows_per_seq=sp)
        xs, cv_s = _ffn_sample(hs, w_up, ffn_conv_w, cb, w_down, xs, g[3], state_conv[i], layer=i, bf=bf,
                               ns=ss)
        outs["conv_p"].append(cv_p)
        outs["conv_s"].append(cv_s)
    st = lambda k: jnp.stack(outs[k])
    return (xp.reshape(nbp, sp, d), xs.reshape(nbs, ss, d),
            st("ckv_p"), st("kr_p"), st("pool_p"), st("conv_p"),
            st("ckv_s"), st("kr_s"), st("pool_s"), st("conv_s"))
```

```python
import functools

import numpy as np
import jax
import jax.numpy as jnp
from jax import lax
from jax.experimental import pallas as pl
from jax.experimental.pallas import tpu as pltpu

F32 = jnp.float32
BF16 = jnp.bfloat16

CHUNK = 64
N_HEADS = 16
Q_LORA = 512
KV_LORA = 512
D_NOPE = 128
D_ROPE = 64
D_V = 128
ROPE_BASE = 10000.0
ATTN_SCALE = (D_NOPE + D_ROPE) ** -0.5
Q_SCALE = ATTN_SCALE * float(np.log2(np.e))
POOL_WINDOWS = (2, 4, 8, 16)
POOL_HIST = max(POOL_WINDOWS) - 1
CONV_WIDTH = 3
EPS = 1e-6

LANES = 128
SUBLANES = 8
HEAD_PAD = 2 * LANES
POOL_HALO = 16
VMEM_LIMIT = 56 * 1024 * 1024
NEG_BIG = -1e30


def _cparams(sem):
    return pltpu.CompilerParams(dimension_semantics=sem, vmem_limit_bytes=VMEM_LIMIT)


def _rms(xf, g):
    ms = jnp.mean(xf * xf, axis=-1, keepdims=True)
    return xf * lax.rsqrt(ms + EPS) * g


def _dot(a, b):
    return jnp.dot(a, b, preferred_element_type=F32)


def _dot_nt(a, b):
    return lax.dot_general(a, b, (((1,), (1,)), ((), ())), preferred_element_type=F32)


def _rope_cols(r, cos_t, sin_t):
    return r * cos_t + pltpu.roll(r, D_ROPE, axis=1) * sin_t


def _proj_kernel(x_ref, g_ref, w1_ref, qn_ref, kvn_ref, wuq_ref, cos_ref, sin_ref, *rest, with_kv, with_resid):
    rest = list(rest)
    if with_kv:
        wuk_ref, wuv_ref = rest[:2]
        rest = rest[2:]
    x = x_ref[...]
    if with_resid:
        f_ref, g3_ref = rest[:2]
        xo_ref = rest[-1]
        rest = rest[2:-1]
        x = x + _rms(f_ref[...], g3_ref[...])
        xo_ref[...] = x
    if with_kv:
        q_ref, ckv_ref, kr_ref, k_ref, v_ref = rest
    else:
        q_ref, ckv_ref, kr_ref = rest
    h = _rms(x, g_ref[...]).astype(BF16)
    y = _dot(h, w1_ref[...])
    cq = _rms(y[:, :Q_LORA], qn_ref[...]).astype(BF16)
    ckv = _rms(y[:, Q_LORA:Q_LORA + KV_LORA], kvn_ref[...])
    ckv_ref[...] = ckv
    cos_t = cos_ref[...]
    sin_t = sin_ref[...]
    kr = _rope_cols(y[:, Q_LORA + KV_LORA:], cos_t, sin_t)
    kr_ref[...] = kr[:, :D_ROPE]
    q = _dot(cq, wuq_ref[...]) * Q_SCALE
    for hd in range(N_HEADS):
        b0 = hd * HEAD_PAD
        q_ref[:, b0:b0 + LANES] = q[:, b0:b0 + LANES].astype(BF16)
        q_ref[:, b0 + LANES:b0 + HEAD_PAD] = _rope_cols(
            q[:, b0 + LANES:b0 + HEAD_PAD], cos_t, sin_t).astype(BF16)
    if with_kv:
        ckv_b = ckv.astype(BF16)
        kr_b = kr.astype(BF16)
        kn = _dot(ckv_b, wuk_ref[...])
        for hd in range(N_HEADS):
            b0 = hd * HEAD_PAD
            k_ref[:, b0:b0 + LANES] = kn[:, hd * D_NOPE:(hd + 1) * D_NOPE].astype(BF16)
            k_ref[:, b0 + LANES:b0 + HEAD_PAD] = kr_b
        v_ref[...] = _dot(ckv_b, wuv_ref[...]).astype(BF16)


def _mla_project(x2d, g, w1, qn, kvn, wuq, cos_t, sin_t, wuk=None, wuv=None, *, bm, resid=None):
    m, d = x2d.shape
    with_kv = wuk is not None
    nt = cos_t.shape[0] // bm
    row = lambda i: (i, 0)
    const = lambda i: (0, 0)
    in_specs = [
        pl.BlockSpec((bm, d), row),
        pl.BlockSpec((1, d), const),
        pl.BlockSpec(w1.shape, const),
        pl.BlockSpec((1, Q_LORA), const),
        pl.BlockSpec((1, KV_LORA), const),
        pl.BlockSpec(wuq.shape, const),
        pl.BlockSpec((bm, LANES), lambda i: (i % nt, 0)),
        pl.BlockSpec((bm, LANES), lambda i: (i % nt, 0)),
    ]
    args = [x2d, g, w1, qn, kvn, wuq, cos_t, sin_t]
    out_shape = [
        jax.ShapeDtypeStruct((m, N_HEADS * HEAD_PAD), BF16),
        jax.ShapeDtypeStruct((m, KV_LORA), F32),
        jax.ShapeDtypeStruct((m, D_ROPE), F32),
    ]
    out_specs = [
        pl.BlockSpec((bm, N_HEADS * HEAD_PAD), row),
        pl.BlockSpec((bm, KV_LORA), row),
        pl.BlockSpec((bm, D_ROPE), row),
    ]
    if with_kv:
        in_specs += [pl.BlockSpec(wuk.shape, const), pl.BlockSpec(wuv.shape, const)]
        args += [wuk, wuv]
        out_shape += [
            jax.ShapeDtypeStruct((m, N_HEADS * HEAD_PAD), BF16),
            jax.ShapeDtypeStruct((m, N_HEADS * D_V), BF16),
        ]
        out_specs += [
            pl.BlockSpec((bm, N_HEADS * HEAD_PAD), row),
            pl.BlockSpec((bm, N_HEADS * D_V), row),
        ]
    if resid is not None:
        in_specs += [pl.BlockSpec((bm, d), row), pl.BlockSpec((1, d), const)]
        args += list(resid)
        out_shape.append(jax.ShapeDtypeStruct((m, d), F32))
        out_specs.append(pl.BlockSpec((bm, d), row))
    return pl.pallas_call(
        functools.partial(_proj_kernel, with_kv=with_kv, with_resid=resid is not None),
        grid=(m // bm,),
        in_specs=in_specs,
        out_specs=out_specs,
        out_shape=out_shape,
        compiler_params=_cparams(("arbitrary",)),
        name="mla_project",
    )(*args)


def _attn_kernel(q_ref, k_ref, v_ref, o_ref, m_scr, l_scr, acc_scr, *, blk, bk, sub):
    qi = pl.program_id(2)
    m_scr[...] = jnp.full(m_scr.shape, NEG_BIG, F32)
    l_scr[...] = jnp.zeros(l_scr.shape, F32)
    acc_scr[...] = jnp.zeros(acc_scr.shape, F32)

    def step(start, diag):
        scores = []
        for c in range(blk // sub):
            nk = bk if diag is None else min(bk, (c + 1) * sub - diag)
            if nk <= 0:
                continue
            k = k_ref[0, pl.ds(start, nk), :]
            st = _dot_nt(k, q_ref[0, c * sub:(c + 1) * sub, :])
            if diag is not None and diag + nk > c * sub + CHUNK:
                kc = (lax.broadcasted_iota(jnp.int32, st.shape, 0) + diag) // CHUNK
                qc = (lax.broadcasted_iota(jnp.int32, st.shape, 1) + c * sub) // CHUNK
                st = jnp.where(kc <= qc, st, NEG_BIG)
            scores.append((c, nk, st))
        for c, nk, st in scores:
            cs = slice(c * sub, (c + 1) * sub)
            v = v_ref[0, pl.ds(start, nk), :]
            m_prev = m_scr[:, cs]
            m_new = jnp.maximum(m_prev, jnp.max(st, axis=0, keepdims=True))
            alpha = jnp.exp2(m_prev - m_new)
            p = jnp.exp2(st - m_new)
            l_scr[:, cs] = alpha * l_scr[:, cs] + jnp.sum(p, axis=0, keepdims=True)
            pv = lax.dot_general(v, p.astype(BF16), (((0,), (0,)), ((), ())),
                                 preferred_element_type=F32)
            acc_scr[:, cs] = alpha * acc_scr[:, cs] + pv
            m_scr[:, cs] = m_new

    def body(ki, carry):
        step(pl.multiple_of(ki * bk, bk), None)
        return carry

    per_q = blk // bk
    lax.fori_loop(0, qi * per_q, body, 0)
    for d in range(per_q):
        step(pl.multiple_of(qi * blk + d * bk, bk), d * bk)
    o_ref[0] = jnp.transpose(acc_scr[...] / l_scr[...]).astype(o_ref.dtype)


def _attention(q, k, v, *, blk, bk):
    b, s, _ = q.shape
    return pl.pallas_call(
        functools.partial(_attn_kernel, blk=blk, bk=bk, sub=min(bk, HEAD_PAD)),
        grid=(b, N_HEADS, s // blk),
        in_specs=[
            pl.BlockSpec((1, blk, HEAD_PAD), lambda bi, h, qi: (bi, qi, h)),
            pl.BlockSpec((1, s, HEAD_PAD), lambda bi, h, qi: (bi, 0, h)),
            pl.BlockSpec((1, s, D_V), lambda bi, h, qi: (bi, 0, h)),
        ],
        out_specs=pl.BlockSpec((1, blk, D_V), lambda bi, h, qi: (bi, qi, h)),
        out_shape=jax.ShapeDtypeStruct((b, s, N_HEADS * D_V), BF16),
        scratch_shapes=[
            pltpu.VMEM((1, blk), F32),
            pltpu.VMEM((1, blk), F32),
            pltpu.VMEM((D_V, blk), F32),
        ],
        compiler_params=_cparams(("arbitrary", "arbitrary", "arbitrary")),
        name="prompt_attention",
    )(q, k, v)


def _qlat_kernel(q_ref, wukt_ref, qlat_ref, qrope_ref):
    q = q_ref[...]
    qlat_ref[0] = _dot(q[:, :D_NOPE], wukt_ref[0]).astype(BF16)
    qrope_ref[0] = q[:, LANES:]


def _q_latent(q, wukt):
    m = q.shape[0]
    return pl.pallas_call(
        _qlat_kernel,
        grid=(N_HEADS,),
        in_specs=[
            pl.BlockSpec((m, HEAD_PAD), lambda h: (0, h)),
            pl.BlockSpec((1, D_NOPE, KV_LORA), lambda h: (h, 0, 0)),
        ],
        out_specs=[
            pl.BlockSpec((1, m, KV_LORA), lambda h: (h, 0, 0)),
            pl.BlockSpec((1, m, LANES), lambda h: (h, 0, 0)),
        ],
        out_shape=[
            jax.ShapeDtypeStruct((N_HEADS, m, KV_LORA), BF16),
            jax.ShapeDtypeStruct((N_HEADS, m, LANES), BF16),
        ],
        compiler_params=_cparams(("arbitrary",)),
        name="sample_q_latent",
    )(q, wukt)


def _sattn_kernel(qlat_ref, qrope_ref, cckv_ref, ckr_ref, nckv_ref, nkr_ref, o_ref, *, past, ns):
    rows = N_HEADS * ns
    ql = qlat_ref[...].reshape(rows, KV_LORA)
    qr = qrope_ref[...].reshape(rows, LANES)[:, :D_ROPE]
    cc = cckv_ref[0].astype(BF16)
    ck = ckr_ref[0].astype(BF16)
    nc = nckv_ref[...].astype(BF16)
    nk = nkr_ref[...].astype(BF16)
    s1 = _dot_nt(ql, cc) + _dot_nt(qr, ck)
    s2 = _dot_nt(ql, nc) + _dot_nt(qr, nk)
    qt = lax.broadcasted_iota(jnp.int32, s2.shape, 0) % ns
    kt = lax.broadcasted_iota(jnp.int32, s2.shape, 1)
    s2 = jnp.where((past + kt) // CHUNK <= (past + qt) // CHUNK, s2, NEG_BIG)
    m = jnp.maximum(jnp.max(s1, axis=1, keepdims=True), jnp.max(s2, axis=1, keepdims=True))
    p1 = jnp.exp2(s1 - m)
    p2 = jnp.exp2(s2 - m)
    l = jnp.sum(p1, axis=1, keepdims=True) + jnp.sum(p2, axis=1, keepdims=True)
    o = (_dot(p1.astype(BF16), cc) + _dot(p2.astype(BF16), nc)) / l
    o_ref[...] = o.astype(BF16).reshape(N_HEADS, ns, KV_LORA)


def _sample_attention(qlat, qrope, cache_ckv, cache_kr, new_ckv, new_kr, *, layer, ns):
    _, nb, past, _ = cache_ckv.shape
    m = nb * ns
    return pl.pallas_call(
        functools.partial(_sattn_kernel, past=past, ns=ns),
        grid=(nb,),
        in_specs=[
            pl.BlockSpec((N_HEADS, ns, KV_LORA), lambda b: (0, b, 0)),
            pl.BlockSpec((N_HEADS, ns, LANES), lambda b: (0, b, 0)),
            pl.BlockSpec((None, 1, past, KV_LORA), lambda b: (layer, b, 0, 0)),
            pl.BlockSpec((None, 1, past, D_ROPE), lambda b: (layer, b, 0, 0)),
            pl.BlockSpec((ns, KV_LORA), lambda b: (b, 0)),
            pl.BlockSpec((ns, D_ROPE), lambda b: (b, 0)),
        ],
        out_specs=pl.BlockSpec((N_HEADS, ns, KV_LORA), lambda b: (0, b, 0)),
        out_shape=jax.ShapeDtypeStruct((N_HEADS, m, KV_LORA), BF16),
        compiler_params=_cparams(("arbitrary",)),
        name="sample_attention",
    )(qlat, qrope, cache_ckv, cache_kr, new_ckv, new_kr)


def _ouv_kernel(olat_ref, wuv_ref, o_ref):
    o_ref[...] = _dot(olat_ref[0], wuv_ref[...]).astype(BF16)


def _o_from_latent(olat, wuv):
    m = olat.shape[1]
    return pl.pallas_call(
        _ouv_kernel,
        grid=(N_HEADS,),
        in_specs=[
            pl.BlockSpec((1, m, KV_LORA), lambda h: (h, 0, 0)),
            pl.BlockSpec((KV_LORA, D_V), lambda h: (0, h)),
        ],
        out_specs=pl.BlockSpec((m, D_V), lambda h: (0, h)),
        out_shape=jax.ShapeDtypeStruct((m, N_HEADS * D_V), BF16),
        compiler_params=_cparams(("arbitrary",)),
        name="sample_o_from_latent",
    )(olat, wuv)


def _post_kernel(o_ref, w_ref, x_ref, g1_ref, g2_ref, xo_ref, h2_ref):
    xn = x_ref[...] + _rms(_dot(o_ref[...], w_ref[...]), g1_ref[...])
    xo_ref[...] = xn
    h2_ref[...] = _rms(xn, g2_ref[...]).astype(BF16)


def _attn_out(o2d, w_o, x2d, g1, g2, *, bm):
    m, d = x2d.shape
    row = lambda i: (i, 0)
    const = lambda i: (0, 0)
    return pl.pallas_call(
        _post_kernel,
        grid=(m // bm,),
        in_specs=[
            pl.BlockSpec((bm, o2d.shape[1]), row),
            pl.BlockSpec(w_o.shape, const),
            pl.BlockSpec((bm, d), row),
            pl.BlockSpec((1, d), const),
            pl.BlockSpec((1, d), const),
        ],
        out_specs=[pl.BlockSpec((bm, d), row), pl.BlockSpec((bm, d), row)],
        out_shape=[jax.ShapeDtypeStruct((m, d), F32), jax.ShapeDtypeStruct((m, d), BF16)],
        compiler_params=_cparams(("arbitrary",)),
        name="attn_out_post",
    )(o2d, w_o, x2d, g1, g2)


def _pool_kernel(x_ref, halo_ref, g0_ref, w_ref, sc_ref, g1_ref, g2_ref, *rest,
                 bm, blocks_per_seq, pos0, halo_normed, with_resid):
    i = pl.program_id(0)
    x = x_ref[...]
    halo = halo_ref[...]
    if with_resid:
        f_ref, fhalo_ref, g3_ref, xo_ref, h2_ref, tail_ref = rest
        x = x + _rms(f_ref[...], g3_ref[...])
        halo = halo + _rms(fhalo_ref[...], g3_ref[...])
    else:
        xo_ref, h2_ref, tail_ref = rest
    g0 = g0_ref[...]
    h = _rms(x, g0)
    if not halo_normed:
        halo = _rms(halo, g0)
        halo = jnp.where(i % blocks_per_seq == 0, 0.0, halo)
    cat = jnp.concatenate([halo, h], axis=0)
    t = (i % blocks_per_seq) * bm + lax.broadcasted_iota(jnp.int32, (bm, 1), 0)
    posf = (t + pos0).astype(F32)
    gd = x.shape[1] // len(POOL_WINDOWS)
    ys = []
    for g, w in enumerate(POOL_WINDOWS):
        c = cat[:, g * gd:(g + 1) * gd]
        acc = c
        span = 1
        while span < w:
            acc = acc + pltpu.roll(acc, span, axis=0)
            span *= 2
        cnt = jnp.minimum(jnp.float32(w), posf + 1.0)
        mean = acc[POOL_HALO:] / cnt
        dlt = (mean - h[:, g * gd:(g + 1) * gd]).astype(BF16)
        ys.append(_dot(dlt, w_ref[g]))
    y = jnp.concatenate(ys, axis=1) * sc_ref[...]
    xn = x + _rms(y, g1_ref[...])
    xo_ref[...] = xn
    h2_ref[...] = _rms(xn, g2_ref[...]).astype(BF16)
    tail_ref[0] = h[bm - POOL_HALO:]


def _pool_layer(x2d, halo_src, g0, w_pool, scale, g1, g2, *, bm, rows_per_seq, pos0, halo_normed, resid=None):
    m, d = x2d.shape
    bps = rows_per_seq // bm
    nseq = m // rows_per_seq
    row = lambda i: (i, 0)
    const = lambda i: (0, 0)
    if halo_normed:
        halo_map = lambda i: (i, 0)
    else:
        hb = bm // POOL_HALO
        halo_map = lambda i: (jnp.maximum(i * hb - 1, 0), 0)
    in_specs = [
        pl.BlockSpec((bm, d), row),
        pl.BlockSpec((POOL_HALO, d), halo_map),
        pl.BlockSpec((1, d), const),
        pl.BlockSpec(w_pool.shape, lambda i: (0, 0, 0)),
        pl.BlockSpec((1, d), const),
        pl.BlockSpec((1, d), const),
        pl.BlockSpec((1, d), const),
    ]
    args = [x2d, halo_src, g0, w_pool, scale, g1, g2]
    if resid is not None:
        assert not halo_normed
        in_specs += [pl.BlockSpec((bm, d), row), pl.BlockSpec((POOL_HALO, d), halo_map),
                     pl.BlockSpec((1, d), const)]
        args += [resid[0], resid[0], resid[1]]
    return pl.pallas_call(
        functools.partial(_pool_kernel, bm=bm, blocks_per_seq=bps, pos0=pos0, halo_normed=halo_normed,
                          with_resid=resid is not None),
        grid=(m // bm,),
        in_specs=in_specs,
        out_specs=[
            pl.BlockSpec((bm, d), row),
            pl.BlockSpec((bm, d), row),
            pl.BlockSpec((1, POOL_HALO, d), lambda i: (i // bps, 0, 0)),
        ],
        out_shape=[
            jax.ShapeDtypeStruct((m, d), F32),
            jax.ShapeDtypeStruct((m, d), BF16),
            jax.ShapeDtypeStruct((nseq, POOL_HALO, d), F32),
        ],
        compiler_params=_cparams(("arbitrary",)),
        name="pool_layer",
    )(*args)


def _conv3(u, prev1, prev2, cw_ref, cb_ref):
    return cb_ref[...] + cw_ref[0:1, :] * prev2 + cw_ref[1:2, :] * prev1 + cw_ref[2:3, :] * u


def _ffn_tail(j, nf, act, wd_ref, x_ref, g3_ref, xo_ref, acc_scr):
    part = _dot(act.astype(BF16), wd_ref[...])

    @pl.when(j == 0)
    def _():
        acc_scr[...] = part

    @pl.when(j > 0)
    def _():
        acc_scr[...] += part

    @pl.when(j == nf - 1)
    def _():
        xo_ref[...] = x_ref[...] + _rms(acc_scr[...], g3_ref[...])


def _ffn_prompt_kernel(h_ref, wg_ref, wv_ref, cwg_ref, cwv_ref, cbg_ref, cbv_ref, wd_ref,
                       f_ref, tg_ref, tv_ref, cg_scr, cv_scr, act_scr, *, bm, blocks_per_seq, nf):
    t = pl.program_id(0)
    cur = jnp.minimum(t, pl.num_programs(0) - 2)
    i = cur // nf
    j = cur % nf
    prev = jnp.maximum(t - 1, 0)

    @pl.when(t == 0)
    def _():
        act_scr[...] = jnp.zeros(act_scr.shape, BF16)

    @pl.when(prev % nf == 0)
    def _():
        f_ref[...] = jnp.zeros(f_ref.shape, F32)

    f_ref[...] += _dot(act_scr[(t + 1) % 2], wd_ref[...])

    h = h_ref[...]
    first = i % blocks_per_seq == 0

    def branch(w_ref, cw_ref, cb_ref, carry_scr, tail_ref):
        u = _dot(h, w_ref[...])
        prev8 = jnp.where(first, 0.0, carry_scr[j])
        c = _conv3(u, pltpu.roll(u, 1, axis=0), pltpu.roll(u, 2, axis=0), cw_ref, cb_ref)
        head = jnp.concatenate([prev8, u[:SUBLANES]], axis=0)
        c_head = _conv3(head, pltpu.roll(head, 1, axis=0), pltpu.roll(head, 2, axis=0),
                        cw_ref, cb_ref)[SUBLANES:]
        c = jnp.concatenate([c_head, c[SUBLANES:]], axis=0)
        last8 = u[bm - SUBLANES:]
        carry_scr[j] = last8
        tail_ref[0] = last8
        return c

    gate = branch(wg_ref, cwg_ref, cbg_ref, cg_scr, tg_ref)
    val = branch(wv_ref, cwv_ref, cbv_ref, cv_scr, tv_ref)
    act_scr[t % 2] = (gate * jax.nn.sigmoid(gate) * val).astype(BF16)


def _resid_kernel(f_ref, x_ref, g_ref, xo_ref):
    xo_ref[...] = x_ref[...] + _rms(f_ref[...], g_ref[...])


def _residual_norm(f2d, x2d, g, *, bm):
    m, d = x2d.shape
    row = lambda i: (i, 0)
    return pl.pallas_call(
        _resid_kernel,
        grid=(m // bm,),
        in_specs=[pl.BlockSpec((bm, d), row), pl.BlockSpec((bm, d), row), pl.BlockSpec((1, d), lambda i: (0, 0))],
        out_specs=pl.BlockSpec((bm, d), row),
        out_shape=jax.ShapeDtypeStruct((m, d), F32),
        compiler_params=_cparams(("arbitrary",)),
        name="residual_norm",
    )(f2d, x2d, g)


def _ffn_sample_kernel(h_ref, wg_ref, wv_ref, cwg_ref, cwv_ref, cbg_ref, cbv_ref, wd_ref, x_ref, g3_ref,
                       stg_ref, stv_ref, xo_ref, ug_ref, uv_ref, acc_scr, *, ns, nf):
    j = pl.program_id(0)
    h = h_ref[...]
    m = h.shape[0]
    t = lax.broadcasted_iota(jnp.int32, (m, 1), 0) % ns

    def branch(w_ref, cw_ref, cb_ref, st_ref, u_ref):
        u = _dot(h, w_ref[...])
        u_ref[...] = u
        st = st_ref[...]

        def per_row(k):
            row = st[:, k:k + 1, :]
            return jnp.broadcast_to(row, (m // ns, ns, row.shape[2])).reshape(m, row.shape[2])

        older, newer = per_row(0), per_row(1)
        prev1 = jnp.where(t < 1, newer, pltpu.roll(u, 1, axis=0))
        prev2 = jnp.where(t < 1, older, jnp.where(t < 2, newer, pltpu.roll(u, 2, axis=0)))
        return _conv3(u, prev1, prev2, cw_ref, cb_ref)

    gate = branch(wg_ref, cwg_ref, cbg_ref, stg_ref, ug_ref)
    val = branch(wv_ref, cwv_ref, cbv_ref, stv_ref, uv_ref)
    act = gate * jax.nn.sigmoid(gate) * val
    _ffn_tail(j, nf, act, wd_ref, x_ref, g3_ref, xo_ref, acc_scr)


def _ffn_prompt(h2d, w_up, cw, cb, w_down, *, layer, bm, bf, rows_per_seq):
    m, d = h2d.shape
    dff = w_down.shape[1]
    nf = dff // bf
    bps = rows_per_seq // bm
    nseq = m // rows_per_seq
    n = (m // bm) * nf
    cur = lambda t: jnp.minimum(t, n - 1)
    prv = lambda t: jnp.maximum(t - 1, 0)
    f, tg, tv = pl.pallas_call(
        functools.partial(_ffn_prompt_kernel, bm=bm, blocks_per_seq=bps, nf=nf),
        grid=(n + 1,),
        in_specs=[
            pl.BlockSpec((bm, d), lambda t: (cur(t) // nf, 0)),
            pl.BlockSpec((None, d, bf), lambda t: (layer, 0, cur(t) % nf)),
            pl.BlockSpec((None, d, bf), lambda t: (layer, 0, nf + cur(t) % nf)),
            pl.BlockSpec((None, CONV_WIDTH, bf), lambda t: (layer, 0, cur(t) % nf)),
            pl.BlockSpec((None, CONV_WIDTH, bf), lambda t: (layer, 0, nf + cur(t) % nf)),
            pl.BlockSpec((None, 1, bf), lambda t: (layer, 0, cur(t) % nf)),
            pl.BlockSpec((None, 1, bf), lambda t: (layer, 0, nf + cur(t) % nf)),
            pl.BlockSpec((None, bf, d), lambda t: (layer, prv(t) % nf, 0)),
        ],
        out_specs=[
            pl.BlockSpec((bm, d), lambda t: (prv(t) // nf, 0)),
            pl.BlockSpec((1, SUBLANES, bf), lambda t: (cur(t) // nf, 0, cur(t) % nf)),
            pl.BlockSpec((1, SUBLANES, bf), lambda t: (cur(t) // nf, 0, cur(t) % nf)),
        ],
        out_shape=[
            jax.ShapeDtypeStruct((m, d), F32),
            jax.ShapeDtypeStruct((m // bm, SUBLANES, dff), F32),
            jax.ShapeDtypeStruct((m // bm, SUBLANES, dff), F32),
        ],
        scratch_shapes=[
            pltpu.VMEM((nf, SUBLANES, bf), F32),
            pltpu.VMEM((nf, SUBLANES, bf), F32),
            pltpu.VMEM((2, bm, bf), BF16),
        ],
        compiler_params=_cparams(("arbitrary",)),
        name="ffn_prompt",
    )(h2d, w_up, w_up, cw, cw, cb, cb, w_down)
    k = CONV_WIDTH - 1
    tg = tg.reshape(nseq, bps, SUBLANES, dff)[:, bps - 1, SUBLANES - k:]
    tv = tv.reshape(nseq, bps, SUBLANES, dff)[:, bps - 1, SUBLANES - k:]
    return f, jnp.concatenate([tg, tv], axis=-1)


def _ffn_sample(h2d, w_up, cw, cb, w_down, x2d, g3, state, *, layer, bf, ns):
    m, d = x2d.shape
    dff = w_down.shape[1]
    nf = dff // bf
    nb = m // ns
    k = CONV_WIDTH - 1
    assert state.shape[1:] == (nb, k, 2 * dff) and k == 2
    full = lambda j: (0, 0)
    colg = lambda j: (0, j)
    colv = lambda j: (0, nf + j)
    xo, ug, uv = pl.pallas_call(
        functools.partial(_ffn_sample_kernel, ns=ns, nf=nf),
        grid=(nf,),
        in_specs=[
            pl.BlockSpec((m, d), full),
            pl.BlockSpec((None, d, bf), lambda j: (layer, 0, j)),
            pl.BlockSpec((None, d, bf), lambda j: (layer, 0, nf + j)),
            pl.BlockSpec((None, CONV_WIDTH, bf), lambda j: (layer, 0, j)),
            pl.BlockSpec((None, CONV_WIDTH, bf), lambda j: (layer, 0, nf + j)),
            pl.BlockSpec((None, 1, bf), lambda j: (layer, 0, j)),
            pl.BlockSpec((None, 1, bf), lambda j: (layer, 0, nf + j)),
            pl.BlockSpec((None, bf, d), lambda j: (layer, j, 0)),
            pl.BlockSpec((m, d), full),
            pl.BlockSpec((1, d), full),
            pl.BlockSpec((None, nb, k, bf), lambda j: (layer, 0, 0, j)),
            pl.BlockSpec((None, nb, k, bf), lambda j: (layer, 0, 0, nf + j)),
        ],
        out_specs=[
            pl.BlockSpec((m, d), full),
            pl.BlockSpec((m, bf), colg),
            pl.BlockSpec((m, bf), colg),
        ],
        out_shape=[
            jax.ShapeDtypeStruct((m, d), F32),
            jax.ShapeDtypeStruct((m, dff), F32),
            jax.ShapeDtypeStruct((m, dff), F32),
        ],
        scratch_shapes=[pltpu.VMEM((m, d), F32)],
        compiler_params=_cparams(("arbitrary",)),
        name="ffn_sample",
    )(h2d, w_up, w_up, cw, cw, cb, cb, w_down, x2d, g3, state, state)
    u = jnp.concatenate([ug, uv], axis=-1).reshape(nb, ns, 2 * dff)
    return xo, u[:, ns - k:]


def _rope_tables(pos, reps):
    half = D_ROPE // 2
    inv = ROPE_BASE ** (-jnp.arange(half, dtype=F32) / half)
    ang = pos[:, None] * inv[None, :]
    c, s = jnp.cos(ang), jnp.sin(ang)
    z = jnp.zeros((pos.shape[0], LANES - D_ROPE), F32)
    cos_t = jnp.concatenate([c, c, z], axis=1)
    sin_t = jnp.concatenate([s, s, z], axis=1)
    return jnp.tile(cos_t, (reps, 1)), jnp.tile(sin_t, (reps, 1))


def _signed_partner(w):
    half = D_ROPE // 2
    return jnp.concatenate([-w[..., half:], w[..., :half]], axis=-1)


def _stage_mla_weights(w_dq, w_uq, w_dkv, w_uk, w_uv, w_o):
    w_kr = w_dkv[:, KV_LORA:]
    w1 = jnp.concatenate([w_dq, w_dkv[:, :KV_LORA], w_kr, _signed_partner(w_kr)], axis=1).astype(BF16)
    wq = w_uq.reshape(Q_LORA, N_HEADS, D_NOPE + D_ROPE)
    wq_rope = wq[..., D_NOPE:]
    wuq = jnp.concatenate([wq[..., :D_NOPE], wq_rope, _signed_partner(wq_rope)], axis=-1)
    wuq = wuq.reshape(Q_LORA, N_HEADS * HEAD_PAD).astype(BF16)
    wuk = w_uk.reshape(KV_LORA, N_HEADS * D_NOPE).astype(BF16)
    wuv = w_uv.reshape(KV_LORA, N_HEADS * D_V).astype(BF16)
    wukt = jnp.transpose(w_uk, (1, 2, 0)).astype(BF16)
    return w1, wuq, wuk, wuv, wukt, w_o.astype(BF16)


def _block_rows(m, target):
    bm = min(m, target)
    while m % bm:
        bm //= 2
    return bm


def kernel(x_prompt, x_sample, cache_ckv, cache_krope, state_pool, state_conv, norm_g, mla_w_dq, mla_q_norm, mla_w_uq, mla_w_dkv, mla_kv_norm, mla_w_uk, mla_w_uv, mla_w_o, pool_w, pool_scale, ffn_w_up, ffn_conv_w, ffn_conv_b, ffn_w_down):
    nbp, sp, d = x_prompt.shape
    nbs, ss, _ = x_sample.shape
    past = cache_ckv.shape[2]
    depth = norm_g.shape[0]
    mp, ms = nbp * sp, nbs * ss
    xp = x_prompt.reshape(mp, d)
    xs = x_sample.reshape(ms, d)

    bm_proj = _block_rows(sp, 256)
    bm_post = _block_rows(sp, 512)
    bm_ffn = _block_rows(sp, 1024)
    bm_pool = _block_rows(sp, 256)
    blk_attn = _block_rows(sp, 4096)
    bk_attn = _block_rows(blk_attn, 512)
    bf = 512
    bms = _block_rows(ms, 256)

    cos_p, sin_p = _rope_tables(jnp.arange(sp, dtype=jnp.int32).astype(F32), 1)
    cos_s, sin_s = _rope_tables((past + jnp.arange(ss, dtype=jnp.int32)).astype(F32), max(bms // ss, 1))

    w_up = ffn_w_up.astype(BF16)
    w_down = ffn_w_down.astype(BF16)
    cb = ffn_conv_b.reshape(depth, 1, -1)

    outs = {k: [] for k in ("ckv_p", "kr_p", "pool_p", "conv_p", "ckv_s", "kr_s", "pool_s", "conv_s")}
    pend = None
    for i in range(depth):
        g = norm_g[i].reshape(4, 1, d)
        j = i // 2
        if i % 2 == 0:
            w1, wuq, wuk, wuv, wukt, wo = _stage_mla_weights(
                mla_w_dq[j], mla_w_uq[j], mla_w_dkv[j], mla_w_uk[j], mla_w_uv[j], mla_w_o[j])
            qn = mla_q_norm[j].reshape(1, Q_LORA)
            kvn = mla_kv_norm[j].reshape(1, KV_LORA)
            proj = _mla_project(xp, g[0], w1, qn, kvn, wuq, cos_p, sin_p, wuk, wuv, bm=bm_proj, resid=pend)
            q, ckv, kr, k, v = proj[:5]
            if pend is not None:
                xp = proj[5]
            o = _attention(q.reshape(nbp, sp, -1), k.reshape(nbp, sp, -1), v.reshape(nbp, sp, -1),
                           blk=blk_attn, bk=bk_attn)
            xp, hp = _attn_out(o.reshape(mp, -1), wo, xp, g[1], g[2], bm=bm_post)
            outs["ckv_p"].append(ckv.reshape(nbp, sp, KV_LORA))
            outs["kr_p"].append(kr.reshape(nbp, sp, D_ROPE))
            qs, ckv_s, kr_s = _mla_project(xs, g[0], w1, qn, kvn, wuq, cos_s, sin_s, bm=bms)
            qlat, qrope = _q_latent(qs, wukt)
            olat = _sample_attention(qlat, qrope, cache_ckv, cache_krope, ckv_s, kr_s, layer=j, ns=ss)
            os_ = _o_from_latent(olat, wuv)
            xs, hs = _attn_out(os_, wo, xs, g[1], g[2], bm=bms)
            outs["ckv_s"].append(ckv_s.reshape(nbs, ss, KV_LORA))
            outs["kr_s"].append(kr_s.reshape(nbs, ss, D_ROPE))
        else:
            wp = pool_w[j].astype(BF16)
            sc = pool_scale[j].reshape(1, d)
            xp, hp, tail_p = _pool_layer(xp, xp, g[0], wp, sc, g[1], g[2], bm=bm_pool,
                                         rows_per_seq=sp, pos0=0, halo_normed=False, resid=pend)
            outs["pool_p"].append(tail_p[:, POOL_HALO - POOL_HIST:])
            hist = jnp.pad(state_pool[j], ((0, 0), (POOL_HALO - POOL_HIST, 0), (0, 0))).reshape(-1, d)
            xs, hs, tail_s = _pool_layer(xs, hist, g[0], wp, sc, g[1], g[2], bm=ss,
                                         rows_per_seq=ss, pos0=past, halo_normed=True)
            outs["pool_s"].append(tail_s[:, POOL_HALO - POOL_HIST:])
        fp, cv_p = _ffn_prompt(hp, w_up, ffn_conv_w, cb, w_down, layer=i, bm=bm_ffn, bf=bf, rows_per_seq=sp)
        pend = (fp, g[3])
        xs, cv_s = _ffn_sample(hs, w_up, ffn_conv_w, cb, w_down, xs, g[3], state_conv, layer=i, bf=bf, ns=ss)
        outs["conv_p"].append(cv_p)
        outs["conv_s"].append(cv_s)
    xp = _residual_norm(pend[0], xp, pend[1], bm=bm_post)
    st = lambda k: jnp.stack(outs[k])
    return (xp.reshape(nbp, sp, d), xs.reshape(nbs, ss, d),
            st("ckv_p"), st("kr_p"), st("pool_p"), st("conv_p"),
            st("ckv_s"), st("kr_s"), st("pool_s"), st("conv_s"))
```

```python
import functools

import numpy as np
import jax
import jax.numpy as jnp
from jax import lax
from jax.experimental import pallas as pl
from jax.experimental.pallas import tpu as pltpu

F32 = jnp.float32
BF16 = jnp.bfloat16

CHUNK = 64
N_HEADS = 16
Q_LORA = 512
KV_LORA = 512
D_NOPE = 128
D_ROPE = 64
D_V = 128
ROPE_BASE = 10000.0
ATTN_SCALE = (D_NOPE + D_ROPE) ** -0.5
Q_SCALE = ATTN_SCALE * float(np.log2(np.e))
POOL_WINDOWS = (2, 4, 8, 16)
POOL_HIST = max(POOL_WINDOWS) - 1
CONV_WIDTH = 3
EPS = 1e-6

LANES = 128
SUBLANES = 8
HEAD_PAD = 2 * LANES
POOL_HALO = 16
VMEM_LIMIT = 58 * 1024 * 1024
NEG_BIG = -1e30


def _cparams(sem):
    return pltpu.CompilerParams(dimension_semantics=sem, vmem_limit_bytes=VMEM_LIMIT)


def _rms(xf, g):
    ms = jnp.mean(xf * xf, axis=-1, keepdims=True)
    return xf * lax.rsqrt(ms + EPS) * g


def _dot(a, b):
    return jnp.dot(a, b, preferred_element_type=F32)


def _dot_nt(a, b):
    return lax.dot_general(a, b, (((1,), (1,)), ((), ())), preferred_element_type=F32)


def _rope_cols(r, cos_t, sin_t):
    return r * cos_t + pltpu.roll(r, D_ROPE, axis=1) * sin_t


def _proj_kernel(x_ref, g_ref, w1_ref, qn_ref, kvn_ref, wuq_ref, cos_ref, sin_ref, *rest, with_kv, with_resid):
    rest = list(rest)
    if with_kv:
        wuk_ref, wuv_ref = rest[:2]
        rest = rest[2:]
    x = x_ref[...]
    if with_resid:
        f_ref, g3_ref = rest[:2]
        xo_ref = rest[-1]
        rest = rest[2:-1]
        x = x + _rms(f_ref[...], g3_ref[...])
        xo_ref[...] = x
    if with_kv:
        q_ref, ckv_ref, kr_ref, k_ref, v_ref = rest
    else:
        q_ref, ckv_ref, kr_ref = rest
    h = _rms(x, g_ref[...]).astype(BF16)
    y = _dot(h, w1_ref[...])
    cq = _rms(y[:, :Q_LORA], qn_ref[...]).astype(BF16)
    ckv = _rms(y[:, Q_LORA:Q_LORA + KV_LORA], kvn_ref[...])
    ckv_ref[...] = ckv
    cos_t = cos_ref[...]
    sin_t = sin_ref[...]
    kr = _rope_cols(y[:, Q_LORA + KV_LORA:], cos_t, sin_t)
    kr_ref[...] = kr[:, :D_ROPE]
    q = _dot(cq, wuq_ref[...]) * Q_SCALE
    for hd in range(N_HEADS):
        b0 = hd * HEAD_PAD
        q_ref[:, b0:b0 + LANES] = q[:, b0:b0 + LANES].astype(BF16)
        q_ref[:, b0 + LANES:b0 + HEAD_PAD] = _rope_cols(
            q[:, b0 + LANES:b0 + HEAD_PAD], cos_t, sin_t).astype(BF16)
    if with_kv:
        ckv_b = ckv.astype(BF16)
        kr_b = kr.astype(BF16)
        kn = _dot(ckv_b, wuk_ref[...])
        for hd in range(N_HEADS):
            b0 = hd * HEAD_PAD
            k_ref[:, b0:b0 + LANES] = kn[:, hd * D_NOPE:(hd + 1) * D_NOPE].astype(BF16)
            k_ref[:, b0 + LANES:b0 + HEAD_PAD] = kr_b
        v_ref[...] = _dot(ckv_b, wuv_ref[...]).astype(BF16)


def _mla_project(x2d, g, w1, qn, kvn, wuq, cos_t, sin_t, wuk=None, wuv=None, *, bm, resid=None):
    m, d = x2d.shape
    with_kv = wuk is not None
    nt = cos_t.shape[0] // bm
    row = lambda i: (i, 0)
    const = lambda i: (0, 0)
    in_specs = [
        pl.BlockSpec((bm, d), row),
        pl.BlockSpec((1, d), const),
        pl.BlockSpec(w1.shape, const),
        pl.BlockSpec((1, Q_LORA), const),
        pl.BlockSpec((1, KV_LORA), const),
        pl.BlockSpec(wuq.shape, const),
        pl.BlockSpec((bm, LANES), lambda i: (i % nt, 0)),
        pl.BlockSpec((bm, LANES), lambda i: (i % nt, 0)),
    ]
    args = [x2d, g, w1, qn, kvn, wuq, cos_t, sin_t]
    out_shape = [
        jax.ShapeDtypeStruct((m, N_HEADS * HEAD_PAD), BF16),
        jax.ShapeDtypeStruct((m, KV_LORA), F32),
        jax.ShapeDtypeStruct((m, D_ROPE), F32),
    ]
    out_specs = [
        pl.BlockSpec((bm, N_HEADS * HEAD_PAD), row),
        pl.BlockSpec((bm, KV_LORA), row),
        pl.BlockSpec((bm, D_ROPE), row),
    ]
    if with_kv:
        in_specs += [pl.BlockSpec(wuk.shape, const), pl.BlockSpec(wuv.shape, const)]
        args += [wuk, wuv]
        out_shape += [
            jax.ShapeDtypeStruct((m, N_HEADS * HEAD_PAD), BF16),
            jax.ShapeDtypeStruct((m, N_HEADS * D_V), BF16),
        ]
        out_specs += [
            pl.BlockSpec((bm, N_HEADS * HEAD_PAD), row),
            pl.BlockSpec((bm, N_HEADS * D_V), row),
        ]
    if resid is not None:
        in_specs += [pl.BlockSpec((bm, d), row), pl.BlockSpec((1, d), const)]
        args += list(resid)
        out_shape.append(jax.ShapeDtypeStruct((m, d), F32))
        out_specs.append(pl.BlockSpec((bm, d), row))
    return pl.pallas_call(
        functools.partial(_proj_kernel, with_kv=with_kv, with_resid=resid is not None),
        grid=(m // bm,),
        in_specs=in_specs,
        out_specs=out_specs,
        out_shape=out_shape,
        compiler_params=_cparams(("arbitrary",)),
        name="mla_project",
    )(*args)


def _attn_kernel(q_ref, k_ref, v_ref, o_ref, m_scr, l_scr, acc_scr, *, blk, bk, sub):
    qi = pl.program_id(2)
    m_scr[...] = jnp.full(m_scr.shape, NEG_BIG, F32)
    l_scr[...] = jnp.zeros(l_scr.shape, F32)
    acc_scr[...] = jnp.zeros(acc_scr.shape, F32)

    def step(start, diag):
        scores = []
        for c in range(blk // sub):
            nk = bk if diag is None else min(bk, (c + 1) * sub - diag)
            if nk <= 0:
                continue
            k = k_ref[0, pl.ds(start, nk), :]
            st = _dot_nt(k, q_ref[0, c * sub:(c + 1) * sub, :])
            if diag is not None and diag + nk > c * sub + CHUNK:
                kc = (lax.broadcasted_iota(jnp.int32, st.shape, 0) + diag) // CHUNK
                qc = (lax.broadcasted_iota(jnp.int32, st.shape, 1) + c * sub) // CHUNK
                st = jnp.where(kc <= qc, st, NEG_BIG)
            scores.append((c, nk, st))
        for c, nk, st in scores:
            cs = slice(c * sub, (c + 1) * sub)
            v = v_ref[0, pl.ds(start, nk), :]
            m_prev = m_scr[:, cs]
            m_new = jnp.maximum(m_prev, jnp.max(st, axis=0, keepdims=True))
            alpha = jnp.exp2(m_prev - m_new)
            p = jnp.exp2(st - m_new)
            l_scr[:, cs] = alpha * l_scr[:, cs] + jnp.sum(p, axis=0, keepdims=True)
            pv = lax.dot_general(v, p.astype(BF16), (((0,), (0,)), ((), ())),
                                 preferred_element_type=F32)
            acc_scr[:, cs] = alpha * acc_scr[:, cs] + pv
            m_scr[:, cs] = m_new

    def body(ki, carry):
        step(pl.multiple_of(ki * bk, bk), None)
        return carry

    per_q = blk // bk
    lax.fori_loop(0, qi * per_q, body, 0)
    for d in range(per_q):
        step(pl.multiple_of(qi * blk + d * bk, bk), d * bk)
    o_ref[0] = jnp.transpose(acc_scr[...] / l_scr[...]).astype(o_ref.dtype)


def _attention(q, k, v, *, blk, bk):
    b, s, _ = q.shape
    return pl.pallas_call(
        functools.partial(_attn_kernel, blk=blk, bk=bk, sub=min(bk, HEAD_PAD)),
        grid=(b, N_HEADS, s // blk),
        in_specs=[
            pl.BlockSpec((1, blk, HEAD_PAD), lambda bi, h, qi: (bi, qi, h)),
            pl.BlockSpec((1, s, HEAD_PAD), lambda bi, h, qi: (bi, 0, h)),
            pl.BlockSpec((1, s, D_V), lambda bi, h, qi: (bi, 0, h)),
        ],
        out_specs=pl.BlockSpec((1, blk, D_V), lambda bi, h, qi: (bi, qi, h)),
        out_shape=jax.ShapeDtypeStruct((b, s, N_HEADS * D_V), BF16),
        scratch_shapes=[
            pltpu.VMEM((1, blk), F32),
            pltpu.VMEM((1, blk), F32),
            pltpu.VMEM((D_V, blk), F32),
        ],
        compiler_params=_cparams(("arbitrary", "arbitrary", "arbitrary")),
        name="prompt_attention",
    )(q, k, v)


def _qlat_kernel(q_ref, wukt_ref, qlat_ref, qrope_ref):
    q = q_ref[...]
    qlat_ref[0] = _dot(q[:, :D_NOPE], wukt_ref[0]).astype(BF16)
    qrope_ref[0] = q[:, LANES:]


def _q_latent(q, wukt):
    m = q.shape[0]
    return pl.pallas_call(
        _qlat_kernel,
        grid=(N_HEADS,),
        in_specs=[
            pl.BlockSpec((m, HEAD_PAD), lambda h: (0, h)),
            pl.BlockSpec((1, D_NOPE, KV_LORA), lambda h: (h, 0, 0)),
        ],
        out_specs=[
            pl.BlockSpec((1, m, KV_LORA), lambda h: (h, 0, 0)),
            pl.BlockSpec((1, m, LANES), lambda h: (h, 0, 0)),
        ],
        out_shape=[
            jax.ShapeDtypeStruct((N_HEADS, m, KV_LORA), BF16),
            jax.ShapeDtypeStruct((N_HEADS, m, LANES), BF16),
        ],
        compiler_params=_cparams(("arbitrary",)),
        name="sample_q_latent",
    )(q, wukt)


def _sattn_kernel(qlat_ref, qrope_ref, cckv_ref, ckr_ref, nckv_ref, nkr_ref, o_ref, *, past, ns):
    rows = N_HEADS * ns
    ql = qlat_ref[...].reshape(rows, KV_LORA)
    qr = qrope_ref[...].reshape(rows, LANES)[:, :D_ROPE]
    cc = cckv_ref[0].astype(BF16)
    ck = ckr_ref[0].astype(BF16)
    nc = nckv_ref[...].astype(BF16)
    nk = nkr_ref[...].astype(BF16)
    s1 = _dot_nt(ql, cc) + _dot_nt(qr, ck)
    s2 = _dot_nt(ql, nc) + _dot_nt(qr, nk)
    qt = lax.broadcasted_iota(jnp.int32, s2.shape, 0) % ns
    kt = lax.broadcasted_iota(jnp.int32, s2.shape, 1)
    s2 = jnp.where((past + kt) // CHUNK <= (past + qt) // CHUNK, s2, NEG_BIG)
    m = jnp.maximum(jnp.max(s1, axis=1, keepdims=True), jnp.max(s2, axis=1, keepdims=True))
    p1 = jnp.exp2(s1 - m)
    p2 = jnp.exp2(s2 - m)
    l = jnp.sum(p1, axis=1, keepdims=True) + jnp.sum(p2, axis=1, keepdims=True)
    o = (_dot(p1.astype(BF16), cc) + _dot(p2.astype(BF16), nc)) / l
    o_ref[...] = o.astype(BF16).reshape(N_HEADS, ns, KV_LORA)


def _sample_attention(qlat, qrope, cache_ckv, cache_kr, new_ckv, new_kr, *, layer, ns):
    _, nb, past, _ = cache_ckv.shape
    m = nb * ns
    return pl.pallas_call(
        functools.partial(_sattn_kernel, past=past, ns=ns),
        grid=(nb,),
        in_specs=[
            pl.BlockSpec((N_HEADS, ns, KV_LORA), lambda b: (0, b, 0)),
            pl.BlockSpec((N_HEADS, ns, LANES), lambda b: (0, b, 0)),
            pl.BlockSpec((None, 1, past, KV_LORA), lambda b: (layer, b, 0, 0)),
            pl.BlockSpec((None, 1, past, D_ROPE), lambda b: (layer, b, 0, 0)),
            pl.BlockSpec((ns, KV_LORA), lambda b: (b, 0)),
            pl.BlockSpec((ns, D_ROPE), lambda b: (b, 0)),
        ],
        out_specs=pl.BlockSpec((N_HEADS, ns, KV_LORA), lambda b: (0, b, 0)),
        out_shape=jax.ShapeDtypeStruct((N_HEADS, m, KV_LORA), BF16),
        compiler_params=_cparams(("arbitrary",)),
        name="sample_attention",
    )(qlat, qrope, cache_ckv, cache_kr, new_ckv, new_kr)


def _ouv_kernel(olat_ref, wuv_ref, o_ref):
    o_ref[...] = _dot(olat_ref[0], wuv_ref[...]).astype(BF16)


def _o_from_latent(olat, wuv):
    m = olat.shape[1]
    return pl.pallas_call(
        _ouv_kernel,
        grid=(N_HEADS,),
        in_specs=[
            pl.BlockSpec((1, m, KV_LORA), lambda h: (h, 0, 0)),
            pl.BlockSpec((KV_LORA, D_V), lambda h: (0, h)),
        ],
        out_specs=pl.BlockSpec((m, D_V), lambda h: (0, h)),
        out_shape=jax.ShapeDtypeStruct((m, N_HEADS * D_V), BF16),
        compiler_params=_cparams(("arbitrary",)),
        name="sample_o_from_latent",
    )(olat, wuv)


def _post_kernel(o_ref, w_ref, x_ref, g1_ref, g2_ref, xo_ref, h2_ref):
    xn = x_ref[...] + _rms(_dot(o_ref[...], w_ref[...]), g1_ref[...])
    xo_ref[...] = xn
    h2_ref[...] = _rms(xn, g2_ref[...]).astype(BF16)


def _attn_out(o2d, w_o, x2d, g1, g2, *, bm):
    m, d = x2d.shape
    row = lambda i: (i, 0)
    const = lambda i: (0, 0)
    return pl.pallas_call(
        _post_kernel,
        grid=(m // bm,),
        in_specs=[
            pl.BlockSpec((bm, o2d.shape[1]), row),
            pl.BlockSpec(w_o.shape, const),
            pl.BlockSpec((bm, d), row),
            pl.BlockSpec((1, d), const),
            pl.BlockSpec((1, d), const),
        ],
        out_specs=[pl.BlockSpec((bm, d), row), pl.BlockSpec((bm, d), row)],
        out_shape=[jax.ShapeDtypeStruct((m, d), F32), jax.ShapeDtypeStruct((m, d), BF16)],
        compiler_params=_cparams(("arbitrary",)),
        name="attn_out_post",
    )(o2d, w_o, x2d, g1, g2)


def _pool_kernel(x_ref, halo_ref, g0_ref, w_ref, sc_ref, g1_ref, g2_ref, *rest,
                 bm, blocks_per_seq, pos0, halo_normed, with_resid):
    i = pl.program_id(0)
    x = x_ref[...]
    halo = halo_ref[...]
    if with_resid:
        f_ref, fhalo_ref, g3_ref, xo_ref, h2_ref, tail_ref = rest
        x = x + _rms(f_ref[...], g3_ref[...])
        halo = halo + _rms(fhalo_ref[...], g3_ref[...])
    else:
        xo_ref, h2_ref, tail_ref = rest
    g0 = g0_ref[...]
    h = _rms(x, g0)
    if not halo_normed:
        halo = _rms(halo, g0)
        halo = jnp.where(i % blocks_per_seq == 0, 0.0, halo)
    cat = jnp.concatenate([halo, h], axis=0)
    t = (i % blocks_per_seq) * bm + lax.broadcasted_iota(jnp.int32, (bm, 1), 0)
    posf = (t + pos0).astype(F32)
    gd = x.shape[1] // len(POOL_WINDOWS)
    ys = []
    for g, w in enumerate(POOL_WINDOWS):
        c = cat[:, g * gd:(g + 1) * gd]
        acc = c
        span = 1
        while span < w:
            acc = acc + pltpu.roll(acc, span, axis=0)
            span *= 2
        cnt = jnp.minimum(jnp.float32(w), posf + 1.0)
        mean = acc[POOL_HALO:] / cnt
        dlt = (mean - h[:, g * gd:(g + 1) * gd]).astype(BF16)
        ys.append(_dot(dlt, w_ref[g]))
    y = jnp.concatenate(ys, axis=1) * sc_ref[...]
    xn = x + _rms(y, g1_ref[...])
    xo_ref[...] = xn
    h2_ref[...] = _rms(xn, g2_ref[...]).astype(BF16)
    tail_ref[0] = h[bm - POOL_HALO:]


def _pool_layer(x2d, halo_src, g0, w_pool, scale, g1, g2, *, bm, rows_per_seq, pos0, halo_normed, resid=None):
    m, d = x2d.shape
    bps = rows_per_seq // bm
    nseq = m // rows_per_seq
    row = lambda i: (i, 0)
    const = lambda i: (0, 0)
    if halo_normed:
        halo_map = lambda i: (i, 0)
    else:
        hb = bm // POOL_HALO
        halo_map = lambda i: (jnp.maximum(i * hb - 1, 0), 0)
    in_specs = [
        pl.BlockSpec((bm, d), row),
        pl.BlockSpec((POOL_HALO, d), halo_map),
        pl.BlockSpec((1, d), const),
        pl.BlockSpec(w_pool.shape, lambda i: (0, 0, 0)),
        pl.BlockSpec((1, d), const),
        pl.BlockSpec((1, d), const),
        pl.BlockSpec((1, d), const),
    ]
    args = [x2d, halo_src, g0, w_pool, scale, g1, g2]
    if resid is not None:
        assert not halo_normed
        in_specs += [pl.BlockSpec((bm, d), row), pl.BlockSpec((POOL_HALO, d), halo_map),
                     pl.BlockSpec((1, d), const)]
        args += [resid[0], resid[0], resid[1]]
    return pl.pallas_call(
        functools.partial(_pool_kernel, bm=bm, blocks_per_seq=bps, pos0=pos0, halo_normed=halo_normed,
                          with_resid=resid is not None),
        grid=(m // bm,),
        in_specs=in_specs,
        out_specs=[
            pl.BlockSpec((bm, d), row),
            pl.BlockSpec((bm, d), row),
            pl.BlockSpec((1, POOL_HALO, d), lambda i: (i // bps, 0, 0)),
        ],
        out_shape=[
            jax.ShapeDtypeStruct((m, d), F32),
            jax.ShapeDtypeStruct((m, d), BF16),
            jax.ShapeDtypeStruct((nseq, POOL_HALO, d), F32),
        ],
        compiler_params=_cparams(("arbitrary",)),
        name="pool_layer",
    )(*args)


def _conv3(u, prev1, prev2, cw_ref, cb_ref):
    return cb_ref[...] + cw_ref[0:1, :] * prev2 + cw_ref[1:2, :] * prev1 + cw_ref[2:3, :] * u


def _ffn_tail(j, nf, act, wd_ref, x_ref, g3_ref, xo_ref, acc_scr):
    part = _dot(act.astype(BF16), wd_ref[...])

    @pl.when(j == 0)
    def _():
        acc_scr[...] = part

    @pl.when(j > 0)
    def _():
        acc_scr[...] += part

    @pl.when(j == nf - 1)
    def _():
        xo_ref[...] = x_ref[...] + _rms(acc_scr[...], g3_ref[...])


def _ffn_prompt_kernel(h_ref, wg_ref, wv_ref, cwg_ref, cwv_ref, cbg_ref, cbv_ref, wd_ref,
                       f_ref, tg_ref, tv_ref, cg_scr, cv_scr, act_scr, *, bm, blocks_per_seq, nf):
    t = pl.program_id(0)
    cur = jnp.minimum(t, pl.num_programs(0) - 2)
    i = cur // nf
    j = cur % nf
    prev = jnp.maximum(t - 1, 0)

    @pl.when(t == 0)
    def _():
        act_scr[...] = jnp.zeros(act_scr.shape, BF16)

    @pl.when(prev % nf == 0)
    def _():
        f_ref[...] = jnp.zeros(f_ref.shape, F32)

    f_ref[...] += _dot(act_scr[(t + 1) % 2], wd_ref[...])

    h = h_ref[...]
    first = i % blocks_per_seq == 0

    def branch(w_ref, cw_ref, cb_ref, carry_scr, tail_ref):
        u = _dot(h, w_ref[...])
        prev8 = jnp.where(first, 0.0, carry_scr[j])
        c = _conv3(u, pltpu.roll(u, 1, axis=0), pltpu.roll(u, 2, axis=0), cw_ref, cb_ref)
        head = jnp.concatenate([prev8, u[:SUBLANES]], axis=0)
        c_head = _conv3(head, pltpu.roll(head, 1, axis=0), pltpu.roll(head, 2, axis=0),
                        cw_ref, cb_ref)[SUBLANES:]
        c = jnp.concatenate([c_head, c[SUBLANES:]], axis=0)
        last8 = u[bm - SUBLANES:]
        carry_scr[j] = last8
        tail_ref[0] = last8
        return c

    gate = branch(wg_ref, cwg_ref, cbg_ref, cg_scr, tg_ref)
    val = branch(wv_ref, cwv_ref, cbv_ref, cv_scr, tv_ref)
    act_scr[t % 2] = (gate * jax.nn.sigmoid(gate) * val).astype(BF16)


def _resid_kernel(f_ref, x_ref, g_ref, xo_ref):
    xo_ref[...] = x_ref[...] + _rms(f_ref[...], g_ref[...])


def _residual_norm(f2d, x2d, g, *, bm):
    m, d = x2d.shape
    row = lambda i: (i, 0)
    return pl.pallas_call(
        _resid_kernel,
        grid=(m // bm,),
        in_specs=[pl.BlockSpec((bm, d), row), pl.BlockSpec((bm, d), row), pl.BlockSpec((1, d), lambda i: (0, 0))],
        out_specs=pl.BlockSpec((bm, d), row),
        out_shape=jax.ShapeDtypeStruct((m, d), F32),
        compiler_params=_cparams(("arbitrary",)),
        name="residual_norm",
    )(f2d, x2d, g)


def _ffn_sample_kernel(h_ref, wg_ref, wv_ref, cwg_ref, cwv_ref, cbg_ref, cbv_ref, wd_ref, x_ref, g3_ref,
                       stg_ref, stv_ref, xo_ref, ug_ref, uv_ref, acc_scr, *, ns, nf):
    j = pl.program_id(0)
    h = h_ref[...]
    m = h.shape[0]
    t = lax.broadcasted_iota(jnp.int32, (m, 1), 0) % ns

    def branch(w_ref, cw_ref, cb_ref, st_ref, u_ref):
        u = _dot(h, w_ref[...])
        u_ref[...] = u
        st = st_ref[...]

        def per_row(k):
            row = st[:, k:k + 1, :]
            return jnp.broadcast_to(row, (m // ns, ns, row.shape[2])).reshape(m, row.shape[2])

        older, newer = per_row(0), per_row(1)
        prev1 = jnp.where(t < 1, newer, pltpu.roll(u, 1, axis=0))
        prev2 = jnp.where(t < 1, older, jnp.where(t < 2, newer, pltpu.roll(u, 2, axis=0)))
        return _conv3(u, prev1, prev2, cw_ref, cb_ref)

    gate = branch(wg_ref, cwg_ref, cbg_ref, stg_ref, ug_ref)
    val = branch(wv_ref, cwv_ref, cbv_ref, stv_ref, uv_ref)
    act = gate * jax.nn.sigmoid(gate) * val
    _ffn_tail(j, nf, act, wd_ref, x_ref, g3_ref, xo_ref, acc_scr)


def _ffn_prompt(h2d, w_up, cw, cb, w_down, *, layer, bm, bf, rows_per_seq):
    m, d = h2d.shape
    dff = w_down.shape[1]
    nf = dff // bf
    bps = rows_per_seq // bm
    nseq = m // rows_per_seq
    n = (m // bm) * nf
    cur = lambda t: jnp.minimum(t, n - 1)
    prv = lambda t: jnp.maximum(t - 1, 0)
    f, tg, tv = pl.pallas_call(
        functools.partial(_ffn_prompt_kernel, bm=bm, blocks_per_seq=bps, nf=nf),
        grid=(n + 1,),
        in_specs=[
            pl.BlockSpec((bm, d), lambda t: (cur(t) // nf, 0)),
            pl.BlockSpec((None, d, bf), lambda t: (layer, 0, cur(t) % nf)),
            pl.BlockSpec((None, d, bf), lambda t: (layer, 0, nf + cur(t) % nf)),
            pl.BlockSpec((None, CONV_WIDTH, bf), lambda t: (layer, 0, cur(t) % nf)),
            pl.BlockSpec((None, CONV_WIDTH, bf), lambda t: (layer, 0, nf + cur(t) % nf)),
            pl.BlockSpec((None, 1, bf), lambda t: (layer, 0, cur(t) % nf)),
            pl.BlockSpec((None, 1, bf), lambda t: (layer, 0, nf + cur(t) % nf)),
            pl.BlockSpec((None, bf, d), lambda t: (layer, prv(t) % nf, 0)),
        ],
        out_specs=[
            pl.BlockSpec((bm, d), lambda t: (prv(t) // nf, 0)),
            pl.BlockSpec((1, SUBLANES, bf), lambda t: (cur(t) // nf, 0, cur(t) % nf)),
            pl.BlockSpec((1, SUBLANES, bf), lambda t: (cur(t) // nf, 0, cur(t) % nf)),
        ],
        out_shape=[
            jax.ShapeDtypeStruct((m, d), F32),
            jax.ShapeDtypeStruct((m // bm, SUBLANES, dff), F32),
            jax.ShapeDtypeStruct((m // bm, SUBLANES, dff), F32),
        ],
        scratch_shapes=[
            pltpu.VMEM((nf, SUBLANES, bf), F32),
            pltpu.VMEM((nf, SUBLANES, bf), F32),
            pltpu.VMEM((2, bm, bf), BF16),
        ],
        compiler_params=_cparams(("arbitrary",)),
        name="ffn_prompt",
    )(h2d, w_up, w_up, cw, cw, cb, cb, w_down)
    k = CONV_WIDTH - 1
    tg = tg.reshape(nseq, bps, SUBLANES, dff)[:, bps - 1, SUBLANES - k:]
    tv = tv.reshape(nseq, bps, SUBLANES, dff)[:, bps - 1, SUBLANES - k:]
    return f, jnp.concatenate([tg, tv], axis=-1)


def _ffn_sample(h2d, w_up, cw, cb, w_down, x2d, g3, state, *, layer, bf, ns):
    m, d = x2d.shape
    dff = w_down.shape[1]
    nf = dff // bf
    nb = m // ns
    k = CONV_WIDTH - 1
    assert state.shape[1:] == (nb, k, 2 * dff) and k == 2
    full = lambda j: (0, 0)
    colg = lambda j: (0, j)
    colv = lambda j: (0, nf + j)
    xo, ug, uv = pl.pallas_call(
        functools.partial(_ffn_sample_kernel, ns=ns, nf=nf),
        grid=(nf,),
        in_specs=[
            pl.BlockSpec((m, d), full),
            pl.BlockSpec((None, d, bf), lambda j: (layer, 0, j)),
            pl.BlockSpec((None, d, bf), lambda j: (layer, 0, nf + j)),
            pl.BlockSpec((None, CONV_WIDTH, bf), lambda j: (layer, 0, j)),
            pl.BlockSpec((None, CONV_WIDTH, bf), lambda j: (layer, 0, nf + j)),
            pl.BlockSpec((None, 1, bf), lambda j: (layer, 0, j)),
            pl.BlockSpec((None, 1, bf), lambda j: (layer, 0, nf + j)),
            pl.BlockSpec((None, bf, d), lambda j: (layer, j, 0)),
            pl.BlockSpec((m, d), full),
            pl.BlockSpec((1, d), full),
            pl.BlockSpec((None, nb, k, bf), lambda j: (layer, 0, 0, j)),
            pl.BlockSpec((None, nb, k, bf), lambda j: (layer, 0, 0, nf + j)),
        ],
        out_specs=[
            pl.BlockSpec((m, d), full),
            pl.BlockSpec((m, bf), colg),
            pl.BlockSpec((m, bf), colg),
        ],
        out_shape=[
            jax.ShapeDtypeStruct((m, d), F32),
            jax.ShapeDtypeStruct((m, dff), F32),
            jax.ShapeDtypeStruct((m, dff), F32),
        ],
        scratch_shapes=[pltpu.VMEM((m, d), F32)],
        compiler_params=_cparams(("arbitrary",)),
        name="ffn_sample",
    )(h2d, w_up, w_up, cw, cw, cb, cb, w_down, x2d, g3, state, state)
    u = jnp.concatenate([ug, uv], axis=-1).reshape(nb, ns, 2 * dff)
    return xo, u[:, ns - k:]


def _rope_tables(pos, reps):
    half = D_ROPE // 2
    inv = ROPE_BASE ** (-jnp.arange(half, dtype=F32) / half)
    ang = pos[:, None] * inv[None, :]
    c, s = jnp.cos(ang), jnp.sin(ang)
    z = jnp.zeros((pos.shape[0], LANES - D_ROPE), F32)
    cos_t = jnp.concatenate([c, c, z], axis=1)
    sin_t = jnp.concatenate([s, s, z], axis=1)
    return jnp.tile(cos_t, (reps, 1)), jnp.tile(sin_t, (reps, 1))


def _signed_partner(w):
    half = D_ROPE // 2
    return jnp.concatenate([-w[..., half:], w[..., :half]], axis=-1)


def _stage_mla_weights(w_dq, w_uq, w_dkv, w_uk, w_uv, w_o):
    w_kr = w_dkv[:, KV_LORA:]
    w1 = jnp.concatenate([w_dq, w_dkv[:, :KV_LORA], w_kr, _signed_partner(w_kr)], axis=1).astype(BF16)
    wq = w_uq.reshape(Q_LORA, N_HEADS, D_NOPE + D_ROPE)
    wq_rope = wq[..., D_NOPE:]
    wuq = jnp.concatenate([wq[..., :D_NOPE], wq_rope, _signed_partner(wq_rope)], axis=-1)
    wuq = wuq.reshape(Q_LORA, N_HEADS * HEAD_PAD).astype(BF16)
    wuk = w_uk.reshape(KV_LORA, N_HEADS * D_NOPE).astype(BF16)
    wuv = w_uv.reshape(KV_LORA, N_HEADS * D_V).astype(BF16)
    wukt = jnp.transpose(w_uk, (1, 2, 0)).astype(BF16)
    return w1, wuq, wuk, wuv, wukt, w_o.astype(BF16)


def _block_rows(m, target):
    bm = min(m, target)
    while m % bm:
        bm //= 2
    return bm


def _block_cols(n, target):
    return max(c for c in range(LANES, min(n, target) + 1, LANES) if n % c == 0)


def kernel(x_prompt, x_sample, cache_ckv, cache_krope, state_pool, state_conv, norm_g, mla_w_dq, mla_q_norm, mla_w_uq, mla_w_dkv, mla_kv_norm, mla_w_uk, mla_w_uv, mla_w_o, pool_w, pool_scale, ffn_w_up, ffn_conv_w, ffn_conv_b, ffn_w_down):
    nbp, sp, d = x_prompt.shape
    nbs, ss, _ = x_sample.shape
    past = cache_ckv.shape[2]
    depth = norm_g.shape[0]
    mp, ms = nbp * sp, nbs * ss
    xp = x_prompt.reshape(mp, d)
    xs = x_sample.reshape(ms, d)

    bm_proj = _block_rows(sp, 256)
    bm_post = _block_rows(sp, 512)
    dff = ffn_w_down.shape[1]
    bm_ffn = _block_rows(sp, 512)
    bf_ffn = _block_cols(dff, 1408)
    bm_pool = _block_rows(sp, 512)
    blk_attn = _block_rows(sp, 4096)
    bk_attn = _block_rows(blk_attn, 512)
    bf = _block_cols(dff, 512)
    bms = _block_rows(ms, 256)

    cos_p, sin_p = _rope_tables(jnp.arange(sp, dtype=jnp.int32).astype(F32), 1)
    cos_s, sin_s = _rope_tables((past + jnp.arange(ss, dtype=jnp.int32)).astype(F32), max(bms // ss, 1))

    w_up = ffn_w_up.astype(BF16)
    w_down = ffn_w_down.astype(BF16)
    cb = ffn_conv_b.reshape(depth, 1, -1)

    outs = {k: [] for k in ("ckv_p", "kr_p", "pool_p", "conv_p", "ckv_s", "kr_s", "pool_s", "conv_s")}
    pend = None
    for i in range(depth):
        g = norm_g[i].reshape(4, 1, d)
        j = i // 2
        if i % 2 == 0:
            w1, wuq, wuk, wuv, wukt, wo = _stage_mla_weights(
                mla_w_dq[j], mla_w_uq[j], mla_w_dkv[j], mla_w_uk[j], mla_w_uv[j], mla_w_o[j])
            qn = mla_q_norm[j].reshape(1, Q_LORA)
            kvn = mla_kv_norm[j].reshape(1, KV_LORA)
            proj = _mla_project(xp, g[0], w1, qn, kvn, wuq, cos_p, sin_p, wuk, wuv, bm=bm_proj, resid=pend)
            q, ckv, kr, k, v = proj[:5]
            if pend is not None:
                xp = proj[5]
            o = _attention(q.reshape(nbp, sp, -1), k.reshape(nbp, sp, -1), v.reshape(nbp, sp, -1),
                           blk=blk_attn, bk=bk_attn)
            xp, hp = _attn_out(o.reshape(mp, -1), wo, xp, g[1], g[2], bm=bm_post)
            outs["ckv_p"].append(ckv.reshape(nbp, sp, KV_LORA))
            outs["kr_p"].append(kr.reshape(nbp, sp, D_ROPE))
            qs, ckv_s, kr_s = _mla_project(xs, g[0], w1, qn, kvn, wuq, cos_s, sin_s, bm=bms)
            qlat, qrope = _q_latent(qs, wukt)
            olat = _sample_attention(qlat, qrope, cache_ckv, cache_krope, ckv_s, kr_s, layer=j, ns=ss)
            os_ = _o_from_latent(olat, wuv)
            xs, hs = _attn_out(os_, wo, xs, g[1], g[2], bm=bms)
            outs["ckv_s"].append(ckv_s.reshape(nbs, ss, KV_LORA))
            outs["kr_s"].append(kr_s.reshape(nbs, ss, D_ROPE))
        else:
            wp = pool_w[j].astype(BF16)
            sc = pool_scale[j].reshape(1, d)
            xp, hp, tail_p = _pool_layer(xp, xp, g[0], wp, sc, g[1], g[2], bm=bm_pool,
                                         rows_per_seq=sp, pos0=0, halo_normed=False, resid=pend)
            outs["pool_p"].append(tail_p[:, POOL_HALO - POOL_HIST:])
            hist = jnp.pad(state_pool[j], ((0, 0), (POOL_HALO - POOL_HIST, 0), (0, 0))).reshape(-1, d)
            xs, hs, tail_s = _pool_layer(xs, hist, g[0], wp, sc, g[1], g[2], bm=ss,
                                         rows_per_seq=ss, pos0=past, halo_normed=True)
            outs["pool_s"].append(tail_s[:, POOL_HALO - POOL_HIST:])
        fp, cv_p = _ffn_prompt(hp, w_up, ffn_conv_w, cb, w_down, layer=i, bm=bm_ffn, bf=bf_ffn,
                               rows_per_seq=sp)
        pend = (fp, g[3])
        xs, cv_s = _ffn_sample(hs, w_up, ffn_conv_w, cb, w_down, xs, g[3], state_conv, layer=i, bf=bf, ns=ss)
        outs["conv_p"].append(cv_p)
        outs["conv_s"].append(cv_s)
    xp = _residual_norm(pend[0], xp, pend[1], bm=bm_post)
    st = lambda k: jnp.stack(outs[k])
    return (xp.reshape(nbp, sp, d), xs.reshape(nbs, ss, d),
            st("ckv_p"), st("kr_p"), st("pool_p"), st("conv_p"),
            st("ckv_s"), st("kr_s"), st("pool_s"), st("conv_s"))
```

```python
import functools

import numpy as np
import jax
import jax.numpy as jnp
from jax import lax
from jax.experimental import pallas as pl
from jax.experimental.pallas import tpu as pltpu

F32 = jnp.float32
BF16 = jnp.bfloat16

CHUNK = 64
N_HEADS = 16
Q_LORA = 512
KV_LORA = 512
D_NOPE = 128
D_ROPE = 64
D_V = 128
ROPE_BASE = 10000.0
ATTN_SCALE = (D_NOPE + D_ROPE) ** -0.5
Q_SCALE = ATTN_SCALE * float(np.log2(np.e))
POOL_WINDOWS = (2, 4, 8, 16)
POOL_HIST = max(POOL_WINDOWS) - 1
CONV_WIDTH = 3
EPS = 1e-6

LANES = 128
SUBLANES = 8
HEAD_PAD = 2 * LANES
POOL_HALO = 16
VMEM_LIMIT = 56 * 1024 * 1024
NEG_BIG = -1e30


def _cparams(sem):
    return pltpu.CompilerParams(dimension_semantics=sem, vmem_limit_bytes=VMEM_LIMIT)


def _rms(xf, g):
    ms = jnp.mean(xf * xf, axis=-1, keepdims=True)
    return xf * lax.rsqrt(ms + EPS) * g


def _dot(a, b):
    return jnp.dot(a, b, preferred_element_type=F32)


def _dot_nt(a, b):
    return lax.dot_general(a, b, (((1,), (1,)), ((), ())), preferred_element_type=F32)


def _rope_cols(r, cos_t, sin_t):
    return r * cos_t + pltpu.roll(r, D_ROPE, axis=1) * sin_t


def _proj_kernel(x_ref, g_ref, w1_ref, qn_ref, kvn_ref, wuq_ref, cos_ref, sin_ref, *rest, with_kv, with_resid):
    rest = list(rest)
    if with_kv:
        wuk_ref, wuv_ref = rest[:2]
        rest = rest[2:]
    x = x_ref[...]
    if with_resid:
        f_ref, g3_ref = rest[:2]
        xo_ref = rest[-1]
        rest = rest[2:-1]
        x = x + _rms(f_ref[...], g3_ref[...])
        xo_ref[...] = x
    if with_kv:
        q_ref, ckv_ref, kr_ref, k_ref, v_ref = rest
    else:
        q_ref, ckv_ref, kr_ref = rest
    h = _rms(x, g_ref[...]).astype(BF16)
    y = _dot(h, w1_ref[...])
    cq = _rms(y[:, :Q_LORA], qn_ref[...]).astype(BF16)
    ckv = _rms(y[:, Q_LORA:Q_LORA + KV_LORA], kvn_ref[...])
    ckv_ref[...] = ckv
    cos_t = cos_ref[...]
    sin_t = sin_ref[...]
    kr = _rope_cols(y[:, Q_LORA + KV_LORA:], cos_t, sin_t)
    kr_ref[...] = kr[:, :D_ROPE]
    q = _dot(cq, wuq_ref[...]) * Q_SCALE
    for hd in range(N_HEADS):
        b0 = hd * HEAD_PAD
        q_ref[:, b0:b0 + LANES] = q[:, b0:b0 + LANES].astype(BF16)
        q_ref[:, b0 + LANES:b0 + HEAD_PAD] = _rope_cols(
            q[:, b0 + LANES:b0 + HEAD_PAD], cos_t, sin_t).astype(BF16)
    if with_kv:
        ckv_b = ckv.astype(BF16)
        kr_b = kr.astype(BF16)
        kn = _dot(ckv_b, wuk_ref[...])
        for hd in range(N_HEADS):
            b0 = hd * HEAD_PAD
            k_ref[:, b0:b0 + LANES] = kn[:, hd * D_NOPE:(hd + 1) * D_NOPE].astype(BF16)
            k_ref[:, b0 + LANES:b0 + HEAD_PAD] = kr_b
        v_ref[...] = _dot(ckv_b, wuv_ref[...]).astype(BF16)


def _mla_project(x2d, g, w1, qn, kvn, wuq, cos_t, sin_t, wuk=None, wuv=None, *, bm, resid=None):
    m, d = x2d.shape
    with_kv = wuk is not None
    nt = cos_t.shape[0] // bm
    row = lambda i: (i, 0)
    const = lambda i: (0, 0)
    in_specs = [
        pl.BlockSpec((bm, d), row),
        pl.BlockSpec((1, d), const),
        pl.BlockSpec(w1.shape, const),
        pl.BlockSpec((1, Q_LORA), const),
        pl.BlockSpec((1, KV_LORA), const),
        pl.BlockSpec(wuq.shape, const),
        pl.BlockSpec((bm, LANES), lambda i: (i % nt, 0)),
        pl.BlockSpec((bm, LANES), lambda i: (i % nt, 0)),
    ]
    args = [x2d, g, w1, qn, kvn, wuq, cos_t, sin_t]
    out_shape = [
        jax.ShapeDtypeStruct((m, N_HEADS * HEAD_PAD), BF16),
        jax.ShapeDtypeStruct((m, KV_LORA), F32),
        jax.ShapeDtypeStruct((m, D_ROPE), F32),
    ]
    out_specs = [
        pl.BlockSpec((bm, N_HEADS * HEAD_PAD), row),
        pl.BlockSpec((bm, KV_LORA), row),
        pl.BlockSpec((bm, D_ROPE), row),
    ]
    if with_kv:
        in_specs += [pl.BlockSpec(wuk.shape, const), pl.BlockSpec(wuv.shape, const)]
        args += [wuk, wuv]
        out_shape += [
            jax.ShapeDtypeStruct((m, N_HEADS * HEAD_PAD), BF16),
            jax.ShapeDtypeStruct((m, N_HEADS * D_V), BF16),
        ]
        out_specs += [
            pl.BlockSpec((bm, N_HEADS * HEAD_PAD), row),
            pl.BlockSpec((bm, N_HEADS * D_V), row),
        ]
    if resid is not None:
        in_specs += [pl.BlockSpec((bm, d), row), pl.BlockSpec((1, d), const)]
        args += list(resid)
        out_shape.append(jax.ShapeDtypeStruct((m, d), F32))
        out_specs.append(pl.BlockSpec((bm, d), row))
    return pl.pallas_call(
        functools.partial(_proj_kernel, with_kv=with_kv, with_resid=resid is not None),
        grid=(m // bm,),
        in_specs=in_specs,
        out_specs=out_specs,
        out_shape=out_shape,
        compiler_params=_cparams(("arbitrary",)),
        name="mla_project",
    )(*args)


def _attn_kernel(q_ref, k_ref, v_ref, o_ref, m_scr, l_scr, acc_scr, *, blk, bk, sub):
    qi = pl.program_id(2)
    m_scr[...] = jnp.full(m_scr.shape, NEG_BIG, F32)
    l_scr[...] = jnp.zeros(l_scr.shape, F32)
    acc_scr[...] = jnp.zeros(acc_scr.shape, F32)

    def step(start, diag):
        scores = []
        for c in range(blk // sub):
            nk = bk if diag is None else min(bk, (c + 1) * sub - diag)
            if nk <= 0:
                continue
            k = k_ref[0, pl.ds(start, nk), :]
            st = _dot_nt(k, q_ref[0, c * sub:(c + 1) * sub, :])
            if diag is not None and diag + nk > c * sub + CHUNK:
                kc = (lax.broadcasted_iota(jnp.int32, st.shape, 0) + diag) // CHUNK
                qc = (lax.broadcasted_iota(jnp.int32, st.shape, 1) + c * sub) // CHUNK
                st = jnp.where(kc <= qc, st, NEG_BIG)
            scores.append((c, nk, st))
        for c, nk, st in scores:
            cs = slice(c * sub, (c + 1) * sub)
            v = v_ref[0, pl.ds(start, nk), :]
            m_prev = m_scr[:, cs]
            m_new = jnp.maximum(m_prev, jnp.max(st, axis=0, keepdims=True))
            alpha = jnp.exp2(m_prev - m_new)
            p = jnp.exp2(st - m_new)
            l_scr[:, cs] = alpha * l_scr[:, cs] + jnp.sum(p, axis=0, keepdims=True)
            pv = lax.dot_general(v, p.astype(BF16), (((0,), (0,)), ((), ())),
                                 preferred_element_type=F32)
            acc_scr[:, cs] = alpha * acc_scr[:, cs] + pv
            m_scr[:, cs] = m_new

    def body(ki, carry):
        step(pl.multiple_of(ki * bk, bk), None)
        return carry

    per_q = blk // bk
    lax.fori_loop(0, qi * per_q, body, 0)
    for d in range(per_q):
        step(pl.multiple_of(qi * blk + d * bk, bk), d * bk)
    o_ref[0] = jnp.transpose(acc_scr[...] / l_scr[...]).astype(o_ref.dtype)


def _attention(q, k, v, *, blk, bk):
    b, s, _ = q.shape
    return pl.pallas_call(
        functools.partial(_attn_kernel, blk=blk, bk=bk, sub=min(bk, HEAD_PAD)),
        grid=(b, N_HEADS, s // blk),
        in_specs=[
            pl.BlockSpec((1, blk, HEAD_PAD), lambda bi, h, qi: (bi, qi, h)),
            pl.BlockSpec((1, s, HEAD_PAD), lambda bi, h, qi: (bi, 0, h)),
            pl.BlockSpec((1, s, D_V), lambda bi, h, qi: (bi, 0, h)),
        ],
        out_specs=pl.BlockSpec((1, blk, D_V), lambda bi, h, qi: (bi, qi, h)),
        out_shape=jax.ShapeDtypeStruct((b, s, N_HEADS * D_V), BF16),
        scratch_shapes=[
            pltpu.VMEM((1, blk), F32),
            pltpu.VMEM((1, blk), F32),
            pltpu.VMEM((D_V, blk), F32),
        ],
        compiler_params=_cparams(("arbitrary", "arbitrary", "arbitrary")),
        name="prompt_attention",
    )(q, k, v)


def _qlat_kernel(q_ref, wukt_ref, qlat_ref, qrope_ref):
    q = q_ref[...]
    qlat_ref[0] = _dot(q[:, :D_NOPE], wukt_ref[0]).astype(BF16)
    qrope_ref[0] = q[:, LANES:]


def _q_latent(q, wukt):
    m = q.shape[0]
    return pl.pallas_call(
        _qlat_kernel,
        grid=(N_HEADS,),
        in_specs=[
            pl.BlockSpec((m, HEAD_PAD), lambda h: (0, h)),
            pl.BlockSpec((1, D_NOPE, KV_LORA), lambda h: (h, 0, 0)),
        ],
        out_specs=[
            pl.BlockSpec((1, m, KV_LORA), lambda h: (h, 0, 0)),
            pl.BlockSpec((1, m, LANES), lambda h: (h, 0, 0)),
        ],
        out_shape=[
            jax.ShapeDtypeStruct((N_HEADS, m, KV_LORA), BF16),
            jax.ShapeDtypeStruct((N_HEADS, m, LANES), BF16),
        ],
        compiler_params=_cparams(("arbitrary",)),
        name="sample_q_latent",
    )(q, wukt)


def _sattn_kernel(qlat_ref, qrope_ref, cckv_ref, ckr_ref, nckv_ref, nkr_ref, o_ref, *, past, ns):
    rows = N_HEADS * ns
    ql = qlat_ref[...].reshape(rows, KV_LORA)
    qr = qrope_ref[...].reshape(rows, LANES)[:, :D_ROPE]
    cc = cckv_ref[0].astype(BF16)
    ck = ckr_ref[0].astype(BF16)
    nc = nckv_ref[...].astype(BF16)
    nk = nkr_ref[...].astype(BF16)
    s1 = _dot_nt(ql, cc) + _dot_nt(qr, ck)
    s2 = _dot_nt(ql, nc) + _dot_nt(qr, nk)
    qt = lax.broadcasted_iota(jnp.int32, s2.shape, 0) % ns
    kt = lax.broadcasted_iota(jnp.int32, s2.shape, 1)
    s2 = jnp.where((past + kt) // CHUNK <= (past + qt) // CHUNK, s2, NEG_BIG)
    m = jnp.maximum(jnp.max(s1, axis=1, keepdims=True), jnp.max(s2, axis=1, keepdims=True))
    p1 = jnp.exp2(s1 - m)
    p2 = jnp.exp2(s2 - m)
    l = jnp.sum(p1, axis=1, keepdims=True) + jnp.sum(p2, axis=1, keepdims=True)
    o = (_dot(p1.astype(BF16), cc) + _dot(p2.astype(BF16), nc)) / l
    o_ref[...] = o.astype(BF16).reshape(N_HEADS, ns, KV_LORA)


def _sample_attention(qlat, qrope, cache_ckv, cache_kr, new_ckv, new_kr, *, layer, ns):
    _, nb, past, _ = cache_ckv.shape
    m = nb * ns
    return pl.pallas_call(
        functools.partial(_sattn_kernel, past=past, ns=ns),
        grid=(nb,),
        in_specs=[
            pl.BlockSpec((N_HEADS, ns, KV_LORA), lambda b: (0, b, 0)),
            pl.BlockSpec((N_HEADS, ns, LANES), lambda b: (0, b, 0)),
            pl.BlockSpec((None, 1, past, KV_LORA), lambda b: (layer, b, 0, 0)),
            pl.BlockSpec((None, 1, past, D_ROPE), lambda b: (layer, b, 0, 0)),
            pl.BlockSpec((ns, KV_LORA), lambda b: (b, 0)),
            pl.BlockSpec((ns, D_ROPE), lambda b: (b, 0)),
        ],
        out_specs=pl.BlockSpec((N_HEADS, ns, KV_LORA), lambda b: (0, b, 0)),
        out_shape=jax.ShapeDtypeStruct((N_HEADS, m, KV_LORA), BF16),
        compiler_params=_cparams(("arbitrary",)),
        name="sample_attention",
    )(qlat, qrope, cache_ckv, cache_kr, new_ckv, new_kr)


def _ouv_kernel(olat_ref, wuv_ref, o_ref):
    o_ref[...] = _dot(olat_ref[0], wuv_ref[...]).astype(BF16)


def _o_from_latent(olat, wuv):
    m = olat.shape[1]
    return pl.pallas_call(
        _ouv_kernel,
        grid=(N_HEADS,),
        in_specs=[
            pl.BlockSpec((1, m, KV_LORA), lambda h: (h, 0, 0)),
            pl.BlockSpec((KV_LORA, D_V), lambda h: (0, h)),
        ],
        out_specs=pl.BlockSpec((m, D_V), lambda h: (0, h)),
        out_shape=jax.ShapeDtypeStruct((m, N_HEADS * D_V), BF16),
        compiler_params=_cparams(("arbitrary",)),
        name="sample_o_from_latent",
    )(olat, wuv)


def _post_kernel(o_ref, w_ref, x_ref, g1_ref, g2_ref, xo_ref, h2_ref):
    xn = x_ref[...] + _rms(_dot(o_ref[...], w_ref[...]), g1_ref[...])
    xo_ref[...] = xn
    h2_ref[...] = _rms(xn, g2_ref[...]).astype(BF16)


def _attn_out(o2d, w_o, x2d, g1, g2, *, bm):
    m, d = x2d.shape
    row = lambda i: (i, 0)
    const = lambda i: (0, 0)
    return pl.pallas_call(
        _post_kernel,
        grid=(m // bm,),
        in_specs=[
            pl.BlockSpec((bm, o2d.shape[1]), row),
            pl.BlockSpec(w_o.shape, const),
            pl.BlockSpec((bm, d), row),
            pl.BlockSpec((1, d), const),
            pl.BlockSpec((1, d), const),
        ],
        out_specs=[pl.BlockSpec((bm, d), row), pl.BlockSpec((bm, d), row)],
        out_shape=[jax.ShapeDtypeStruct((m, d), F32), jax.ShapeDtypeStruct((m, d), BF16)],
        compiler_params=_cparams(("arbitrary",)),
        name="attn_out_post",
    )(o2d, w_o, x2d, g1, g2)


def _pool_kernel(x_ref, halo_ref, g0_ref, w_ref, sc_ref, g1_ref, g2_ref, *rest,
                 bm, blocks_per_seq, pos0, halo_normed, with_resid):
    i = pl.program_id(0)
    x = x_ref[...]
    halo = halo_ref[...]
    if with_resid:
        f_ref, fhalo_ref, g3_ref, xo_ref, h2_ref, tail_ref = rest
        x = x + _rms(f_ref[...], g3_ref[...])
        halo = halo + _rms(fhalo_ref[...], g3_ref[...])
    else:
        xo_ref, h2_ref, tail_ref = rest
    g0 = g0_ref[...]
    h = _rms(x, g0)
    if not halo_normed:
        halo = _rms(halo, g0)
        halo = jnp.where(i % blocks_per_seq == 0, 0.0, halo)
    cat = jnp.concatenate([halo, h], axis=0)
    t = (i % blocks_per_seq) * bm + lax.broadcasted_iota(jnp.int32, (bm, 1), 0)
    posf = (t + pos0).astype(F32)
    gd = x.shape[1] // len(POOL_WINDOWS)
    ys = []
    for g, w in enumerate(POOL_WINDOWS):
        c = cat[:, g * gd:(g + 1) * gd]
        acc = c
        span = 1
        while span < w:
            acc = acc + pltpu.roll(acc, span, axis=0)
            span *= 2
        cnt = jnp.minimum(jnp.float32(w), posf + 1.0)
        mean = acc[POOL_HALO:] / cnt
        dlt = (mean - h[:, g * gd:(g + 1) * gd]).astype(BF16)
        ys.append(_dot(dlt, w_ref[g]))
    y = jnp.concatenate(ys, axis=1) * sc_ref[...]
    xn = x + _rms(y, g1_ref[...])
    xo_ref[...] = xn
    h2_ref[...] = _rms(xn, g2_ref[...]).astype(BF16)
    tail_ref[0] = h[bm - POOL_HALO:]


def _pool_layer(x2d, halo_src, g0, w_pool, scale, g1, g2, *, bm, rows_per_seq, pos0, halo_normed, resid=None):
    m, d = x2d.shape
    bps = rows_per_seq // bm
    nseq = m // rows_per_seq
    row = lambda i: (i, 0)
    const = lambda i: (0, 0)
    if halo_normed:
        halo_map = lambda i: (i, 0)
    else:
        hb = bm // POOL_HALO
        halo_map = lambda i: (jnp.maximum(i * hb - 1, 0), 0)
    in_specs = [
        pl.BlockSpec((bm, d), row),
        pl.BlockSpec((POOL_HALO, d), halo_map),
        pl.BlockSpec((1, d), const),
        pl.BlockSpec(w_pool.shape, lambda i: (0, 0, 0)),
        pl.BlockSpec((1, d), const),
        pl.BlockSpec((1, d), const),
        pl.BlockSpec((1, d), const),
    ]
    args = [x2d, halo_src, g0, w_pool, scale, g1, g2]
    if resid is not None:
        assert not halo_normed
        in_specs += [pl.BlockSpec((bm, d), row), pl.BlockSpec((POOL_HALO, d), halo_map),
                     pl.BlockSpec((1, d), const)]
        args += [resid[0], resid[0], resid[1]]
    return pl.pallas_call(
        functools.partial(_pool_kernel, bm=bm, blocks_per_seq=bps, pos0=pos0, halo_normed=halo_normed,
                          with_resid=resid is not None),
        grid=(m // bm,),
        in_specs=in_specs,
        out_specs=[
            pl.BlockSpec((bm, d), row),
            pl.BlockSpec((bm, d), row),
            pl.BlockSpec((1, POOL_HALO, d), lambda i: (i // bps, 0, 0)),
        ],
        out_shape=[
            jax.ShapeDtypeStruct((m, d), F32),
            jax.ShapeDtypeStruct((m, d), BF16),
            jax.ShapeDtypeStruct((nseq, POOL_HALO, d), F32),
        ],
        compiler_params=_cparams(("arbitrary",)),
        name="pool_layer",
    )(*args)


def _conv3(u, prev1, prev2, cw_ref, cb_ref):
    return cb_ref[...] + cw_ref[0:1, :] * prev2 + cw_ref[1:2, :] * prev1 + cw_ref[2:3, :] * u


def _ffn_tail(j, nf, act, wd_ref, x_ref, g3_ref, xo_ref, acc_scr):
    part = _dot(act.astype(BF16), wd_ref[...])

    @pl.when(j == 0)
    def _():
        acc_scr[...] = part

    @pl.when(j > 0)
    def _():
        acc_scr[...] += part

    @pl.when(j == nf - 1)
    def _():
        xo_ref[...] = x_ref[...] + _rms(acc_scr[...], g3_ref[...])


def _ffn_prompt_kernel(h_ref, wg_ref, wv_ref, cwg_ref, cwv_ref, cbg_ref, cbv_ref, wd_ref,
                       f_ref, tg_ref, tv_ref, cg_scr, cv_scr, act_scr, *, bm, blocks_per_seq, nf):
    t = pl.program_id(0)
    cur = jnp.minimum(t, pl.num_programs(0) - 2)
    i = cur // nf
    j = cur % nf
    prev = jnp.maximum(t - 1, 0)

    @pl.when(t == 0)
    def _():
        act_scr[...] = jnp.zeros(act_scr.shape, BF16)

    @pl.when(prev % nf == 0)
    def _():
        f_ref[...] = jnp.zeros(f_ref.shape, F32)

    f_ref[...] += _dot(act_scr[(t + 1) % 2], wd_ref[...])

    h = h_ref[...]
    first = i % blocks_per_seq == 0

    def branch(w_ref, cw_ref, cb_ref, carry_scr, tail_ref):
        u = _dot(h, w_ref[...])
        prev8 = jnp.where(first, 0.0, carry_scr[j])
        c = _conv3(u, pltpu.roll(u, 1, axis=0), pltpu.roll(u, 2, axis=0), cw_ref, cb_ref)
        head = jnp.concatenate([prev8, u[:SUBLANES]], axis=0)
        c_head = _conv3(head, pltpu.roll(head, 1, axis=0), pltpu.roll(head, 2, axis=0),
                        cw_ref, cb_ref)[SUBLANES:]
        c = jnp.concatenate([c_head, c[SUBLANES:]], axis=0)
        last8 = u[bm - SUBLANES:]
        carry_scr[j] = last8
        tail_ref[0] = last8
        return c

    gate = branch(wg_ref, cwg_ref, cbg_ref, cg_scr, tg_ref)
    val = branch(wv_ref, cwv_ref, cbv_ref, cv_scr, tv_ref)
    act_scr[t % 2] = (gate * jax.nn.sigmoid(gate) * val).astype(BF16)


def _resid_kernel(f_ref, x_ref, g_ref, xo_ref):
    xo_ref[...] = x_ref[...] + _rms(f_ref[...], g_ref[...])


def _residual_norm(f2d, x2d, g, *, bm):
    m, d = x2d.shape
    row = lambda i: (i, 0)
    return pl.pallas_call(
        _resid_kernel,
        grid=(m // bm,),
        in_specs=[pl.BlockSpec((bm, d), row), pl.BlockSpec((bm, d), row), pl.BlockSpec((1, d), lambda i: (0, 0))],
        out_specs=pl.BlockSpec((bm, d), row),
        out_shape=jax.ShapeDtypeStruct((m, d), F32),
        compiler_params=_cparams(("arbitrary",)),
        name="residual_norm",
    )(f2d, x2d, g)


def _ffn_sample_kernel(h_ref, wg_ref, wv_ref, cwg_ref, cwv_ref, cbg_ref, cbv_ref, wd_ref, x_ref, g3_ref,
                       stg_ref, stv_ref, xo_ref, ug_ref, uv_ref, acc_scr, *, ns, nf):
    j = pl.program_id(0)
    h = h_ref[...]
    m = h.shape[0]
    t = lax.broadcasted_iota(jnp.int32, (m, 1), 0) % ns

    def branch(w_ref, cw_ref, cb_ref, st_ref, u_ref):
        u = _dot(h, w_ref[...])
        u_ref[...] = u
        st = st_ref[...]

        def per_row(k):
            row = st[:, k:k + 1, :]
            return jnp.broadcast_to(row, (m // ns, ns, row.shape[2])).reshape(m, row.shape[2])

        older, newer = per_row(0), per_row(1)
        prev1 = jnp.where(t < 1, newer, pltpu.roll(u, 1, axis=0))
        prev2 = jnp.where(t < 1, older, jnp.where(t < 2, newer, pltpu.roll(u, 2, axis=0)))
        return _conv3(u, prev1, prev2, cw_ref, cb_ref)

    gate = branch(wg_ref, cwg_ref, cbg_ref, stg_ref, ug_ref)
    val = branch(wv_ref, cwv_ref, cbv_ref, stv_ref, uv_ref)
    act = gate * jax.nn.sigmoid(gate) * val
    _ffn_tail(j, nf, act, wd_ref, x_ref, g3_ref, xo_ref, acc_scr)


def _ffn_prompt(h2d, w_up, cw, cb, w_down, *, layer, bm, rows_per_seq):
    m, d = h2d.shape
    dff = w_down.shape[1]
    bf = w_up.shape[3]
    nf = dff // bf
    bps = rows_per_seq // bm
    nseq = m // rows_per_seq
    n = (m // bm) * nf
    cur = lambda t: jnp.minimum(t, n - 1)
    prv = lambda t: jnp.maximum(t - 1, 0)
    f, tg, tv = pl.pallas_call(
        functools.partial(_ffn_prompt_kernel, bm=bm, blocks_per_seq=bps, nf=nf),
        grid=(n + 1,),
        in_specs=[
            pl.BlockSpec((bm, d), lambda t: (cur(t) // nf, 0)),
            pl.BlockSpec((None, None, d, bf), lambda t: (layer, cur(t) % nf, 0, 0)),
            pl.BlockSpec((None, None, d, bf), lambda t: (layer, nf + cur(t) % nf, 0, 0)),
            pl.BlockSpec((None, CONV_WIDTH, bf), lambda t: (layer, 0, cur(t) % nf)),
            pl.BlockSpec((None, CONV_WIDTH, bf), lambda t: (layer, 0, nf + cur(t) % nf)),
            pl.BlockSpec((None, 1, bf), lambda t: (layer, 0, cur(t) % nf)),
            pl.BlockSpec((None, 1, bf), lambda t: (layer, 0, nf + cur(t) % nf)),
            pl.BlockSpec((None, bf, d), lambda t: (layer, prv(t) % nf, 0)),
        ],
        out_specs=[
            pl.BlockSpec((bm, d), lambda t: (prv(t) // nf, 0)),
            pl.BlockSpec((1, SUBLANES, bf), lambda t: (cur(t) // nf, 0, cur(t) % nf)),
            pl.BlockSpec((1, SUBLANES, bf), lambda t: (cur(t) // nf, 0, cur(t) % nf)),
        ],
        out_shape=[
            jax.ShapeDtypeStruct((m, d), F32),
            jax.ShapeDtypeStruct((m // bm, SUBLANES, dff), F32),
            jax.ShapeDtypeStruct((m // bm, SUBLANES, dff), F32),
        ],
        scratch_shapes=[
            pltpu.VMEM((nf, SUBLANES, bf), F32),
            pltpu.VMEM((nf, SUBLANES, bf), F32),
            pltpu.VMEM((2, bm, bf), BF16),
        ],
        compiler_params=_cparams(("arbitrary",)),
        name="ffn_prompt",
    )(h2d, w_up, w_up, cw, cw, cb, cb, w_down)
    k = CONV_WIDTH - 1
    tg = tg.reshape(nseq, bps, SUBLANES, dff)[:, bps - 1, SUBLANES - k:]
    tv = tv.reshape(nseq, bps, SUBLANES, dff)[:, bps - 1, SUBLANES - k:]
    return f, jnp.concatenate([tg, tv], axis=-1)


def _ffn_sample(h2d, w_up, cw, cb, w_down, x2d, g3, state, *, layer, ns):
    m, d = x2d.shape
    dff = w_down.shape[1]
    bf = w_up.shape[3]
    nf = dff // bf
    nb = m // ns
    k = CONV_WIDTH - 1
    assert state.shape[1:] == (nb, k, 2 * dff) and k == 2
    full = lambda j: (0, 0)
    colg = lambda j: (0, j)
    colv = lambda j: (0, nf + j)
    xo, ug, uv = pl.pallas_call(
        functools.partial(_ffn_sample_kernel, ns=ns, nf=nf),
        grid=(nf,),
        in_specs=[
            pl.BlockSpec((m, d), full),
            pl.BlockSpec((None, None, d, bf), lambda j: (layer, j, 0, 0)),
            pl.BlockSpec((None, None, d, bf), lambda j: (layer, nf + j, 0, 0)),
            pl.BlockSpec((None, CONV_WIDTH, bf), lambda j: (layer, 0, j)),
            pl.BlockSpec((None, CONV_WIDTH, bf), lambda j: (layer, 0, nf + j)),
            pl.BlockSpec((None, 1, bf), lambda j: (layer, 0, j)),
            pl.BlockSpec((None, 1, bf), lambda j: (layer, 0, nf + j)),
            pl.BlockSpec((None, bf, d), lambda j: (layer, j, 0)),
            pl.BlockSpec((m, d), full),
            pl.BlockSpec((1, d), full),
            pl.BlockSpec((None, nb, k, bf), lambda j: (layer, 0, 0, j)),
            pl.BlockSpec((None, nb, k, bf), lambda j: (layer, 0, 0, nf + j)),
        ],
        out_specs=[
            pl.BlockSpec((m, d), full),
            pl.BlockSpec((m, bf), colg),
            pl.BlockSpec((m, bf), colg),
        ],
        out_shape=[
            jax.ShapeDtypeStruct((m, d), F32),
            jax.ShapeDtypeStruct((m, dff), F32),
            jax.ShapeDtypeStruct((m, dff), F32),
        ],
        scratch_shapes=[pltpu.VMEM((m, d), F32)],
        compiler_params=_cparams(("arbitrary",)),
        name="ffn_sample",
    )(h2d, w_up, w_up, cw, cw, cb, cb, w_down, x2d, g3, state, state)
    u = jnp.concatenate([ug, uv], axis=-1).reshape(nb, ns, 2 * dff)
    return xo, u[:, ns - k:]


def _rope_tables(pos, reps):
    half = D_ROPE // 2
    inv = ROPE_BASE ** (-jnp.arange(half, dtype=F32) / half)
    ang = pos[:, None] * inv[None, :]
    c, s = jnp.cos(ang), jnp.sin(ang)
    z = jnp.zeros((pos.shape[0], LANES - D_ROPE), F32)
    cos_t = jnp.concatenate([c, c, z], axis=1)
    sin_t = jnp.concatenate([s, s, z], axis=1)
    return jnp.tile(cos_t, (reps, 1)), jnp.tile(sin_t, (reps, 1))


def _signed_partner(w):
    half = D_ROPE // 2
    return jnp.concatenate([-w[..., half:], w[..., :half]], axis=-1)


def _stage_mla_weights(w_dq, w_uq, w_dkv, w_uk, w_uv, w_o):
    w_kr = w_dkv[:, KV_LORA:]
    w1 = jnp.concatenate([w_dq, w_dkv[:, :KV_LORA], w_kr, _signed_partner(w_kr)], axis=1).astype(BF16)
    wq = w_uq.reshape(Q_LORA, N_HEADS, D_NOPE + D_ROPE)
    wq_rope = wq[..., D_NOPE:]
    wuq = jnp.concatenate([wq[..., :D_NOPE], wq_rope, _signed_partner(wq_rope)], axis=-1)
    wuq = wuq.reshape(Q_LORA, N_HEADS * HEAD_PAD).astype(BF16)
    wuk = w_uk.reshape(KV_LORA, N_HEADS * D_NOPE).astype(BF16)
    wuv = w_uv.reshape(KV_LORA, N_HEADS * D_V).astype(BF16)
    wukt = jnp.transpose(w_uk, (1, 2, 0)).astype(BF16)
    return w1, wuq, wuk, wuv, wukt, w_o.astype(BF16)


def _block_rows(m, target):
    bm = min(m, target)
    while m % bm:
        bm //= 2
    return bm


def _block_cols(n, target):
    return max(c for c in range(LANES, min(n, target) + 1, LANES) if n % c == 0)


def kernel(x_prompt, x_sample, cache_ckv, cache_krope, state_pool, state_conv, norm_g, mla_w_dq, mla_q_norm, mla_w_uq, mla_w_dkv, mla_kv_norm, mla_w_uk, mla_w_uv, mla_w_o, pool_w, pool_scale, ffn_w_up, ffn_conv_w, ffn_conv_b, ffn_w_down):
    nbp, sp, d = x_prompt.shape
    nbs, ss, _ = x_sample.shape
    past = cache_ckv.shape[2]
    depth = norm_g.shape[0]
    mp, ms = nbp * sp, nbs * ss
    xp = x_prompt.reshape(mp, d)
    xs = x_sample.reshape(ms, d)

    bm_proj = _block_rows(sp, 256)
    bm_post = _block_rows(sp, 512)
    dff = ffn_w_down.shape[1]
    bm_ffn = _block_rows(sp, 1024)
    bf = _block_cols(dff, 512)
    bm_pool = _block_rows(sp, 512)
    blk_attn = _block_rows(sp, 4096)
    bk_attn = _block_rows(blk_attn, 512)
    bms = _block_rows(ms, 256)

    cos_p, sin_p = _rope_tables(jnp.arange(sp, dtype=jnp.int32).astype(F32), 1)
    cos_s, sin_s = _rope_tables((past + jnp.arange(ss, dtype=jnp.int32)).astype(F32), max(bms // ss, 1))

    w_up = ffn_w_up.astype(BF16).reshape(depth, d, 2 * dff // bf, bf).transpose(0, 2, 1, 3)
    w_down = ffn_w_down.astype(BF16)
    cb = ffn_conv_b.reshape(depth, 1, -1)

    outs = {k: [] for k in ("ckv_p", "kr_p", "pool_p", "conv_p", "ckv_s", "kr_s", "pool_s", "conv_s")}
    pend = None
    for i in range(depth):
        g = norm_g[i].reshape(4, 1, d)
        j = i // 2
        if i % 2 == 0:
            w1, wuq, wuk, wuv, wukt, wo = _stage_mla_weights(
                mla_w_dq[j], mla_w_uq[j], mla_w_dkv[j], mla_w_uk[j], mla_w_uv[j], mla_w_o[j])
            qn = mla_q_norm[j].reshape(1, Q_LORA)
            kvn = mla_kv_norm[j].reshape(1, KV_LORA)
            proj = _mla_project(xp, g[0], w1, qn, kvn, wuq, cos_p, sin_p, wuk, wuv, bm=bm_proj, resid=pend)
            q, ckv, kr, k, v = proj[:5]
            if pend is not None:
                xp = proj[5]
            o = _attention(q.reshape(nbp, sp, -1), k.reshape(nbp, sp, -1), v.reshape(nbp, sp, -1),
                           blk=blk_attn, bk=bk_attn)
            xp, hp = _attn_out(o.reshape(mp, -1), wo, xp, g[1], g[2], bm=bm_post)
            outs["ckv_p"].append(ckv.reshape(nbp, sp, KV_LORA))
            outs["kr_p"].append(kr.reshape(nbp, sp, D_ROPE))
            qs, ckv_s, kr_s = _mla_project(xs, g[0], w1, qn, kvn, wuq, cos_s, sin_s, bm=bms)
            qlat, qrope = _q_latent(qs, wukt)
            olat = _sample_attention(qlat, qrope, cache_ckv, cache_krope, ckv_s, kr_s, layer=j, ns=ss)
            os_ = _o_from_latent(olat, wuv)
            xs, hs = _attn_out(os_, wo, xs, g[1], g[2], bm=bms)
            outs["ckv_s"].append(ckv_s.reshape(nbs, ss, KV_LORA))
            outs["kr_s"].append(kr_s.reshape(nbs, ss, D_ROPE))
        else:
            wp = pool_w[j].astype(BF16)
            sc = pool_scale[j].reshape(1, d)
            xp, hp, tail_p = _pool_layer(xp, xp, g[0], wp, sc, g[1], g[2], bm=bm_pool,
                                         rows_per_seq=sp, pos0=0, halo_normed=False, resid=pend)
            outs["pool_p"].append(tail_p[:, POOL_HALO - POOL_HIST:])
            hist = jnp.pad(state_pool[j], ((0, 0), (POOL_HALO - POOL_HIST, 0), (0, 0))).reshape(-1, d)
            xs, hs, tail_s = _pool_layer(xs, hist, g[0], wp, sc, g[1], g[2], bm=ss,
                                         rows_per_seq=ss, pos0=past, halo_normed=True)
            outs["pool_s"].append(tail_s[:, POOL_HALO - POOL_HIST:])
        fp, cv_p = _ffn_prompt(hp, w_up, ffn_conv_w, cb, w_down, layer=i, bm=bm_ffn, rows_per_seq=sp)
        pend = (fp, g[3])
        xs, cv_s = _ffn_sample(hs, w_up, ffn_conv_w, cb, w_down, xs, g[3], state_conv, layer=i, ns=ss)
        outs["conv_p"].append(cv_p)
        outs["conv_s"].append(cv_s)
    xp = _residual_norm(pend[0], xp, pend[1], bm=bm_post)
    st = lambda k: jnp.stack(outs[k])
    return (xp.reshape(nbp, sp, d), xs.reshape(nbs, ss, d),
            st("ckv_p"), st("kr_p"), st("pool_p"), st("conv_p"),
            st("ckv_s"), st("kr_s"), st("pool_s"), st("conv_s"))
```

```python
import functools

import numpy as np
import jax
import jax.numpy as jnp
from jax import lax
from jax.experimental import pallas as pl
from jax.experimental.pallas import tpu as pltpu

F32 = jnp.float32
BF16 = jnp.bfloat16

CHUNK = 64
N_HEADS = 16
Q_LORA = 512
KV_LORA = 512
D_NOPE = 128
D_ROPE = 64
D_V = 128
ROPE_BASE = 10000.0
ATTN_SCALE = (D_NOPE + D_ROPE) ** -0.5
Q_SCALE = ATTN_SCALE * float(np.log2(np.e))
POOL_WINDOWS = (2, 4, 8, 16)
POOL_HIST = max(POOL_WINDOWS) - 1
CONV_WIDTH = 3
EPS = 1e-6

LANES = 128
SUBLANES = 8
HEAD_PAD = 2 * LANES
POOL_HALO = 16
VMEM_LIMIT = 56 * 1024 * 1024
NEG_BIG = -1e30


def _cparams(sem):
    return pltpu.CompilerParams(dimension_semantics=sem, vmem_limit_bytes=VMEM_LIMIT)


def _rms(xf, g):
    ms = jnp.mean(xf * xf, axis=-1, keepdims=True)
    return xf * lax.rsqrt(ms + EPS) * g


def _dot(a, b):
    return jnp.dot(a, b, preferred_element_type=F32)


def _dot_nt(a, b):
    return lax.dot_general(a, b, (((1,), (1,)), ((), ())), preferred_element_type=F32)


def _rope_cols(r, cos_t, sin_t):
    return r * cos_t + pltpu.roll(r, D_ROPE, axis=1) * sin_t


def _proj_kernel(x_ref, g_ref, w1_ref, qn_ref, kvn_ref, wuq_ref, cos_ref, sin_ref, *rest,
                 with_kv, with_resid, n_alias):
    n_in = (2 if with_kv else 0) + (2 if with_resid else 0) + n_alias
    ins, outs = list(rest[:n_in]), list(rest[n_in:])
    if with_kv:
        wuk_ref, wuv_ref = ins[:2]
        ins = ins[2:]
    x = x_ref[...]
    if with_resid:
        f_ref, g3_ref = ins[:2]
        xo_ref = outs.pop()
        x = x + _rms(f_ref[...], g3_ref[...])
        xo_ref[...] = x
    if with_kv:
        q_ref, ckv_ref, kr_ref, k_ref, v_ref = outs
    else:
        q_ref, ckv_ref, kr_ref = outs
    h = _rms(x, g_ref[...]).astype(BF16)
    y = _dot(h, w1_ref[...])
    cq = _rms(y[:, :Q_LORA], qn_ref[...]).astype(BF16)
    ckv = _rms(y[:, Q_LORA:Q_LORA + KV_LORA], kvn_ref[...])
    ckv_ref[...] = ckv
    cos_t = cos_ref[...]
    sin_t = sin_ref[...]
    kr = _rope_cols(y[:, Q_LORA + KV_LORA:], cos_t, sin_t)
    kr_ref[...] = kr[:, :D_ROPE]
    q = _dot(cq, wuq_ref[...]) * Q_SCALE
    for hd in range(N_HEADS):
        b0 = hd * HEAD_PAD
        q_ref[:, b0:b0 + LANES] = q[:, b0:b0 + LANES].astype(BF16)
        q_ref[:, b0 + LANES:b0 + HEAD_PAD] = _rope_cols(
            q[:, b0 + LANES:b0 + HEAD_PAD], cos_t, sin_t).astype(BF16)
    if with_kv:
        ckv_b = ckv.astype(BF16)
        kr_b = kr.astype(BF16)
        kn = _dot(ckv_b, wuk_ref[...])
        for hd in range(N_HEADS):
            b0 = hd * HEAD_PAD
            k_ref[:, b0:b0 + LANES] = kn[:, hd * D_NOPE:(hd + 1) * D_NOPE].astype(BF16)
            k_ref[:, b0 + LANES:b0 + HEAD_PAD] = kr_b
        v_ref[...] = _dot(ckv_b, wuv_ref[...]).astype(BF16)


def _mla_project(x2d, g, w1, qn, kvn, wuq, cos_t, sin_t, wuk=None, wuv=None, *, bm, resid=None, stack=None):
    m, d = x2d.shape
    with_kv = wuk is not None
    nt = cos_t.shape[0] // bm
    row = lambda i: (i, 0)
    const = lambda i: (0, 0)
    in_specs = [
        pl.BlockSpec((bm, d), row),
        pl.BlockSpec((1, d), const),
        pl.BlockSpec(w1.shape, const),
        pl.BlockSpec((1, Q_LORA), const),
        pl.BlockSpec((1, KV_LORA), const),
        pl.BlockSpec(wuq.shape, const),
        pl.BlockSpec((bm, LANES), lambda i: (i % nt, 0)),
        pl.BlockSpec((bm, LANES), lambda i: (i % nt, 0)),
    ]
    args = [x2d, g, w1, qn, kvn, wuq, cos_t, sin_t]
    out_shape = [
        jax.ShapeDtypeStruct((m, N_HEADS * HEAD_PAD), BF16),
        jax.ShapeDtypeStruct((m, KV_LORA), F32),
        jax.ShapeDtypeStruct((m, D_ROPE), F32),
    ]
    out_specs = [
        pl.BlockSpec((bm, N_HEADS * HEAD_PAD), row),
        pl.BlockSpec((bm, KV_LORA), row),
        pl.BlockSpec((bm, D_ROPE), row),
    ]
    if with_kv:
        in_specs += [pl.BlockSpec(wuk.shape, const), pl.BlockSpec(wuv.shape, const)]
        args += [wuk, wuv]
        out_shape += [
            jax.ShapeDtypeStruct((m, N_HEADS * HEAD_PAD), BF16),
            jax.ShapeDtypeStruct((m, N_HEADS * D_V), BF16),
        ]
        out_specs += [
            pl.BlockSpec((bm, N_HEADS * HEAD_PAD), row),
            pl.BlockSpec((bm, N_HEADS * D_V), row),
        ]
    if resid is not None:
        in_specs += [pl.BlockSpec((bm, d), row), pl.BlockSpec((1, d), const)]
        args += list(resid)
        out_shape.append(jax.ShapeDtypeStruct((m, d), F32))
        out_specs.append(pl.BlockSpec((bm, d), row))
    aliases = {}
    if stack is not None:
        slot, n_slots, bufs = stack
        for o, width in ((1, KV_LORA), (2, D_ROPE)):
            out_shape[o] = jax.ShapeDtypeStruct((n_slots, m, width), F32)
            out_specs[o] = pl.BlockSpec((None, bm, width), lambda i: (slot, i, 0))
        if bufs is not None:
            aliases = {len(args): 1, len(args) + 1: 2}
            in_specs += [pl.BlockSpec(memory_space=pl.ANY)] * 2
            args += list(bufs)
    return pl.pallas_call(
        functools.partial(_proj_kernel, with_kv=with_kv, with_resid=resid is not None, n_alias=len(aliases)),
        grid=(m // bm,),
        in_specs=in_specs,
        out_specs=out_specs,
        out_shape=out_shape,
        input_output_aliases=aliases,
        compiler_params=_cparams(("arbitrary",)),
        name="mla_project",
    )(*args)


def _attn_kernel(q_ref, k_ref, v_ref, o_ref, m_scr, l_scr, acc_scr, *, blk, bk, sub):
    qi = pl.program_id(2)
    m_scr[...] = jnp.full(m_scr.shape, NEG_BIG, F32)
    l_scr[...] = jnp.zeros(l_scr.shape, F32)
    acc_scr[...] = jnp.zeros(acc_scr.shape, F32)

    def step(start, diag):
        scores = []
        for c in range(blk // sub):
            nk = bk if diag is None else min(bk, (c + 1) * sub - diag)
            if nk <= 0:
                continue
            k = k_ref[0, pl.ds(start, nk), :]
            st = _dot_nt(k, q_ref[0, c * sub:(c + 1) * sub, :])
            if diag is not None and diag + nk > c * sub + CHUNK:
                kc = (lax.broadcasted_iota(jnp.int32, st.shape, 0) + diag) // CHUNK
                qc = (lax.broadcasted_iota(jnp.int32, st.shape, 1) + c * sub) // CHUNK
                st = jnp.where(kc <= qc, st, NEG_BIG)
            scores.append((c, nk, st))
        for c, nk, st in scores:
            cs = slice(c * sub, (c + 1) * sub)
            v = v_ref[0, pl.ds(start, nk), :]
            m_prev = m_scr[:, cs]
            m_new = jnp.maximum(m_prev, jnp.max(st, axis=0, keepdims=True))
            alpha = jnp.exp2(m_prev - m_new)
            p = jnp.exp2(st - m_new)
            l_scr[:, cs] = alpha * l_scr[:, cs] + jnp.sum(p, axis=0, keepdims=True)
            pv = lax.dot_general(v, p.astype(BF16), (((0,), (0,)), ((), ())),
                                 preferred_element_type=F32)
            acc_scr[:, cs] = alpha * acc_scr[:, cs] + pv
            m_scr[:, cs] = m_new

    def body(ki, carry):
        step(pl.multiple_of(ki * bk, bk), None)
        return carry

    per_q = blk // bk
    lax.fori_loop(0, qi * per_q, body, 0)
    for d in range(per_q):
        step(pl.multiple_of(qi * blk + d * bk, bk), d * bk)
    o_ref[0] = jnp.transpose(acc_scr[...] / l_scr[...]).astype(o_ref.dtype)


def _attention(q, k, v, *, blk, bk):
    b, s, _ = q.shape
    return pl.pallas_call(
        functools.partial(_attn_kernel, blk=blk, bk=bk, sub=min(bk, HEAD_PAD)),
        grid=(b, N_HEADS, s // blk),
        in_specs=[
            pl.BlockSpec((1, blk, HEAD_PAD), lambda bi, h, qi: (bi, qi, h)),
            pl.BlockSpec((1, s, HEAD_PAD), lambda bi, h, qi: (bi, 0, h)),
            pl.BlockSpec((1, s, D_V), lambda bi, h, qi: (bi, 0, h)),
        ],
        out_specs=pl.BlockSpec((1, blk, D_V), lambda bi, h, qi: (bi, qi, h)),
        out_shape=jax.ShapeDtypeStruct((b, s, N_HEADS * D_V), BF16),
        scratch_shapes=[
            pltpu.VMEM((1, blk), F32),
            pltpu.VMEM((1, blk), F32),
            pltpu.VMEM((D_V, blk), F32),
        ],
        compiler_params=_cparams(("arbitrary", "arbitrary", "arbitrary")),
        name="prompt_attention",
    )(q, k, v)


def _qlat_kernel(q_ref, wukt_ref, qlat_ref, qrope_ref):
    q = q_ref[...]
    qlat_ref[0] = _dot(q[:, :D_NOPE], wukt_ref[0]).astype(BF16)
    qrope_ref[0] = q[:, LANES:]


def _q_latent(q, wukt):
    m = q.shape[0]
    return pl.pallas_call(
        _qlat_kernel,
        grid=(N_HEADS,),
        in_specs=[
            pl.BlockSpec((m, HEAD_PAD), lambda h: (0, h)),
            pl.BlockSpec((1, D_NOPE, KV_LORA), lambda h: (h, 0, 0)),
        ],
        out_specs=[
            pl.BlockSpec((1, m, KV_LORA), lambda h: (h, 0, 0)),
            pl.BlockSpec((1, m, LANES), lambda h: (h, 0, 0)),
        ],
        out_shape=[
            jax.ShapeDtypeStruct((N_HEADS, m, KV_LORA), BF16),
            jax.ShapeDtypeStruct((N_HEADS, m, LANES), BF16),
        ],
        compiler_params=_cparams(("arbitrary",)),
        name="sample_q_latent",
    )(q, wukt)


def _sattn_kernel(qlat_ref, qrope_ref, cckv_ref, ckr_ref, nckv_ref, nkr_ref, o_ref, *, past, ns):
    rows = N_HEADS * ns
    ql = qlat_ref[...].reshape(rows, KV_LORA)
    qr = qrope_ref[...].reshape(rows, LANES)[:, :D_ROPE]
    cc = cckv_ref[0].astype(BF16)
    ck = ckr_ref[0].astype(BF16)
    nc = nckv_ref[...].astype(BF16)
    nk = nkr_ref[...].astype(BF16)
    s1 = _dot_nt(ql, cc) + _dot_nt(qr, ck)
    s2 = _dot_nt(ql, nc) + _dot_nt(qr, nk)
    qt = lax.broadcasted_iota(jnp.int32, s2.shape, 0) % ns
    kt = lax.broadcasted_iota(jnp.int32, s2.shape, 1)
    s2 = jnp.where((past + kt) // CHUNK <= (past + qt) // CHUNK, s2, NEG_BIG)
    m = jnp.maximum(jnp.max(s1, axis=1, keepdims=True), jnp.max(s2, axis=1, keepdims=True))
    p1 = jnp.exp2(s1 - m)
    p2 = jnp.exp2(s2 - m)
    l = jnp.sum(p1, axis=1, keepdims=True) + jnp.sum(p2, axis=1, keepdims=True)
    o = (_dot(p1.astype(BF16), cc) + _dot(p2.astype(BF16), nc)) / l
    o_ref[...] = o.astype(BF16).reshape(N_HEADS, ns, KV_LORA)


def _sample_attention(qlat, qrope, cache_ckv, cache_kr, new_ckv, new_kr, *, layer, ns):
    _, nb, past, _ = cache_ckv.shape
    m = nb * ns
    return pl.pallas_call(
        functools.partial(_sattn_kernel, past=past, ns=ns),
        grid=(nb,),
        in_specs=[
            pl.BlockSpec((N_HEADS, ns, KV_LORA), lambda b: (0, b, 0)),
            pl.BlockSpec((N_HEADS, ns, LANES), lambda b: (0, b, 0)),
            pl.BlockSpec((None, 1, past, KV_LORA), lambda b: (layer, b, 0, 0)),
            pl.BlockSpec((None, 1, past, D_ROPE), lambda b: (layer, b, 0, 0)),
            pl.BlockSpec((ns, KV_LORA), lambda b: (b, 0)),
            pl.BlockSpec((ns, D_ROPE), lambda b: (b, 0)),
        ],
        out_specs=pl.BlockSpec((N_HEADS, ns, KV_LORA), lambda b: (0, b, 0)),
        out_shape=jax.ShapeDtypeStruct((N_HEADS, m, KV_LORA), BF16),
        compiler_params=_cparams(("arbitrary",)),
        name="sample_attention",
    )(qlat, qrope, cache_ckv, cache_kr, new_ckv, new_kr)


def _ouv_kernel(olat_ref, wuv_ref, o_ref):
    o_ref[...] = _dot(olat_ref[0], wuv_ref[...]).astype(BF16)


def _o_from_latent(olat, wuv):
    m = olat.shape[1]
    return pl.pallas_call(
        _ouv_kernel,
        grid=(N_HEADS,),
        in_specs=[
            pl.BlockSpec((1, m, KV_LORA), lambda h: (h, 0, 0)),
            pl.BlockSpec((KV_LORA, D_V), lambda h: (0, h)),
        ],
        out_specs=pl.BlockSpec((m, D_V), lambda h: (0, h)),
        out_shape=jax.ShapeDtypeStruct((m, N_HEADS * D_V), BF16),
        compiler_params=_cparams(("arbitrary",)),
        name="sample_o_from_latent",
    )(olat, wuv)


def _post_kernel(o_ref, w_ref, x_ref, g1_ref, g2_ref, xo_ref, h2_ref):
    xn = x_ref[...] + _rms(_dot(o_ref[...], w_ref[...]), g1_ref[...])
    xo_ref[...] = xn
    h2_ref[...] = _rms(xn, g2_ref[...]).astype(BF16)


def _attn_out(o2d, w_o, x2d, g1, g2, *, bm):
    m, d = x2d.shape
    row = lambda i: (i, 0)
    const = lambda i: (0, 0)
    return pl.pallas_call(
        _post_kernel,
        grid=(m // bm,),
        in_specs=[
            pl.BlockSpec((bm, o2d.shape[1]), row),
            pl.BlockSpec(w_o.shape, const),
            pl.BlockSpec((bm, d), row),
            pl.BlockSpec((1, d), const),
            pl.BlockSpec((1, d), const),
        ],
        out_specs=[pl.BlockSpec((bm, d), row), pl.BlockSpec((bm, d), row)],
        out_shape=[jax.ShapeDtypeStruct((m, d), F32), jax.ShapeDtypeStruct((m, d), BF16)],
        compiler_params=_cparams(("arbitrary",)),
        name="attn_out_post",
    )(o2d, w_o, x2d, g1, g2)


def _pool_kernel(x_ref, halo_ref, g0_ref, w_ref, sc_ref, g1_ref, g2_ref, *rest,
                 bm, blocks_per_seq, pos0, halo_normed, with_resid):
    i = pl.program_id(0)
    x = x_ref[...]
    halo = halo_ref[...]
    if with_resid:
        f_ref, fhalo_ref, g3_ref, xo_ref, h2_ref, tail_ref = rest
        x = x + _rms(f_ref[...], g3_ref[...])
        halo = halo + _rms(fhalo_ref[...], g3_ref[...])
    else:
        xo_ref, h2_ref, tail_ref = rest
    g0 = g0_ref[...]
    h = _rms(x, g0)
    if not halo_normed:
        halo = _rms(halo, g0)
        halo = jnp.where(i % blocks_per_seq == 0, 0.0, halo)
    cat = jnp.concatenate([halo, h], axis=0)
    t = (i % blocks_per_seq) * bm + lax.broadcasted_iota(jnp.int32, (bm, 1), 0)
    posf = (t + pos0).astype(F32)
    gd = x.shape[1] // len(POOL_WINDOWS)
    ys = []
    for g, w in enumerate(POOL_WINDOWS):
        c = cat[:, g * gd:(g + 1) * gd]
        acc = c
        span = 1
        while span < w:
            acc = acc + pltpu.roll(acc, span, axis=0)
            span *= 2
        cnt = jnp.minimum(jnp.float32(w), posf + 1.0)
        mean = acc[POOL_HALO:] / cnt
        dlt = (mean - h[:, g * gd:(g + 1) * gd]).astype(BF16)
        ys.append(_dot(dlt, w_ref[g]))
    y = jnp.concatenate(ys, axis=1) * sc_ref[...]
    xn = x + _rms(y, g1_ref[...])
    xo_ref[...] = xn
    h2_ref[...] = _rms(xn, g2_ref[...]).astype(BF16)
    tail_ref[0] = h[bm - POOL_HALO:]


def _pool_layer(x2d, halo_src, g0, w_pool, scale, g1, g2, *, bm, rows_per_seq, pos0, halo_normed, resid=None):
    m, d = x2d.shape
    bps = rows_per_seq // bm
    nseq = m // rows_per_seq
    row = lambda i: (i, 0)
    const = lambda i: (0, 0)
    if halo_normed:
        halo_map = lambda i: (i, 0)
    else:
        hb = bm // POOL_HALO
        halo_map = lambda i: (jnp.maximum(i * hb - 1, 0), 0)
    in_specs = [
        pl.BlockSpec((bm, d), row),
        pl.BlockSpec((POOL_HALO, d), halo_map),
        pl.BlockSpec((1, d), const),
        pl.BlockSpec(w_pool.shape, lambda i: (0, 0, 0)),
        pl.BlockSpec((1, d), const),
        pl.BlockSpec((1, d), const),
        pl.BlockSpec((1, d), const),
    ]
    args = [x2d, halo_src, g0, w_pool, scale, g1, g2]
    if resid is not None:
        assert not halo_normed
        in_specs += [pl.BlockSpec((bm, d), row), pl.BlockSpec((POOL_HALO, d), halo_map),
                     pl.BlockSpec((1, d), const)]
        args += [resid[0], resid[0], resid[1]]
    return pl.pallas_call(
        functools.partial(_pool_kernel, bm=bm, blocks_per_seq=bps, pos0=pos0, halo_normed=halo_normed,
                          with_resid=resid is not None),
        grid=(m // bm,),
        in_specs=in_specs,
        out_specs=[
            pl.BlockSpec((bm, d), row),
            pl.BlockSpec((bm, d), row),
            pl.BlockSpec((1, POOL_HALO, d), lambda i: (i // bps, 0, 0)),
        ],
        out_shape=[
            jax.ShapeDtypeStruct((m, d), F32),
            jax.ShapeDtypeStruct((m, d), BF16),
            jax.ShapeDtypeStruct((nseq, POOL_HALO, d), F32),
        ],
        compiler_params=_cparams(("arbitrary",)),
        name="pool_layer",
    )(*args)


def _conv3(u, prev1, prev2, cw_ref, cb_ref):
    return cb_ref[...] + cw_ref[0:1, :] * prev2 + cw_ref[1:2, :] * prev1 + cw_ref[2:3, :] * u


def _ffn_tail(j, nf, act, wd_ref, x_ref, g3_ref, xo_ref, acc_scr):
    part = _dot(act.astype(BF16), wd_ref[...])

    @pl.when(j == 0)
    def _():
        acc_scr[...] = part

    @pl.when(j > 0)
    def _():
        acc_scr[...] += part

    @pl.when(j == nf - 1)
    def _():
        xo_ref[...] = x_ref[...] + _rms(acc_scr[...], g3_ref[...])


def _ffn_prompt_kernel(h_ref, wg_ref, wv_ref, cwg_ref, cwv_ref, cbg_ref, cbv_ref, wd_ref,
                       f_ref, tg_ref, tv_ref, cg_scr, cv_scr, act_scr, *, bm, blocks_per_seq, nf):
    t = pl.program_id(0)
    cur = jnp.minimum(t, pl.num_programs(0) - 2)
    i = cur // nf
    j = cur % nf
    prev = jnp.maximum(t - 1, 0)

    @pl.when(t == 0)
    def _():
        act_scr[...] = jnp.zeros(act_scr.shape, BF16)

    @pl.when(prev % nf == 0)
    def _():
        f_ref[...] = jnp.zeros(f_ref.shape, F32)

    f_ref[...] += _dot(act_scr[(t + 1) % 2], wd_ref[...])

    h = h_ref[...]
    first = i % blocks_per_seq == 0

    def branch(w_ref, cw_ref, cb_ref, carry_scr, tail_ref):
        u = _dot(h, w_ref[...])
        prev8 = jnp.where(first, 0.0, carry_scr[j])
        c = _conv3(u, pltpu.roll(u, 1, axis=0), pltpu.roll(u, 2, axis=0), cw_ref, cb_ref)
        head = jnp.concatenate([prev8, u[:SUBLANES]], axis=0)
        c_head = _conv3(head, pltpu.roll(head, 1, axis=0), pltpu.roll(head, 2, axis=0),
                        cw_ref, cb_ref)[SUBLANES:]
        c = jnp.concatenate([c_head, c[SUBLANES:]], axis=0)
        last8 = u[bm - SUBLANES:]
        carry_scr[j] = last8
        tail_ref[0] = last8
        return c

    gate = branch(wg_ref, cwg_ref, cbg_ref, cg_scr, tg_ref)
    val = branch(wv_ref, cwv_ref, cbv_ref, cv_scr, tv_ref)
    act_scr[t % 2] = (gate * jax.nn.sigmoid(gate) * val).astype(BF16)


def _resid_kernel(f_ref, x_ref, g_ref, xo_ref):
    xo_ref[...] = x_ref[...] + _rms(f_ref[...], g_ref[...])


def _residual_norm(f2d, x2d, g, *, bm):
    m, d = x2d.shape
    row = lambda i: (i, 0)
    return pl.pallas_call(
        _resid_kernel,
        grid=(m // bm,),
        in_specs=[pl.BlockSpec((bm, d), row), pl.BlockSpec((bm, d), row), pl.BlockSpec((1, d), lambda i: (0, 0))],
        out_specs=pl.BlockSpec((bm, d), row),
        out_shape=jax.ShapeDtypeStruct((m, d), F32),
        compiler_params=_cparams(("arbitrary",)),
        name="residual_norm",
    )(f2d, x2d, g)


def _ffn_sample_kernel(h_ref, wg_ref, wv_ref, cwg_ref, cwv_ref, cbg_ref, cbv_ref, wd_ref, x_ref, g3_ref,
                       stg_ref, stv_ref, xo_ref, ug_ref, uv_ref, acc_scr, *, ns, nf):
    j = pl.program_id(0)
    h = h_ref[...]
    m = h.shape[0]
    t = lax.broadcasted_iota(jnp.int32, (m, 1), 0) % ns

    def branch(w_ref, cw_ref, cb_ref, st_ref, u_ref):
        u = _dot(h, w_ref[...])
        u_ref[...] = u
        st = st_ref[...]

        def per_row(k):
            row = st[:, k:k + 1, :]
            return jnp.broadcast_to(row, (m // ns, ns, row.shape[2])).reshape(m, row.shape[2])

        older, newer = per_row(0), per_row(1)
        prev1 = jnp.where(t < 1, newer, pltpu.roll(u, 1, axis=0))
        prev2 = jnp.where(t < 1, older, jnp.where(t < 2, newer, pltpu.roll(u, 2, axis=0)))
        return _conv3(u, prev1, prev2, cw_ref, cb_ref)

    gate = branch(wg_ref, cwg_ref, cbg_ref, stg_ref, ug_ref)
    val = branch(wv_ref, cwv_ref, cbv_ref, stv_ref, uv_ref)
    act = gate * jax.nn.sigmoid(gate) * val
    _ffn_tail(j, nf, act, wd_ref, x_ref, g3_ref, xo_ref, acc_scr)


def _ffn_prompt(h2d, w_up, cw, cb, w_down, *, layer, bm, bf, rows_per_seq):
    m, d = h2d.shape
    dff = w_down.shape[1]
    nf = dff // bf
    bps = rows_per_seq // bm
    nseq = m // rows_per_seq
    n = (m // bm) * nf
    cur = lambda t: jnp.minimum(t, n - 1)
    prv = lambda t: jnp.maximum(t - 1, 0)
    f, tg, tv = pl.pallas_call(
        functools.partial(_ffn_prompt_kernel, bm=bm, blocks_per_seq=bps, nf=nf),
        grid=(n + 1,),
        in_specs=[
            pl.BlockSpec((bm, d), lambda t: (cur(t) // nf, 0)),
            pl.BlockSpec((None, d, bf), lambda t: (layer, 0, cur(t) % nf)),
            pl.BlockSpec((None, d, bf), lambda t: (layer, 0, nf + cur(t) % nf)),
            pl.BlockSpec((None, CONV_WIDTH, bf), lambda t: (layer, 0, cur(t) % nf)),
            pl.BlockSpec((None, CONV_WIDTH, bf), lambda t: (layer, 0, nf + cur(t) % nf)),
            pl.BlockSpec((None, 1, bf), lambda t: (layer, 0, cur(t) % nf)),
            pl.BlockSpec((None, 1, bf), lambda t: (layer, 0, nf + cur(t) % nf)),
            pl.BlockSpec((None, bf, d), lambda t: (layer, prv(t) % nf, 0)),
        ],
        out_specs=[
            pl.BlockSpec((bm, d), lambda t: (prv(t) // nf, 0)),
            pl.BlockSpec((1, SUBLANES, bf), lambda t: (cur(t) // nf, 0, cur(t) % nf)),
            pl.BlockSpec((1, SUBLANES, bf), lambda t: (cur(t) // nf, 0, cur(t) % nf)),
        ],
        out_shape=[
            jax.ShapeDtypeStruct((m, d), F32),
            jax.ShapeDtypeStruct((m // bm, SUBLANES, dff), F32),
            jax.ShapeDtypeStruct((m // bm, SUBLANES, dff), F32),
        ],
        scratch_shapes=[
            pltpu.VMEM((nf, SUBLANES, bf), F32),
            pltpu.VMEM((nf, SUBLANES, bf), F32),
            pltpu.VMEM((2, bm, bf), BF16),
        ],
        compiler_params=_cparams(("arbitrary",)),
        name="ffn_prompt",
    )(h2d, w_up, w_up, cw, cw, cb, cb, w_down)
    k = CONV_WIDTH - 1
    tg = tg.reshape(nseq, bps, SUBLANES, dff)[:, bps - 1, SUBLANES - k:]
    tv = tv.reshape(nseq, bps, SUBLANES, dff)[:, bps - 1, SUBLANES - k:]
    return f, jnp.concatenate([tg, tv], axis=-1)


def _ffn_sample(h2d, w_up, cw, cb, w_down, x2d, g3, state, *, layer, bf, ns):
    m, d = x2d.shape
    dff = w_down.shape[1]
    nf = dff // bf
    nb = m // ns
    k = CONV_WIDTH - 1
    assert state.shape[1:] == (nb, k, 2 * dff) and k == 2
    full = lambda j: (0, 0)
    colg = lambda j: (0, j)
    colv = lambda j: (0, nf + j)
    xo, ug, uv = pl.pallas_call(
        functools.partial(_ffn_sample_kernel, ns=ns, nf=nf),
        grid=(nf,),
        in_specs=[
            pl.BlockSpec((m, d), full),
            pl.BlockSpec((None, d, bf), lambda j: (layer, 0, j)),
            pl.BlockSpec((None, d, bf), lambda j: (layer, 0, nf + j)),
            pl.BlockSpec((None, CONV_WIDTH, bf), lambda j: (layer, 0, j)),
            pl.BlockSpec((None, CONV_WIDTH, bf), lambda j: (layer, 0, nf + j)),
            pl.BlockSpec((None, 1, bf), lambda j: (layer, 0, j)),
            pl.BlockSpec((None, 1, bf), lambda j: (layer, 0, nf + j)),
            pl.BlockSpec((None, bf, d), lambda j: (layer, j, 0)),
            pl.BlockSpec((m, d), full),
            pl.BlockSpec((1, d), full),
            pl.BlockSpec((None, nb, k, bf), lambda j: (layer, 0, 0, j)),
            pl.BlockSpec((None, nb, k, bf), lambda j: (layer, 0, 0, nf + j)),
        ],
        out_specs=[
            pl.BlockSpec((m, d), full),
            pl.BlockSpec((m, bf), colg),
            pl.BlockSpec((m, bf), colg),
        ],
        out_shape=[
            jax.ShapeDtypeStruct((m, d), F32),
            jax.ShapeDtypeStruct((m, dff), F32),
            jax.ShapeDtypeStruct((m, dff), F32),
        ],
        scratch_shapes=[pltpu.VMEM((m, d), F32)],
        compiler_params=_cparams(("arbitrary",)),
        name="ffn_sample",
    )(h2d, w_up, w_up, cw, cw, cb, cb, w_down, x2d, g3, state, state)
    u = jnp.concatenate([ug, uv], axis=-1).reshape(nb, ns, 2 * dff)
    return xo, u[:, ns - k:]


def _rope_tables(pos, reps):
    half = D_ROPE // 2
    inv = ROPE_BASE ** (-jnp.arange(half, dtype=F32) / half)
    ang = pos[:, None] * inv[None, :]
    c, s = jnp.cos(ang), jnp.sin(ang)
    z = jnp.zeros((pos.shape[0], LANES - D_ROPE), F32)
    cos_t = jnp.concatenate([c, c, z], axis=1)
    sin_t = jnp.concatenate([s, s, z], axis=1)
    return jnp.tile(cos_t, (reps, 1)), jnp.tile(sin_t, (reps, 1))


def _signed_partner(w):
    half = D_ROPE // 2
    return jnp.concatenate([-w[..., half:], w[..., :half]], axis=-1)


def _stage_mla_weights(w_dq, w_uq, w_dkv, w_uk, w_uv, w_o):
    w_kr = w_dkv[:, KV_LORA:]
    w1 = jnp.concatenate([w_dq, w_dkv[:, :KV_LORA], w_kr, _signed_partner(w_kr)], axis=1).astype(BF16)
    wq = w_uq.reshape(Q_LORA, N_HEADS, D_NOPE + D_ROPE)
    wq_rope = wq[..., D_NOPE:]
    wuq = jnp.concatenate([wq[..., :D_NOPE], wq_rope, _signed_partner(wq_rope)], axis=-1)
    wuq = wuq.reshape(Q_LORA, N_HEADS * HEAD_PAD).astype(BF16)
    wuk = w_uk.reshape(KV_LORA, N_HEADS * D_NOPE).astype(BF16)
    wuv = w_uv.reshape(KV_LORA, N_HEADS * D_V).astype(BF16)
    wukt = jnp.transpose(w_uk, (1, 2, 0)).astype(BF16)
    return w1, wuq, wuk, wuv, wukt, w_o.astype(BF16)


def _block_rows(m, target):
    bm = min(m, target)
    while m % bm:
        bm //= 2
    return bm


def _block_cols(n, target):
    return max(c for c in range(LANES, min(n, target) + 1, LANES) if n % c == 0)


def kernel(x_prompt, x_sample, cache_ckv, cache_krope, state_pool, state_conv, norm_g, mla_w_dq, mla_q_norm, mla_w_uq, mla_w_dkv, mla_kv_norm, mla_w_uk, mla_w_uv, mla_w_o, pool_w, pool_scale, ffn_w_up, ffn_conv_w, ffn_conv_b, ffn_w_down):
    nbp, sp, d = x_prompt.shape
    nbs, ss, _ = x_sample.shape
    past = cache_ckv.shape[2]
    depth = norm_g.shape[0]
    mp, ms = nbp * sp, nbs * ss
    xp = x_prompt.reshape(mp, d)
    xs = x_sample.reshape(ms, d)

    bm_proj = _block_rows(sp, 256)
    bm_post = _block_rows(sp, 512)
    dff = ffn_w_down.shape[1]
    bm_ffn = _block_rows(sp, 1024)
    bf = _block_cols(dff, 512)
    bm_pool = _block_rows(sp, 512)
    blk_attn = _block_rows(sp, 4096)
    bk_attn = _block_rows(blk_attn, 512)
    bms = _block_rows(ms, 256)

    cos_p, sin_p = _rope_tables(jnp.arange(sp, dtype=jnp.int32).astype(F32), 1)
    cos_s, sin_s = _rope_tables((past + jnp.arange(ss, dtype=jnp.int32)).astype(F32), max(bms // ss, 1))

    w_up = ffn_w_up.astype(BF16)
    w_down = ffn_w_down.astype(BF16)
    cb = ffn_conv_b.reshape(depth, 1, -1)

    outs = {k: [] for k in ("pool_p", "conv_p", "ckv_s", "kr_s", "pool_s", "conv_s")}
    n_mla = mla_w_dq.shape[0]
    kv_stack = None
    pend = None
    for i in range(depth):
        g = norm_g[i].reshape(4, 1, d)
        j = i // 2
        if i % 2 == 0:
            w1, wuq, wuk, wuv, wukt, wo = _stage_mla_weights(
                mla_w_dq[j], mla_w_uq[j], mla_w_dkv[j], mla_w_uk[j], mla_w_uv[j], mla_w_o[j])
            qn = mla_q_norm[j].reshape(1, Q_LORA)
            kvn = mla_kv_norm[j].reshape(1, KV_LORA)
            proj = _mla_project(xp, g[0], w1, qn, kvn, wuq, cos_p, sin_p, wuk, wuv, bm=bm_proj, resid=pend,
                                stack=(j, n_mla, kv_stack))
            q, ckv_all, kr_all, k, v = proj[:5]
            kv_stack = (ckv_all, kr_all)
            if pend is not None:
                xp = proj[5]
            o = _attention(q.reshape(nbp, sp, -1), k.reshape(nbp, sp, -1), v.reshape(nbp, sp, -1),
                           blk=blk_attn, bk=bk_attn)
            xp, hp = _attn_out(o.reshape(mp, -1), wo, xp, g[1], g[2], bm=bm_post)
            qs, ckv_s, kr_s = _mla_project(xs, g[0], w1, qn, kvn, wuq, cos_s, sin_s, bm=bms)
            qlat, qrope = _q_latent(qs, wukt)
            olat = _sample_attention(qlat, qrope, cache_ckv, cache_krope, ckv_s, kr_s, layer=j, ns=ss)
            os_ = _o_from_latent(olat, wuv)
            xs, hs = _attn_out(os_, wo, xs, g[1], g[2], bm=bms)
            outs["ckv_s"].append(ckv_s.reshape(nbs, ss, KV_LORA))
            outs["kr_s"].append(kr_s.reshape(nbs, ss, D_ROPE))
        else:
            wp = pool_w[j].astype(BF16)
            sc = pool_scale[j].reshape(1, d)
            xp, hp, tail_p = _pool_layer(xp, xp, g[0], wp, sc, g[1], g[2], bm=bm_pool,
                                         rows_per_seq=sp, pos0=0, halo_normed=False, resid=pend)
            outs["pool_p"].append(tail_p[:, POOL_HALO - POOL_HIST:])
            hist = jnp.pad(state_pool[j], ((0, 0), (POOL_HALO - POOL_HIST, 0), (0, 0))).reshape(-1, d)
            xs, hs, tail_s = _pool_layer(xs, hist, g[0], wp, sc, g[1], g[2], bm=ss,
                                         rows_per_seq=ss, pos0=past, halo_normed=True)
            outs["pool_s"].append(tail_s[:, POOL_HALO - POOL_HIST:])
        fp, cv_p = _ffn_prompt(hp, w_up, ffn_conv_w, cb, w_down, layer=i, bm=bm_ffn, bf=bf, rows_per_seq=sp)
        pend = (fp, g[3])
        xs, cv_s = _ffn_sample(hs, w_up, ffn_conv_w, cb, w_down, xs, g[3], state_conv, layer=i, bf=bf, ns=ss)
        outs["conv_p"].append(cv_p)
        outs["conv_s"].append(cv_s)
    xp = _residual_norm(pend[0], xp, pend[1], bm=bm_post)
    st = lambda k: jnp.stack(outs[k])
    return (xp.reshape(nbp, sp, d), xs.reshape(nbs, ss, d),
            kv_stack[0].reshape(n_mla, nbp, sp, KV_LORA), kv_stack[1].reshape(n_mla, nbp, sp, D_ROPE),
            st("pool_p"), st("conv_p"),
            st("ckv_s"), st("kr_s"), st("pool_s"), st("conv_s"))
```

```python
import functools

import numpy as np
import jax
import jax.numpy as jnp
from jax import lax
from jax.experimental import pallas as pl
from jax.experimental.pallas import tpu as pltpu

F32 = jnp.float32
BF16 = jnp.bfloat16

CHUNK = 64
N_HEADS = 16
Q_LORA = 512
KV_LORA = 512
D_NOPE = 128
D_ROPE = 64
D_V = 128
ROPE_BASE = 10000.0
ATTN_SCALE = (D_NOPE + D_ROPE) ** -0.5
Q_SCALE = ATTN_SCALE * float(np.log2(np.e))
POOL_WINDOWS = (2, 4, 8, 16)
POOL_HIST = max(POOL_WINDOWS) - 1
CONV_WIDTH = 3
EPS = 1e-6

LANES = 128
SUBLANES = 8
HEAD_PAD = 2 * LANES
POOL_HALO = 16
VMEM_LIMIT = 56 * 1024 * 1024
NEG_BIG = -1e30


def _cparams(sem):
    return pltpu.CompilerParams(dimension_semantics=sem, vmem_limit_bytes=VMEM_LIMIT)


def _rms(xf, g):
    ms = jnp.mean(xf * xf, axis=-1, keepdims=True)
    return xf * lax.rsqrt(ms + EPS) * g


def _dot(a, b):
    return jnp.dot(a, b, preferred_element_type=F32)


def _dot_nt(a, b):
    return lax.dot_general(a, b, (((1,), (1,)), ((), ())), preferred_element_type=F32)


def _rope_cols(r, cos_t, sin_t):
    return r * cos_t + pltpu.roll(r, D_ROPE, axis=1) * sin_t


def _proj_kernel(x_ref, g_ref, w1_ref, qn_ref, kvn_ref, wuq_ref, cos_ref, sin_ref, *rest,
                 with_kv, with_resid, n_alias):
    n_in = (2 if with_kv else 0) + (2 if with_resid else 0) + n_alias
    ins, outs = list(rest[:n_in]), list(rest[n_in:])
    if with_kv:
        wuk_ref, wuv_ref = ins[:2]
        ins = ins[2:]
    x = x_ref[...]
    if with_resid:
        f_ref, g3_ref = ins[:2]
        xo_ref = outs.pop()
        x = x + _rms(f_ref[...], g3_ref[...])
        xo_ref[...] = x
    if with_kv:
        q_ref, ckv_ref, kr_ref, k_ref, v_ref = outs
    else:
        q_ref, ckv_ref, kr_ref = outs
    h = _rms(x, g_ref[...]).astype(BF16)
    y = _dot(h, w1_ref[...])
    cq = _rms(y[:, :Q_LORA], qn_ref[...]).astype(BF16)
    ckv = _rms(y[:, Q_LORA:Q_LORA + KV_LORA], kvn_ref[...])
    ckv_ref[...] = ckv
    cos_t = cos_ref[...]
    sin_t = sin_ref[...]
    kr = _rope_cols(y[:, Q_LORA + KV_LORA:], cos_t, sin_t)
    kr_ref[...] = kr[:, :D_ROPE]
    q = _dot(cq, wuq_ref[...]) * Q_SCALE
    for hd in range(N_HEADS):
        b0 = hd * HEAD_PAD
        q_ref[:, b0:b0 + LANES] = q[:, b0:b0 + LANES].astype(BF16)
        q_ref[:, b0 + LANES:b0 + HEAD_PAD] = _rope_cols(
            q[:, b0 + LANES:b0 + HEAD_PAD], cos_t, sin_t).astype(BF16)
    if with_kv:
        ckv_b = ckv.astype(BF16)
        kr_b = kr.astype(BF16)
        kn = _dot(ckv_b, wuk_ref[...])
        for hd in range(N_HEADS):
            b0 = hd * HEAD_PAD
            k_ref[:, b0:b0 + LANES] = kn[:, hd * D_NOPE:(hd + 1) * D_NOPE].astype(BF16)
            k_ref[:, b0 + LANES:b0 + HEAD_PAD] = kr_b
        v_ref[...] = _dot(ckv_b, wuv_ref[...]).astype(BF16)


def _mla_project(x2d, g, w1, qn, kvn, wuq, cos_t, sin_t, wuk=None, wuv=None, *, bm, resid=None, stack=None):
    m, d = x2d.shape
    with_kv = wuk is not None
    nt = cos_t.shape[0] // bm
    row = lambda i: (i, 0)
    const = lambda i: (0, 0)
    in_specs = [
        pl.BlockSpec((bm, d), row),
        pl.BlockSpec((1, d), const),
        pl.BlockSpec(w1.shape, const),
        pl.BlockSpec((1, Q_LORA), const),
        pl.BlockSpec((1, KV_LORA), const),
        pl.BlockSpec(wuq.shape, const),
        pl.BlockSpec((bm, LANES), lambda i: (i % nt, 0)),
        pl.BlockSpec((bm, LANES), lambda i: (i % nt, 0)),
    ]
    args = [x2d, g, w1, qn, kvn, wuq, cos_t, sin_t]
    out_shape = [
        jax.ShapeDtypeStruct((m, N_HEADS * HEAD_PAD), BF16),
        jax.ShapeDtypeStruct((m, KV_LORA), F32),
        jax.ShapeDtypeStruct((m, D_ROPE), F32),
    ]
    out_specs = [
        pl.BlockSpec((bm, N_HEADS * HEAD_PAD), row),
        pl.BlockSpec((bm, KV_LORA), row),
        pl.BlockSpec((bm, D_ROPE), row),
    ]
    if with_kv:
        in_specs += [pl.BlockSpec(wuk.shape, const), pl.BlockSpec(wuv.shape, const)]
        args += [wuk, wuv]
        out_shape += [
            jax.ShapeDtypeStruct((m, N_HEADS * HEAD_PAD), BF16),
            jax.ShapeDtypeStruct((m, N_HEADS * D_V), BF16),
        ]
        out_specs += [
            pl.BlockSpec((bm, N_HEADS * HEAD_PAD), row),
            pl.BlockSpec((bm, N_HEADS * D_V), row),
        ]
    if resid is not None:
        in_specs += [pl.BlockSpec((bm, d), row), pl.BlockSpec((1, d), const)]
        args += list(resid)
        out_shape.append(jax.ShapeDtypeStruct((m, d), F32))
        out_specs.append(pl.BlockSpec((bm, d), row))
    aliases = {}
    if stack is not None:
        slot, n_slots, bufs = stack
        for o, width in ((1, KV_LORA), (2, D_ROPE)):
            out_shape[o] = jax.ShapeDtypeStruct((n_slots, m, width), F32)
            out_specs[o] = pl.BlockSpec((None, bm, width), lambda i: (slot, i, 0))
        if bufs is not None:
            aliases = {len(args): 1, len(args) + 1: 2}
            in_specs += [pl.BlockSpec(memory_space=pl.ANY)] * 2
            args += list(bufs)
    return pl.pallas_call(
        functools.partial(_proj_kernel, with_kv=with_kv, with_resid=resid is not None, n_alias=len(aliases)),
        grid=(m // bm,),
        in_specs=in_specs,
        out_specs=out_specs,
        out_shape=out_shape,
        input_output_aliases=aliases,
        compiler_params=_cparams(("arbitrary",)),
        name="mla_project",
    )(*args)


def _attn_kernel(q_ref, k_ref, v_ref, o_ref, m_scr, l_scr, acc_scr, *, blk, bk, sub):
    qi = pl.program_id(2)
    m_scr[...] = jnp.full(m_scr.shape, NEG_BIG, F32)
    l_scr[...] = jnp.zeros(l_scr.shape, F32)
    acc_scr[...] = jnp.zeros(acc_scr.shape, F32)

    def step(start, diag):
        scores = []
        for c in range(blk // sub):
            nk = bk if diag is None else min(bk, (c + 1) * sub - diag)
            if nk <= 0:
                continue
            k = k_ref[0, pl.ds(start, nk), :]
            st = _dot_nt(k, q_ref[0, c * sub:(c + 1) * sub, :])
            if diag is not None and diag + nk > c * sub + CHUNK:
                kc = (lax.broadcasted_iota(jnp.int32, st.shape, 0) + diag) // CHUNK
                qc = (lax.broadcasted_iota(jnp.int32, st.shape, 1) + c * sub) // CHUNK
                st = jnp.where(kc <= qc, st, NEG_BIG)
            scores.append((c, nk, st))
        for c, nk, st in scores:
            cs = slice(c * sub, (c + 1) * sub)
            v = v_ref[0, pl.ds(start, nk), :]
            m_prev = m_scr[:, cs]
            m_new = jnp.maximum(m_prev, jnp.max(st, axis=0, keepdims=True))
            alpha = jnp.exp2(m_prev - m_new)
            p = jnp.exp2(st - m_new)
            l_scr[:, cs] = alpha * l_scr[:, cs] + jnp.sum(p, axis=0, keepdims=True)
            pv = lax.dot_general(v, p.astype(BF16), (((0,), (0,)), ((), ())),
                                 preferred_element_type=F32)
            acc_scr[:, cs] = alpha * acc_scr[:, cs] + pv
            m_scr[:, cs] = m_new

    def body(ki, carry):
        step(pl.multiple_of(ki * bk, bk), None)
        return carry

    per_q = blk // bk
    lax.fori_loop(0, qi * per_q, body, 0)
    for d in range(per_q):
        step(pl.multiple_of(qi * blk + d * bk, bk), d * bk)
    o_ref[0] = jnp.transpose(acc_scr[...] / l_scr[...]).astype(o_ref.dtype)


def _attention(q, k, v, *, blk, bk):
    b, s, _ = q.shape
    return pl.pallas_call(
        functools.partial(_attn_kernel, blk=blk, bk=bk, sub=min(bk, HEAD_PAD)),
        grid=(b, N_HEADS, s // blk),
        in_specs=[
            pl.BlockSpec((1, blk, HEAD_PAD), lambda bi, h, qi: (bi, qi, h)),
            pl.BlockSpec((1, s, HEAD_PAD), lambda bi, h, qi: (bi, 0, h)),
            pl.BlockSpec((1, s, D_V), lambda bi, h, qi: (bi, 0, h)),
        ],
        out_specs=pl.BlockSpec((1, blk, D_V), lambda bi, h, qi: (bi, qi, h)),
        out_shape=jax.ShapeDtypeStruct((b, s, N_HEADS * D_V), BF16),
        scratch_shapes=[
            pltpu.VMEM((1, blk), F32),
            pltpu.VMEM((1, blk), F32),
            pltpu.VMEM((D_V, blk), F32),
        ],
        compiler_params=_cparams(("arbitrary", "arbitrary", "arbitrary")),
        name="prompt_attention",
    )(q, k, v)


def _qlat_kernel(q_ref, wukt_ref, qlat_ref, qrope_ref):
    q = q_ref[...]
    qlat_ref[0] = _dot(q[:, :D_NOPE], wukt_ref[0]).astype(BF16)
    qrope_ref[0] = q[:, LANES:]


def _q_latent(q, wukt):
    m = q.shape[0]
    return pl.pallas_call(
        _qlat_kernel,
        grid=(N_HEADS,),
        in_specs=[
            pl.BlockSpec((m, HEAD_PAD), lambda h: (0, h)),
            pl.BlockSpec((1, D_NOPE, KV_LORA), lambda h: (h, 0, 0)),
        ],
        out_specs=[
            pl.BlockSpec((1, m, KV_LORA), lambda h: (h, 0, 0)),
            pl.BlockSpec((1, m, LANES), lambda h: (h, 0, 0)),
        ],
        out_shape=[
            jax.ShapeDtypeStruct((N_HEADS, m, KV_LORA), BF16),
            jax.ShapeDtypeStruct((N_HEADS, m, LANES), BF16),
        ],
        compiler_params=_cparams(("arbitrary",)),
        name="sample_q_latent",
    )(q, wukt)


def _sattn_kernel(qlat_ref, qrope_ref, cckv_ref, ckr_ref, nckv_ref, nkr_ref, o_ref, *, past, ns):
    rows = N_HEADS * ns
    ql = qlat_ref[...].reshape(rows, KV_LORA)
    qr = qrope_ref[...].reshape(rows, LANES)[:, :D_ROPE]
    cc = cckv_ref[0].astype(BF16)
    ck = ckr_ref[0].astype(BF16)
    nc = nckv_ref[...].astype(BF16)
    nk = nkr_ref[...].astype(BF16)
    s1 = _dot_nt(ql, cc) + _dot_nt(qr, ck)
    s2 = _dot_nt(ql, nc) + _dot_nt(qr, nk)
    qt = lax.broadcasted_iota(jnp.int32, s2.shape, 0) % ns
    kt = lax.broadcasted_iota(jnp.int32, s2.shape, 1)
    s2 = jnp.where((past + kt) // CHUNK <= (past + qt) // CHUNK, s2, NEG_BIG)
    m = jnp.maximum(jnp.max(s1, axis=1, keepdims=True), jnp.max(s2, axis=1, keepdims=True))
    p1 = jnp.exp2(s1 - m)
    p2 = jnp.exp2(s2 - m)
    l = jnp.sum(p1, axis=1, keepdims=True) + jnp.sum(p2, axis=1, keepdims=True)
    o = (_dot(p1.astype(BF16), cc) + _dot(p2.astype(BF16), nc)) / l
    o_ref[...] = o.astype(BF16).reshape(N_HEADS, ns, KV_LORA)


def _sample_attention(qlat, qrope, cache_ckv, cache_kr, new_ckv, new_kr, *, layer, ns):
    _, nb, past, _ = cache_ckv.shape
    m = nb * ns
    return pl.pallas_call(
        functools.partial(_sattn_kernel, past=past, ns=ns),
        grid=(nb,),
        in_specs=[
            pl.BlockSpec((N_HEADS, ns, KV_LORA), lambda b: (0, b, 0)),
            pl.BlockSpec((N_HEADS, ns, LANES), lambda b: (0, b, 0)),
            pl.BlockSpec((None, 1, past, KV_LORA), lambda b: (layer, b, 0, 0)),
            pl.BlockSpec((None, 1, past, D_ROPE), lambda b: (layer, b, 0, 0)),
            pl.BlockSpec((ns, KV_LORA), lambda b: (b, 0)),
            pl.BlockSpec((ns, D_ROPE), lambda b: (b, 0)),
        ],
        out_specs=pl.BlockSpec((N_HEADS, ns, KV_LORA), lambda b: (0, b, 0)),
        out_shape=jax.ShapeDtypeStruct((N_HEADS, m, KV_LORA), BF16),
        compiler_params=_cparams(("arbitrary",)),
        name="sample_attention",
    )(qlat, qrope, cache_ckv, cache_kr, new_ckv, new_kr)


def _ouv_kernel(olat_ref, wuv_ref, o_ref):
    o_ref[...] = _dot(olat_ref[0], wuv_ref[...]).astype(BF16)


def _o_from_latent(olat, wuv):
    m = olat.shape[1]
    return pl.pallas_call(
        _ouv_kernel,
        grid=(N_HEADS,),
        in_specs=[
            pl.BlockSpec((1, m, KV_LORA), lambda h: (h, 0, 0)),
            pl.BlockSpec((KV_LORA, D_V), lambda h: (0, h)),
        ],
        out_specs=pl.BlockSpec((m, D_V), lambda h: (0, h)),
        out_shape=jax.ShapeDtypeStruct((m, N_HEADS * D_V), BF16),
        compiler_params=_cparams(("arbitrary",)),
        name="sample_o_from_latent",
    )(olat, wuv)


def _post_kernel(o_ref, w_ref, x_ref, g1_ref, g2_ref, xo_ref, h2_ref):
    xn = x_ref[...] + _rms(_dot(o_ref[...], w_ref[...]), g1_ref[...])
    xo_ref[...] = xn
    h2_ref[...] = _rms(xn, g2_ref[...]).astype(BF16)


def _attn_out(o2d, w_o, x2d, g1, g2, *, bm):
    m, d = x2d.shape
    row = lambda i: (i, 0)
    const = lambda i: (0, 0)
    return pl.pallas_call(
        _post_kernel,
        grid=(m // bm,),
        in_specs=[
            pl.BlockSpec((bm, o2d.shape[1]), row),
            pl.BlockSpec(w_o.shape, const),
            pl.BlockSpec((bm, d), row),
            pl.BlockSpec((1, d), const),
            pl.BlockSpec((1, d), const),
        ],
        out_specs=[pl.BlockSpec((bm, d), row), pl.BlockSpec((bm, d), row)],
        out_shape=[jax.ShapeDtypeStruct((m, d), F32), jax.ShapeDtypeStruct((m, d), BF16)],
        compiler_params=_cparams(("arbitrary",)),
        name="attn_out_post",
    )(o2d, w_o, x2d, g1, g2)


def _pool_kernel(x_ref, halo_ref, g0_ref, w_ref, sc_ref, g1_ref, g2_ref, *rest,
                 bm, blocks_per_seq, pos0, halo_normed, with_resid):
    i = pl.program_id(0)
    x = x_ref[...]
    halo = halo_ref[...]
    if with_resid:
        f_ref, fhalo_ref, g3_ref, xo_ref, h2_ref, tail_ref = rest
        x = x + _rms(f_ref[...], g3_ref[...])
        halo = halo + _rms(fhalo_ref[...], g3_ref[...])
    else:
        xo_ref, h2_ref, tail_ref = rest
    g0 = g0_ref[...]
    h = _rms(x, g0)
    if not halo_normed:
        halo = _rms(halo, g0)
        halo = jnp.where(i % blocks_per_seq == 0, 0.0, halo)
    cat = jnp.concatenate([halo, h], axis=0)
    t = (i % blocks_per_seq) * bm + lax.broadcasted_iota(jnp.int32, (bm, 1), 0)
    posf = (t + pos0).astype(F32)
    gd = x.shape[1] // len(POOL_WINDOWS)
    ys = []
    for g, w in enumerate(POOL_WINDOWS):
        c = cat[:, g * gd:(g + 1) * gd]
        acc = c
        span = 1
        while span < w:
            acc = acc + pltpu.roll(acc, span, axis=0)
            span *= 2
        cnt = jnp.minimum(jnp.float32(w), posf + 1.0)
        mean = acc[POOL_HALO:] / cnt
        dlt = (mean - h[:, g * gd:(g + 1) * gd]).astype(BF16)
        ys.append(_dot(dlt, w_ref[g]))
    y = jnp.concatenate(ys, axis=1) * sc_ref[...]
    xn = x + _rms(y, g1_ref[...])
    xo_ref[...] = xn
    h2_ref[...] = _rms(xn, g2_ref[...]).astype(BF16)
    tail_ref[0] = h[bm - POOL_HALO:]


def _pool_layer(x2d, halo_src, g0, w_pool, scale, g1, g2, *, bm, rows_per_seq, pos0, halo_normed, resid=None):
    m, d = x2d.shape
    bps = rows_per_seq // bm
    nseq = m // rows_per_seq
    row = lambda i: (i, 0)
    const = lambda i: (0, 0)
    if halo_normed:
        halo_map = lambda i: (i, 0)
    else:
        hb = bm // POOL_HALO
        halo_map = lambda i: (jnp.maximum(i * hb - 1, 0), 0)
    in_specs = [
        pl.BlockSpec((bm, d), row),
        pl.BlockSpec((POOL_HALO, d), halo_map),
        pl.BlockSpec((1, d), const),
        pl.BlockSpec(w_pool.shape, lambda i: (0, 0, 0)),
        pl.BlockSpec((1, d), const),
        pl.BlockSpec((1, d), const),
        pl.BlockSpec((1, d), const),
    ]
    args = [x2d, halo_src, g0, w_pool, scale, g1, g2]
    if resid is not None:
        assert not halo_normed
        in_specs += [pl.BlockSpec((bm, d), row), pl.BlockSpec((POOL_HALO, d), halo_map),
                     pl.BlockSpec((1, d), const)]
        args += [resid[0], resid[0], resid[1]]
    return pl.pallas_call(
        functools.partial(_pool_kernel, bm=bm, blocks_per_seq=bps, pos0=pos0, halo_normed=halo_normed,
                          with_resid=resid is not None),
        grid=(m // bm,),
        in_specs=in_specs,
        out_specs=[
            pl.BlockSpec((bm, d), row),
            pl.BlockSpec((bm, d), row),
            pl.BlockSpec((1, POOL_HALO, d), lambda i: (i // bps, 0, 0)),
        ],
        out_shape=[
            jax.ShapeDtypeStruct((m, d), F32),
            jax.ShapeDtypeStruct((m, d), BF16),
            jax.ShapeDtypeStruct((nseq, POOL_HALO, d), F32),
        ],
        compiler_params=_cparams(("arbitrary",)),
        name="pool_layer",
    )(*args)


def _conv3(u, prev1, prev2, taps, bias):
    return bias + taps[0:1, :] * prev2 + taps[1:2, :] * prev1 + taps[2:3, :] * u


def _ffn_tail(j, nf, act, wd_ref, x_ref, g3_ref, xo_ref, acc_scr):
    part = _dot(act.astype(BF16), wd_ref[...])

    @pl.when(j == 0)
    def _():
        acc_scr[...] = part

    @pl.when(j > 0)
    def _():
        acc_scr[...] += part

    @pl.when(j == nf - 1)
    def _():
        xo_ref[...] = x_ref[...] + _rms(acc_scr[...], g3_ref[...])


def _ffn_prompt_kernel(h_ref, wg_ref, wv_ref, cwb_ref, wd_ref, f_ref, tail_ref, cg_scr, cv_scr, act_scr,
                       *, bm, blocks_per_seq, nf):
    t = pl.program_id(0)
    cur = jnp.minimum(t, pl.num_programs(0) - 2)
    i = cur // nf
    j = cur % nf
    prev = jnp.maximum(t - 1, 0)

    @pl.when(t == 0)
    def _():
        act_scr[...] = jnp.zeros(act_scr.shape, BF16)

    @pl.when(prev % nf == 0)
    def _():
        f_ref[...] = jnp.zeros(f_ref.shape, F32)

    f_ref[...] += _dot(act_scr[(t + 1) % 2], wd_ref[...])

    h = h_ref[...]
    first = i % blocks_per_seq == 0

    def branch(w_ref, col, carry_scr):
        u = _dot(h, w_ref[...])
        p = cwb_ref[col]
        taps, bias = p[:CONV_WIDTH], p[CONV_WIDTH:]
        prev8 = jnp.where(first, 0.0, carry_scr[j])
        c = _conv3(u, pltpu.roll(u, 1, axis=0), pltpu.roll(u, 2, axis=0), taps, bias)
        head = jnp.concatenate([prev8, u[:SUBLANES]], axis=0)
        c_head = _conv3(head, pltpu.roll(head, 1, axis=0), pltpu.roll(head, 2, axis=0),
                        taps, bias)[SUBLANES:]
        c = jnp.concatenate([c_head, c[SUBLANES:]], axis=0)
        last8 = u[bm - SUBLANES:]
        carry_scr[j] = last8
        tail_ref[0, col] = last8
        return c

    gate = branch(wg_ref, j, cg_scr)
    val = branch(wv_ref, nf + j, cv_scr)
    act_scr[t % 2] = (gate * jax.nn.sigmoid(gate) * val).astype(BF16)


def _resid_kernel(f_ref, x_ref, g_ref, xo_ref):
    xo_ref[...] = x_ref[...] + _rms(f_ref[...], g_ref[...])


def _residual_norm(f2d, x2d, g, *, bm):
    m, d = x2d.shape
    row = lambda i: (i, 0)
    return pl.pallas_call(
        _resid_kernel,
        grid=(m // bm,),
        in_specs=[pl.BlockSpec((bm, d), row), pl.BlockSpec((bm, d), row), pl.BlockSpec((1, d), lambda i: (0, 0))],
        out_specs=pl.BlockSpec((bm, d), row),
        out_shape=jax.ShapeDtypeStruct((m, d), F32),
        compiler_params=_cparams(("arbitrary",)),
        name="residual_norm",
    )(f2d, x2d, g)


def _ffn_sample_kernel(h_ref, wg_ref, wv_ref, cwg_ref, cwv_ref, cbg_ref, cbv_ref, wd_ref, x_ref, g3_ref,
                       stg_ref, stv_ref, xo_ref, ug_ref, uv_ref, acc_scr, *, ns, nf):
    j = pl.program_id(0)
    h = h_ref[...]
    m = h.shape[0]
    t = lax.broadcasted_iota(jnp.int32, (m, 1), 0) % ns

    def branch(w_ref, cw_ref, cb_ref, st_ref, u_ref):
        u = _dot(h, w_ref[...])
        u_ref[...] = u
        st = st_ref[...]

        def per_row(k):
            row = st[:, k:k + 1, :]
            return jnp.broadcast_to(row, (m // ns, ns, row.shape[2])).reshape(m, row.shape[2])

        older, newer = per_row(0), per_row(1)
        prev1 = jnp.where(t < 1, newer, pltpu.roll(u, 1, axis=0))
        prev2 = jnp.where(t < 1, older, jnp.where(t < 2, newer, pltpu.roll(u, 2, axis=0)))
        return _conv3(u, prev1, prev2, cw_ref[...], cb_ref[...])

    gate = branch(wg_ref, cwg_ref, cbg_ref, stg_ref, ug_ref)
    val = branch(wv_ref, cwv_ref, cbv_ref, stv_ref, uv_ref)
    act = gate * jax.nn.sigmoid(gate) * val
    _ffn_tail(j, nf, act, wd_ref, x_ref, g3_ref, xo_ref, acc_scr)


def _ffn_prompt(h2d, w_up, cwb, w_down, *, layer, bm, rows_per_seq):
    m, d = h2d.shape
    dff = w_down.shape[1]
    bf = cwb.shape[3]
    nf = dff // bf
    bps = rows_per_seq // bm
    nseq = m // rows_per_seq
    n = (m // bm) * nf
    cur = lambda t: jnp.minimum(t, n - 1)
    prv = lambda t: jnp.maximum(t - 1, 0)
    f, tails = pl.pallas_call(
        functools.partial(_ffn_prompt_kernel, bm=bm, blocks_per_seq=bps, nf=nf),
        grid=(n + 1,),
        in_specs=[
            pl.BlockSpec((bm, d), lambda t: (cur(t) // nf, 0)),
            pl.BlockSpec((None, d, bf), lambda t: (layer, 0, cur(t) % nf)),
            pl.BlockSpec((None, d, bf), lambda t: (layer, 0, nf + cur(t) % nf)),
            pl.BlockSpec((None, 2 * nf, CONV_WIDTH + 1, bf), lambda t: (layer, 0, 0, 0)),
            pl.BlockSpec((None, bf, d), lambda t: (layer, prv(t) % nf, 0)),
        ],
        out_specs=[
            pl.BlockSpec((bm, d), lambda t: (prv(t) // nf, 0)),
            pl.BlockSpec((1, 2 * nf, SUBLANES, bf), lambda t: (cur(t) // nf, 0, 0, 0)),
        ],
        out_shape=[
            jax.ShapeDtypeStruct((m, d), F32),
            jax.ShapeDtypeStruct((m // bm, 2 * nf, SUBLANES, bf), F32),
        ],
        scratch_shapes=[
            pltpu.VMEM((nf, SUBLANES, bf), F32),
            pltpu.VMEM((nf, SUBLANES, bf), F32),
            pltpu.VMEM((2, bm, bf), BF16),
        ],
        compiler_params=_cparams(("arbitrary",)),
        name="ffn_prompt",
    )(h2d, w_up, w_up, cwb, w_down)
    k = CONV_WIDTH - 1
    tails = tails.reshape(nseq, bps, 2 * nf, SUBLANES, bf)[:, bps - 1, :, SUBLANES - k:]
    return f, tails.transpose(0, 2, 1, 3).reshape(nseq, k, 2 * dff)


def _ffn_sample(h2d, w_up, cw, cb, w_down, x2d, g3, state, *, layer, bf, ns):
    m, d = x2d.shape
    dff = w_down.shape[1]
    nf = dff // bf
    nb = m // ns
    k = CONV_WIDTH - 1
    assert state.shape[1:] == (nb, k, 2 * dff) and k == 2
    full = lambda j: (0, 0)
    colg = lambda j: (0, j)
    colv = lambda j: (0, nf + j)
    xo, ug, uv = pl.pallas_call(
        functools.partial(_ffn_sample_kernel, ns=ns, nf=nf),
        grid=(nf,),
        in_specs=[
            pl.BlockSpec((m, d), full),
            pl.BlockSpec((None, d, bf), lambda j: (layer, 0, j)),
            pl.BlockSpec((None, d, bf), lambda j: (layer, 0, nf + j)),
            pl.BlockSpec((None, CONV_WIDTH, bf), lambda j: (layer, 0, j)),
            pl.BlockSpec((None, CONV_WIDTH, bf), lambda j: (layer, 0, nf + j)),
            pl.BlockSpec((None, 1, bf), lambda j: (layer, 0, j)),
            pl.BlockSpec((None, 1, bf), lambda j: (layer, 0, nf + j)),
            pl.BlockSpec((None, bf, d), lambda j: (layer, j, 0)),
            pl.BlockSpec((m, d), full),
            pl.BlockSpec((1, d), full),
            pl.BlockSpec((None, nb, k, bf), lambda j: (layer, 0, 0, j)),
            pl.BlockSpec((None, nb, k, bf), lambda j: (layer, 0, 0, nf + j)),
        ],
        out_specs=[
            pl.BlockSpec((m, d), full),
            pl.BlockSpec((m, bf), colg),
            pl.BlockSpec((m, bf), colg),
        ],
        out_shape=[
            jax.ShapeDtypeStruct((m, d), F32),
            jax.ShapeDtypeStruct((m, dff), F32),
            jax.ShapeDtypeStruct((m, dff), F32),
        ],
        scratch_shapes=[pltpu.VMEM((m, d), F32)],
        compiler_params=_cparams(("arbitrary",)),
        name="ffn_sample",
    )(h2d, w_up, w_up, cw, cw, cb, cb, w_down, x2d, g3, state, state)
    u = jnp.concatenate([ug, uv], axis=-1).reshape(nb, ns, 2 * dff)
    return xo, u[:, ns - k:]


def _rope_tables(pos, reps):
    half = D_ROPE // 2
    inv = ROPE_BASE ** (-jnp.arange(half, dtype=F32) / half)
    ang = pos[:, None] * inv[None, :]
    c, s = jnp.cos(ang), jnp.sin(ang)
    z = jnp.zeros((pos.shape[0], LANES - D_ROPE), F32)
    cos_t = jnp.concatenate([c, c, z], axis=1)
    sin_t = jnp.concatenate([s, s, z], axis=1)
    return jnp.tile(cos_t, (reps, 1)), jnp.tile(sin_t, (reps, 1))


def _signed_partner(w):
    half = D_ROPE // 2
    return jnp.concatenate([-w[..., half:], w[..., :half]], axis=-1)


def _stage_mla_weights(w_dq, w_uq, w_dkv, w_uk, w_uv, w_o):
    w_kr = w_dkv[:, KV_LORA:]
    w1 = jnp.concatenate([w_dq, w_dkv[:, :KV_LORA], w_kr, _signed_partner(w_kr)], axis=1).astype(BF16)
    wq = w_uq.reshape(Q_LORA, N_HEADS, D_NOPE + D_ROPE)
    wq_rope = wq[..., D_NOPE:]
    wuq = jnp.concatenate([wq[..., :D_NOPE], wq_rope, _signed_partner(wq_rope)], axis=-1)
    wuq = wuq.reshape(Q_LORA, N_HEADS * HEAD_PAD).astype(BF16)
    wuk = w_uk.reshape(KV_LORA, N_HEADS * D_NOPE).astype(BF16)
    wuv = w_uv.reshape(KV_LORA, N_HEADS * D_V).astype(BF16)
    wukt = jnp.transpose(w_uk, (1, 2, 0)).astype(BF16)
    return w1, wuq, wuk, wuv, wukt, w_o.astype(BF16)


def _block_rows(m, target):
    bm = min(m, target)
    while m % bm:
        bm //= 2
    return bm


def _block_cols(n, target):
    return max(c for c in range(LANES, min(n, target) + 1, LANES) if n % c == 0)


def kernel(x_prompt, x_sample, cache_ckv, cache_krope, state_pool, state_conv, norm_g, mla_w_dq, mla_q_norm, mla_w_uq, mla_w_dkv, mla_kv_norm, mla_w_uk, mla_w_uv, mla_w_o, pool_w, pool_scale, ffn_w_up, ffn_conv_w, ffn_conv_b, ffn_w_down):
    nbp, sp, d = x_prompt.shape
    nbs, ss, _ = x_sample.shape
    past = cache_ckv.shape[2]
    depth = norm_g.shape[0]
    mp, ms = nbp * sp, nbs * ss
    xp = x_prompt.reshape(mp, d)
    xs = x_sample.reshape(ms, d)

    bm_proj = _block_rows(sp, 256)
    bm_post = _block_rows(sp, 512)
    dff = ffn_w_down.shape[1]
    bm_ffn = _block_rows(sp, 1024)
    bf = _block_cols(dff, 512)
    bm_pool = _block_rows(sp, 512)
    blk_attn = _block_rows(sp, 4096)
    bk_attn = _block_rows(blk_attn, 512)
    bms = _block_rows(ms, 256)

    cos_p, sin_p = _rope_tables(jnp.arange(sp, dtype=jnp.int32).astype(F32), 1)
    cos_s, sin_s = _rope_tables((past + jnp.arange(ss, dtype=jnp.int32)).astype(F32), max(bms // ss, 1))

    w_up = ffn_w_up.astype(BF16)
    w_down = ffn_w_down.astype(BF16)
    cb = ffn_conv_b.reshape(depth, 1, -1)
    cwb = jnp.concatenate([ffn_conv_w, cb], axis=1).reshape(depth, CONV_WIDTH + 1, -1, bf).transpose(0, 2, 1, 3)

    outs = {k: [] for k in ("pool_p", "conv_p", "ckv_s", "kr_s", "pool_s", "conv_s")}
    n_mla = mla_w_dq.shape[0]
    kv_stack = None
    pend = None
    for i in range(depth):
        g = norm_g[i].reshape(4, 1, d)
        j = i // 2
        if i % 2 == 0:
            w1, wuq, wuk, wuv, wukt, wo = _stage_mla_weights(
                mla_w_dq[j], mla_w_uq[j], mla_w_dkv[j], mla_w_uk[j], mla_w_uv[j], mla_w_o[j])
            qn = mla_q_norm[j].reshape(1, Q_LORA)
            kvn = mla_kv_norm[j].reshape(1, KV_LORA)
            proj = _mla_project(xp, g[0], w1, qn, kvn, wuq, cos_p, sin_p, wuk, wuv, bm=bm_proj, resid=pend,
                                stack=(j, n_mla, kv_stack))
            q, ckv_all, kr_all, k, v = proj[:5]
            kv_stack = (ckv_all, kr_all)
            if pend is not None:
                xp = proj[5]
            o = _attention(q.reshape(nbp, sp, -1), k.reshape(nbp, sp, -1), v.reshape(nbp, sp, -1),
                           blk=blk_attn, bk=bk_attn)
            xp, hp = _attn_out(o.reshape(mp, -1), wo, xp, g[1], g[2], bm=bm_post)
            qs, ckv_s, kr_s = _mla_project(xs, g[0], w1, qn, kvn, wuq, cos_s, sin_s, bm=bms)
            qlat, qrope = _q_latent(qs, wukt)
            olat = _sample_attention(qlat, qrope, cache_ckv, cache_krope, ckv_s, kr_s, layer=j, ns=ss)
            os_ = _o_from_latent(olat, wuv)
            xs, hs = _attn_out(os_, wo, xs, g[1], g[2], bm=bms)
            outs["ckv_s"].append(ckv_s.reshape(nbs, ss, KV_LORA))
            outs["kr_s"].append(kr_s.reshape(nbs, ss, D_ROPE))
        else:
            wp = pool_w[j].astype(BF16)
            sc = pool_scale[j].reshape(1, d)
            xp, hp, tail_p = _pool_layer(xp, xp, g[0], wp, sc, g[1], g[2], bm=bm_pool,
                                         rows_per_seq=sp, pos0=0, halo_normed=False, resid=pend)
            outs["pool_p"].append(tail_p[:, POOL_HALO - POOL_HIST:])
            hist = jnp.pad(state_pool[j], ((0, 0), (POOL_HALO - POOL_HIST, 0), (0, 0))).reshape(-1, d)
            xs, hs, tail_s = _pool_layer(xs, hist, g[0], wp, sc, g[1], g[2], bm=ss,
                                         rows_per_seq=ss, pos0=past, halo_normed=True)
            outs["pool_s"].append(tail_s[:, POOL_HALO - POOL_HIST:])
        fp, cv_p = _ffn_prompt(hp, w_up, cwb, w_down, layer=i, bm=bm_ffn, rows_per_seq=sp)
        pend = (fp, g[3])
        xs, cv_s = _ffn_sample(hs, w_up, ffn_conv_w, cb, w_down, xs, g[3], state_conv, layer=i, bf=bf, ns=ss)
        outs["conv_p"].append(cv_p)
        outs["conv_s"].append(cv_s)
    xp = _residual_norm(pend[0], xp, pend[1], bm=bm_post)
    st = lambda k: jnp.stack(outs[k])
    return (xp.reshape(nbp, sp, d), xs.reshape(nbs, ss, d),
            kv_stack[0].reshape(n_mla, nbp, sp, KV_LORA), kv_stack[1].reshape(n_mla, nbp, sp, D_ROPE),
            st("pool_p"), st("conv_p"),
            st("ckv_s"), st("kr_s"), st("pool_s"), st("conv_s"))
```

```python
import functools

import numpy as np
import jax
import jax.numpy as jnp
from jax import lax
from jax.experimental import pallas as pl
from jax.experimental.pallas import tpu as pltpu

F32 = jnp.float32
BF16 = jnp.bfloat16

CHUNK = 64
N_HEADS = 16
Q_LORA = 512
KV_LORA = 512
D_NOPE = 128
D_ROPE = 64
D_V = 128
ROPE_BASE = 10000.0
ATTN_SCALE = (D_NOPE + D_ROPE) ** -0.5
Q_SCALE = ATTN_SCALE * float(np.log2(np.e))
POOL_WINDOWS = (2, 4, 8, 16)
POOL_HIST = max(POOL_WINDOWS) - 1
CONV_WIDTH = 3
EPS = 1e-6

LANES = 128
SUBLANES = 8
HEAD_PAD = 2 * LANES
POOL_HALO = 16
VMEM_LIMIT = 56 * 1024 * 1024
NEG_BIG = -1e30


def _cparams(sem):
    return pltpu.CompilerParams(dimension_semantics=sem, vmem_limit_bytes=VMEM_LIMIT)


def _rms(xf, g):
    ms = jnp.mean(xf * xf, axis=-1, keepdims=True)
    return xf * lax.rsqrt(ms + EPS) * g


def _dot(a, b):
    return jnp.dot(a, b, preferred_element_type=F32)


def _dot_nt(a, b):
    return lax.dot_general(a, b, (((1,), (1,)), ((), ())), preferred_element_type=F32)


def _rope_cols(r, cos_t, sin_t):
    return r * cos_t + pltpu.roll(r, D_ROPE, axis=1) * sin_t


def _proj_kernel(x_ref, g_ref, w1_ref, qn_ref, kvn_ref, wuq_ref, cos_ref, sin_ref, *rest,
                 with_kv, with_resid, n_alias):
    n_in = (2 if with_kv else 0) + (2 if with_resid else 0) + n_alias
    ins, outs = list(rest[:n_in]), list(rest[n_in:])
    if with_kv:
        wuk_ref, wuv_ref = ins[:2]
        ins = ins[2:]
    x = x_ref[...]
    if with_resid:
        f_ref, g3_ref = ins[:2]
        xo_ref = outs.pop()
        x = x + _rms(f_ref[...], g3_ref[...])
        xo_ref[...] = x
    if with_kv:
        q_ref, ckv_ref, kr_ref, k_ref, v_ref = outs
    else:
        q_ref, ckv_ref, kr_ref = outs
    h = _rms(x, g_ref[...]).astype(BF16)
    y = _dot(h, w1_ref[...])
    cq = _rms(y[:, :Q_LORA], qn_ref[...]).astype(BF16)
    ckv = _rms(y[:, Q_LORA:Q_LORA + KV_LORA], kvn_ref[...])
    ckv_ref[...] = ckv
    cos_t = cos_ref[...]
    sin_t = sin_ref[...]
    kr = _rope_cols(y[:, Q_LORA + KV_LORA:], cos_t, sin_t)
    kr_ref[...] = kr[:, :D_ROPE]
    q = _dot(cq, wuq_ref[...]) * Q_SCALE
    for hd in range(N_HEADS):
        b0 = hd * HEAD_PAD
        q_ref[:, b0:b0 + LANES] = q[:, b0:b0 + LANES].astype(BF16)
        q_ref[:, b0 + LANES:b0 + HEAD_PAD] = _rope_cols(
            q[:, b0 + LANES:b0 + HEAD_PAD], cos_t, sin_t).astype(BF16)
    if with_kv:
        ckv_b = ckv.astype(BF16)
        kr_b = kr.astype(BF16)
        kn = _dot(ckv_b, wuk_ref[...])
        for hd in range(N_HEADS):
            b0 = hd * HEAD_PAD
            k_ref[:, b0:b0 + LANES] = kn[:, hd * D_NOPE:(hd + 1) * D_NOPE].astype(BF16)
            k_ref[:, b0 + LANES:b0 + HEAD_PAD] = kr_b
        v_ref[...] = _dot(ckv_b, wuv_ref[...]).astype(BF16)


def _mla_project(x2d, g, w1, qn, kvn, wuq, cos_t, sin_t, wuk=None, wuv=None, *, bm, resid=None, stack=None):
    m, d = x2d.shape
    with_kv = wuk is not None
    nt = cos_t.shape[0] // bm
    row = lambda i: (i, 0)
    const = lambda i: (0, 0)
    in_specs = [
        pl.BlockSpec((bm, d), row),
        pl.BlockSpec((1, d), const),
        pl.BlockSpec(w1.shape, const),
        pl.BlockSpec((1, Q_LORA), const),
        pl.BlockSpec((1, KV_LORA), const),
        pl.BlockSpec(wuq.shape, const),
        pl.BlockSpec((bm, LANES), lambda i: (i % nt, 0)),
        pl.BlockSpec((bm, LANES), lambda i: (i % nt, 0)),
    ]
    args = [x2d, g, w1, qn, kvn, wuq, cos_t, sin_t]
    out_shape = [
        jax.ShapeDtypeStruct((m, N_HEADS * HEAD_PAD), BF16),
        jax.ShapeDtypeStruct((m, KV_LORA), F32),
        jax.ShapeDtypeStruct((m, D_ROPE), F32),
    ]
    out_specs = [
        pl.BlockSpec((bm, N_HEADS * HEAD_PAD), row),
        pl.BlockSpec((bm, KV_LORA), row),
        pl.BlockSpec((bm, D_ROPE), row),
    ]
    if with_kv:
        in_specs += [pl.BlockSpec(wuk.shape, const), pl.BlockSpec(wuv.shape, const)]
        args += [wuk, wuv]
        out_shape += [
            jax.ShapeDtypeStruct((m, N_HEADS * HEAD_PAD), BF16),
            jax.ShapeDtypeStruct((m, N_HEADS * D_V), BF16),
        ]
        out_specs += [
            pl.BlockSpec((bm, N_HEADS * HEAD_PAD), row),
            pl.BlockSpec((bm, N_HEADS * D_V), row),
        ]
    if resid is not None:
        in_specs += [pl.BlockSpec((bm, d), row), pl.BlockSpec((1, d), const)]
        args += list(resid)
        out_shape.append(jax.ShapeDtypeStruct((m, d), F32))
        out_specs.append(pl.BlockSpec((bm, d), row))
    aliases = {}
    if stack is not None:
        slot, n_slots, bufs = stack
        for o, width in ((1, KV_LORA), (2, D_ROPE)):
            out_shape[o] = jax.ShapeDtypeStruct((n_slots, m, width), F32)
            out_specs[o] = pl.BlockSpec((None, bm, width), lambda i: (slot, i, 0))
        if bufs is not None:
            aliases = {len(args): 1, len(args) + 1: 2}
            in_specs += [pl.BlockSpec(memory_space=pl.ANY)] * 2
            args += list(bufs)
    return pl.pallas_call(
        functools.partial(_proj_kernel, with_kv=with_kv, with_resid=resid is not None, n_alias=len(aliases)),
        grid=(m // bm,),
        in_specs=in_specs,
        out_specs=out_specs,
        out_shape=out_shape,
        input_output_aliases=aliases,
        compiler_params=_cparams(("arbitrary",)),
        name="mla_project",
    )(*args)


def _attn_kernel(q_ref, k_ref, v_ref, o_ref, m_scr, l_scr, acc_scr, *, blk, bk, sub):
    qi = pl.program_id(2)
    m_scr[...] = jnp.full(m_scr.shape, NEG_BIG, F32)
    l_scr[...] = jnp.zeros(l_scr.shape, F32)
    acc_scr[...] = jnp.zeros(acc_scr.shape, F32)

    def step(start, diag):
        scores = []
        for c in range(blk // sub):
            nk = bk if diag is None else min(bk, (c + 1) * sub - diag)
            if nk <= 0:
                continue
            k = k_ref[0, pl.ds(start, nk), :]
            st = _dot_nt(k, q_ref[0, c * sub:(c + 1) * sub, :])
            if diag is not None and diag + nk > c * sub + CHUNK:
                kc = (lax.broadcasted_iota(jnp.int32, st.shape, 0) + diag) // CHUNK
                qc = (lax.broadcasted_iota(jnp.int32, st.shape, 1) + c * sub) // CHUNK
                st = jnp.where(kc <= qc, st, NEG_BIG)
            scores.append((c, nk, st))
        for c, nk, st in scores:
            cs = slice(c * sub, (c + 1) * sub)
            v = v_ref[0, pl.ds(start, nk), :]
            m_prev = m_scr[:, cs]
            m_new = jnp.maximum(m_prev, jnp.max(st, axis=0, keepdims=True))
            alpha = jnp.exp2(m_prev - m_new)
            p = jnp.exp2(st - m_new)
            l_scr[:, cs] = alpha * l_scr[:, cs] + jnp.sum(p, axis=0, keepdims=True)
            pv = lax.dot_general(v, p.astype(BF16), (((0,), (0,)), ((), ())),
                                 preferred_element_type=F32)
            acc_scr[:, cs] = alpha * acc_scr[:, cs] + pv
            m_scr[:, cs] = m_new

    def body(ki, carry):
        step(pl.multiple_of(ki * bk, bk), None)
        return carry

    per_q = blk // bk
    lax.fori_loop(0, qi * per_q, body, 0)
    for d in range(per_q):
        step(pl.multiple_of(qi * blk + d * bk, bk), d * bk)
    o_ref[0] = jnp.transpose(acc_scr[...] / l_scr[...]).astype(o_ref.dtype)


def _attention(q, k, v, *, blk, bk):
    b, s, _ = q.shape
    return pl.pallas_call(
        functools.partial(_attn_kernel, blk=blk, bk=bk, sub=min(bk, HEAD_PAD)),
        grid=(b, N_HEADS, s // blk),
        in_specs=[
            pl.BlockSpec((1, blk, HEAD_PAD), lambda bi, h, qi: (bi, qi, h)),
            pl.BlockSpec((1, s, HEAD_PAD), lambda bi, h, qi: (bi, 0, h)),
            pl.BlockSpec((1, s, D_V), lambda bi, h, qi: (bi, 0, h)),
        ],
        out_specs=pl.BlockSpec((1, blk, D_V), lambda bi, h, qi: (bi, qi, h)),
        out_shape=jax.ShapeDtypeStruct((b, s, N_HEADS * D_V), BF16),
        scratch_shapes=[
            pltpu.VMEM((1, blk), F32),
            pltpu.VMEM((1, blk), F32),
            pltpu.VMEM((D_V, blk), F32),
        ],
        compiler_params=_cparams(("arbitrary", "arbitrary", "arbitrary")),
        name="prompt_attention",
    )(q, k, v)


def _qlat_kernel(q_ref, wukt_ref, qlat_ref, qrope_ref):
    q = q_ref[...]
    qlat_ref[0] = _dot(q[:, :D_NOPE], wukt_ref[0]).astype(BF16)
    qrope_ref[0] = q[:, LANES:]


def _q_latent(q, wukt):
    m = q.shape[0]
    return pl.pallas_call(
        _qlat_kernel,
        grid=(N_HEADS,),
        in_specs=[
            pl.BlockSpec((m, HEAD_PAD), lambda h: (0, h)),
            pl.BlockSpec((1, D_NOPE, KV_LORA), lambda h: (h, 0, 0)),
        ],
        out_specs=[
            pl.BlockSpec((1, m, KV_LORA), lambda h: (h, 0, 0)),
            pl.BlockSpec((1, m, LANES), lambda h: (h, 0, 0)),
        ],
        out_shape=[
            jax.ShapeDtypeStruct((N_HEADS, m, KV_LORA), BF16),
            jax.ShapeDtypeStruct((N_HEADS, m, LANES), BF16),
        ],
        compiler_params=_cparams(("arbitrary",)),
        name="sample_q_latent",
    )(q, wukt)


def _sattn_kernel(qlat_ref, qrope_ref, cckv_ref, ckr_ref, nckv_ref, nkr_ref, o_ref, *, past, ns):
    rows = N_HEADS * ns
    ql = qlat_ref[...].reshape(rows, KV_LORA)
    qr = qrope_ref[...].reshape(rows, LANES)[:, :D_ROPE]
    cc = cckv_ref[0].astype(BF16)
    ck = ckr_ref[0].astype(BF16)
    nc = nckv_ref[...].astype(BF16)
    nk = nkr_ref[...].astype(BF16)
    s1 = _dot_nt(ql, cc) + _dot_nt(qr, ck)
    s2 = _dot_nt(ql, nc) + _dot_nt(qr, nk)
    qt = lax.broadcasted_iota(jnp.int32, s2.shape, 0) % ns
    kt = lax.broadcasted_iota(jnp.int32, s2.shape, 1)
    s2 = jnp.where((past + kt) // CHUNK <= (past + qt) // CHUNK, s2, NEG_BIG)
    m = jnp.maximum(jnp.max(s1, axis=1, keepdims=True), jnp.max(s2, axis=1, keepdims=True))
    p1 = jnp.exp2(s1 - m)
    p2 = jnp.exp2(s2 - m)
    l = jnp.sum(p1, axis=1, keepdims=True) + jnp.sum(p2, axis=1, keepdims=True)
    o = (_dot(p1.astype(BF16), cc) + _dot(p2.astype(BF16), nc)) / l
    o_ref[...] = o.astype(BF16).reshape(N_HEADS, ns, KV_LORA)


def _sample_attention(qlat, qrope, cache_ckv, cache_kr, new_ckv, new_kr, *, layer, ns):
    _, nb, past, _ = cache_ckv.shape
    m = nb * ns
    return pl.pallas_call(
        functools.partial(_sattn_kernel, past=past, ns=ns),
        grid=(nb,),
        in_specs=[
            pl.BlockSpec((N_HEADS, ns, KV_LORA), lambda b: (0, b, 0)),
            pl.BlockSpec((N_HEADS, ns, LANES), lambda b: (0, b, 0)),
            pl.BlockSpec((None, 1, past, KV_LORA), lambda b: (layer, b, 0, 0)),
            pl.BlockSpec((None, 1, past, D_ROPE), lambda b: (layer, b, 0, 0)),
            pl.BlockSpec((ns, KV_LORA), lambda b: (b, 0)),
            pl.BlockSpec((ns, D_ROPE), lambda b: (b, 0)),
        ],
        out_specs=pl.BlockSpec((N_HEADS, ns, KV_LORA), lambda b: (0, b, 0)),
        out_shape=jax.ShapeDtypeStruct((N_HEADS, m, KV_LORA), BF16),
        compiler_params=_cparams(("arbitrary",)),
        name="sample_attention",
    )(qlat, qrope, cache_ckv, cache_kr, new_ckv, new_kr)


def _ouv_kernel(olat_ref, wuv_ref, o_ref):
    o_ref[...] = _dot(olat_ref[0], wuv_ref[...]).astype(BF16)


def _o_from_latent(olat, wuv):
    m = olat.shape[1]
    return pl.pallas_call(
        _ouv_kernel,
        grid=(N_HEADS,),
        in_specs=[
            pl.BlockSpec((1, m, KV_LORA), lambda h: (h, 0, 0)),
            pl.BlockSpec((KV_LORA, D_V), lambda h: (0, h)),
        ],
        out_specs=pl.BlockSpec((m, D_V), lambda h: (0, h)),
        out_shape=jax.ShapeDtypeStruct((m, N_HEADS * D_V), BF16),
        compiler_params=_cparams(("arbitrary",)),
        name="sample_o_from_latent",
    )(olat, wuv)


def _post_kernel(o_ref, w_ref, x_ref, g1_ref, g2_ref, xo_ref, h2_ref):
    xn = x_ref[...] + _rms(_dot(o_ref[...], w_ref[...]), g1_ref[...])
    xo_ref[...] = xn
    h2_ref[...] = _rms(xn, g2_ref[...]).astype(BF16)


def _attn_out(o2d, w_o, x2d, g1, g2, *, bm):
    m, d = x2d.shape
    row = lambda i: (i, 0)
    const = lambda i: (0, 0)
    return pl.pallas_call(
        _post_kernel,
        grid=(m // bm,),
        in_specs=[
            pl.BlockSpec((bm, o2d.shape[1]), row),
            pl.BlockSpec(w_o.shape, const),
            pl.BlockSpec((bm, d), row),
            pl.BlockSpec((1, d), const),
            pl.BlockSpec((1, d), const),
        ],
        out_specs=[pl.BlockSpec((bm, d), row), pl.BlockSpec((bm, d), row)],
        out_shape=[jax.ShapeDtypeStruct((m, d), F32), jax.ShapeDtypeStruct((m, d), BF16)],
        compiler_params=_cparams(("arbitrary",)),
        name="attn_out_post",
    )(o2d, w_o, x2d, g1, g2)


def _pool_kernel(x_ref, halo_ref, g0_ref, w_ref, sc_ref, g1_ref, g2_ref, *rest,
                 bm, blocks_per_seq, pos0, halo_normed, with_resid):
    i = pl.program_id(0)
    x = x_ref[...]
    halo = halo_ref[...]
    if with_resid:
        f_ref, fhalo_ref, g3_ref, xo_ref, h2_ref, tail_ref = rest
        x = x + _rms(f_ref[...], g3_ref[...])
        halo = halo + _rms(fhalo_ref[...], g3_ref[...])
    else:
        xo_ref, h2_ref, tail_ref = rest
    g0 = g0_ref[...]
    h = _rms(x, g0)
    if not halo_normed:
        halo = _rms(halo, g0)
        halo = jnp.where(i % blocks_per_seq == 0, 0.0, halo)
    cat = jnp.concatenate([halo, h], axis=0)
    t = (i % blocks_per_seq) * bm + lax.broadcasted_iota(jnp.int32, (bm, 1), 0)
    posf = (t + pos0).astype(F32)
    gd = x.shape[1] // len(POOL_WINDOWS)
    ys = []
    for g, w in enumerate(POOL_WINDOWS):
        c = cat[:, g * gd:(g + 1) * gd]
        acc = c
        span = 1
        while span < w:
            acc = acc + pltpu.roll(acc, span, axis=0)
            span *= 2
        inv_cnt = 1.0 / jnp.minimum(jnp.float32(w), posf + 1.0)
        mean = acc[POOL_HALO:] * inv_cnt
        dlt = (mean - h[:, g * gd:(g + 1) * gd]).astype(BF16)
        ys.append(_dot(dlt, w_ref[g]))
    y = jnp.concatenate(ys, axis=1) * sc_ref[...]
    xn = x + _rms(y, g1_ref[...])
    xo_ref[...] = xn
    h2_ref[...] = _rms(xn, g2_ref[...]).astype(BF16)
    tail_ref[0] = h[bm - POOL_HALO:]


def _pool_layer(x2d, halo_src, g0, w_pool, scale, g1, g2, *, bm, rows_per_seq, pos0, halo_normed, resid=None):
    m, d = x2d.shape
    bps = rows_per_seq // bm
    nseq = m // rows_per_seq
    row = lambda i: (i, 0)
    const = lambda i: (0, 0)
    if halo_normed:
        halo_map = lambda i: (i, 0)
    else:
        hb = bm // POOL_HALO
        halo_map = lambda i: (jnp.maximum(i * hb - 1, 0), 0)
    in_specs = [
        pl.BlockSpec((bm, d), row),
        pl.BlockSpec((POOL_HALO, d), halo_map),
        pl.BlockSpec((1, d), const),
        pl.BlockSpec(w_pool.shape, lambda i: (0, 0, 0)),
        pl.BlockSpec((1, d), const),
        pl.BlockSpec((1, d), const),
        pl.BlockSpec((1, d), const),
    ]
    args = [x2d, halo_src, g0, w_pool, scale, g1, g2]
    if resid is not None:
        assert not halo_normed
        in_specs += [pl.BlockSpec((bm, d), row), pl.BlockSpec((POOL_HALO, d), halo_map),
                     pl.BlockSpec((1, d), const)]
        args += [resid[0], resid[0], resid[1]]
    return pl.pallas_call(
        functools.partial(_pool_kernel, bm=bm, blocks_per_seq=bps, pos0=pos0, halo_normed=halo_normed,
                          with_resid=resid is not None),
        grid=(m // bm,),
        in_specs=in_specs,
        out_specs=[
            pl.BlockSpec((bm, d), row),
            pl.BlockSpec((bm, d), row),
            pl.BlockSpec((1, POOL_HALO, d), lambda i: (i // bps, 0, 0)),
        ],
        out_shape=[
            jax.ShapeDtypeStruct((m, d), F32),
            jax.ShapeDtypeStruct((m, d), BF16),
            jax.ShapeDtypeStruct((nseq, POOL_HALO, d), F32),
        ],
        compiler_params=_cparams(("arbitrary",)),
        name="pool_layer",
    )(*args)


def _conv3(u, prev1, prev2, taps, bias):
    return bias + taps[0:1, :] * prev2 + taps[1:2, :] * prev1 + taps[2:3, :] * u


def _ffn_tail(j, nf, act, wd_ref, x_ref, g3_ref, xo_ref, acc_scr):
    part = _dot(act.astype(BF16), wd_ref[...])

    @pl.when(j == 0)
    def _():
        acc_scr[...] = part

    @pl.when(j > 0)
    def _():
        acc_scr[...] += part

    @pl.when(j == nf - 1)
    def _():
        xo_ref[...] = x_ref[...] + _rms(acc_scr[...], g3_ref[...])


def _ffn_prompt_kernel(h_ref, wg_ref, wv_ref, cwb_ref, wd_ref, f_ref, tail_ref, cg_scr, cv_scr, act_scr,
                       *, bm, blocks_per_seq, nf):
    t = pl.program_id(0)
    cur = jnp.minimum(t, pl.num_programs(0) - 2)
    i = cur // nf
    j = cur % nf
    prev = jnp.maximum(t - 1, 0)

    @pl.when(t == 0)
    def _():
        act_scr[...] = jnp.zeros(act_scr.shape, BF16)

    @pl.when(prev % nf == 0)
    def _():
        f_ref[...] = jnp.zeros(f_ref.shape, F32)

    f_ref[...] += _dot(act_scr[(t + 1) % 2], wd_ref[...])

    h = h_ref[...]
    first = i % blocks_per_seq == 0

    def branch(w_ref, col, carry_scr):
        u = _dot(h, w_ref[...])
        p = cwb_ref[col]
        taps, bias = p[:CONV_WIDTH], p[CONV_WIDTH:]
        prev8 = jnp.where(first, 0.0, carry_scr[j])
        c = _conv3(u, pltpu.roll(u, 1, axis=0), pltpu.roll(u, 2, axis=0), taps, bias)
        head = jnp.concatenate([prev8, u[:SUBLANES]], axis=0)
        c_head = _conv3(head, pltpu.roll(head, 1, axis=0), pltpu.roll(head, 2, axis=0),
                        taps, bias)[SUBLANES:]
        c = jnp.concatenate([c_head, c[SUBLANES:]], axis=0)
        last8 = u[bm - SUBLANES:]
        carry_scr[j] = last8
        tail_ref[0, col] = last8
        return c

    gate = branch(wg_ref, j, cg_scr)
    val = branch(wv_ref, nf + j, cv_scr)
    act_scr[t % 2] = (gate * jax.nn.sigmoid(gate) * val).astype(BF16)


def _resid_kernel(f_ref, x_ref, g_ref, xo_ref):
    xo_ref[...] = x_ref[...] + _rms(f_ref[...], g_ref[...])


def _residual_norm(f2d, x2d, g, *, bm):
    m, d = x2d.shape
    row = lambda i: (i, 0)
    return pl.pallas_call(
        _resid_kernel,
        grid=(m // bm,),
        in_specs=[pl.BlockSpec((bm, d), row), pl.BlockSpec((bm, d), row), pl.BlockSpec((1, d), lambda i: (0, 0))],
        out_specs=pl.BlockSpec((bm, d), row),
        out_shape=jax.ShapeDtypeStruct((m, d), F32),
        compiler_params=_cparams(("arbitrary",)),
        name="residual_norm",
    )(f2d, x2d, g)


def _ffn_sample_kernel(h_ref, wg_ref, wv_ref, cwg_ref, cwv_ref, cbg_ref, cbv_ref, wd_ref, x_ref, g3_ref,
                       stg_ref, stv_ref, xo_ref, ug_ref, uv_ref, acc_scr, *, ns, nf):
    j = pl.program_id(0)
    h = h_ref[...]
    m = h.shape[0]
    t = lax.broadcasted_iota(jnp.int32, (m, 1), 0) % ns

    def branch(w_ref, cw_ref, cb_ref, st_ref, u_ref):
        u = _dot(h, w_ref[...])
        u_ref[...] = u
        st = st_ref[...]

        def per_row(k):
            row = st[:, k:k + 1, :]
            return jnp.broadcast_to(row, (m // ns, ns, row.shape[2])).reshape(m, row.shape[2])

        older, newer = per_row(0), per_row(1)
        prev1 = jnp.where(t < 1, newer, pltpu.roll(u, 1, axis=0))
        prev2 = jnp.where(t < 1, older, jnp.where(t < 2, newer, pltpu.roll(u, 2, axis=0)))
        return _conv3(u, prev1, prev2, cw_ref[...], cb_ref[...])

    gate = branch(wg_ref, cwg_ref, cbg_ref, stg_ref, ug_ref)
    val = branch(wv_ref, cwv_ref, cbv_ref, stv_ref, uv_ref)
    act = gate * jax.nn.sigmoid(gate) * val
    _ffn_tail(j, nf, act, wd_ref, x_ref, g3_ref, xo_ref, acc_scr)


def _ffn_prompt(h2d, w_up, cwb, w_down, *, layer, bm, rows_per_seq):
    m, d = h2d.shape
    dff = w_down.shape[1]
    bf = cwb.shape[3]
    nf = dff // bf
    bps = rows_per_seq // bm
    nseq = m // rows_per_seq
    n = (m // bm) * nf
    cur = lambda t: jnp.minimum(t, n - 1)
    prv = lambda t: jnp.maximum(t - 1, 0)
    f, tails = pl.pallas_call(
        functools.partial(_ffn_prompt_kernel, bm=bm, blocks_per_seq=bps, nf=nf),
        grid=(n + 1,),
        in_specs=[
            pl.BlockSpec((bm, d), lambda t: (cur(t) // nf, 0)),
            pl.BlockSpec((None, d, bf), lambda t: (layer, 0, cur(t) % nf)),
            pl.BlockSpec((None, d, bf), lambda t: (layer, 0, nf + cur(t) % nf)),
            pl.BlockSpec((None, 2 * nf, CONV_WIDTH + 1, bf), lambda t: (layer, 0, 0, 0)),
            pl.BlockSpec((None, bf, d), lambda t: (layer, prv(t) % nf, 0)),
        ],
        out_specs=[
            pl.BlockSpec((bm, d), lambda t: (prv(t) // nf, 0)),
            pl.BlockSpec((1, 2 * nf, SUBLANES, bf), lambda t: (cur(t) // nf, 0, 0, 0)),
        ],
        out_shape=[
            jax.ShapeDtypeStruct((m, d), F32),
            jax.ShapeDtypeStruct((m // bm, 2 * nf, SUBLANES, bf), F32),
        ],
        scratch_shapes=[
            pltpu.VMEM((nf, SUBLANES, bf), F32),
            pltpu.VMEM((nf, SUBLANES, bf), F32),
            pltpu.VMEM((2, bm, bf), BF16),
        ],
        compiler_params=_cparams(("arbitrary",)),
        name="ffn_prompt",
    )(h2d, w_up, w_up, cwb, w_down)
    k = CONV_WIDTH - 1
    tails = tails.reshape(nseq, bps, 2 * nf, SUBLANES, bf)[:, bps - 1, :, SUBLANES - k:]
    return f, tails.transpose(0, 2, 1, 3).reshape(nseq, k, 2 * dff)


def _ffn_sample(h2d, w_up, cw, cb, w_down, x2d, g3, state, *, layer, bf, ns):
    m, d = x2d.shape
    dff = w_down.shape[1]
    nf = dff // bf
    nb = m // ns
    k = CONV_WIDTH - 1
    assert state.shape[1:] == (nb, k, 2 * dff) and k == 2
    full = lambda j: (0, 0)
    colg = lambda j: (0, j)
    colv = lambda j: (0, nf + j)
    xo, ug, uv = pl.pallas_call(
        functools.partial(_ffn_sample_kernel, ns=ns, nf=nf),
        grid=(nf,),
        in_specs=[
            pl.BlockSpec((m, d), full),
            pl.BlockSpec((None, d, bf), lambda j: (layer, 0, j)),
            pl.BlockSpec((None, d, bf), lambda j: (layer, 0, nf + j)),
            pl.BlockSpec((None, CONV_WIDTH, bf), lambda j: (layer, 0, j)),
            pl.BlockSpec((None, CONV_WIDTH, bf), lambda j: (layer, 0, nf + j)),
            pl.BlockSpec((None, 1, bf), lambda j: (layer, 0, j)),
            pl.BlockSpec((None, 1, bf), lambda j: (layer, 0, nf + j)),
            pl.BlockSpec((None, bf, d), lambda j: (layer, j, 0)),
            pl.BlockSpec((m, d), full),
            pl.BlockSpec((1, d), full),
            pl.BlockSpec((None, nb, k, bf), lambda j: (layer, 0, 0, j)),
            pl.BlockSpec((None, nb, k, bf), lambda j: (layer, 0, 0, nf + j)),
        ],
        out_specs=[
            pl.BlockSpec((m, d), full),
            pl.BlockSpec((m, bf), colg),
            pl.BlockSpec((m, bf), colg),
        ],
        out_shape=[
            jax.ShapeDtypeStruct((m, d), F32),
            jax.ShapeDtypeStruct((m, dff), F32),
            jax.ShapeDtypeStruct((m, dff), F32),
        ],
        scratch_shapes=[pltpu.VMEM((m, d), F32)],
        compiler_params=_cparams(("arbitrary",)),
        name="ffn_sample",
    )(h2d, w_up, w_up, cw, cw, cb, cb, w_down, x2d, g3, state, state)
    u = jnp.concatenate([ug, uv], axis=-1).reshape(nb, ns, 2 * dff)
    return xo, u[:, ns - k:]


def _rope_tables(pos, reps):
    half = D_ROPE // 2
    inv = ROPE_BASE ** (-jnp.arange(half, dtype=F32) / half)
    ang = pos[:, None] * inv[None, :]
    c, s = jnp.cos(ang), jnp.sin(ang)
    z = jnp.zeros((pos.shape[0], LANES - D_ROPE), F32)
    cos_t = jnp.concatenate([c, c, z], axis=1)
    sin_t = jnp.concatenate([s, s, z], axis=1)
    return jnp.tile(cos_t, (reps, 1)), jnp.tile(sin_t, (reps, 1))


def _signed_partner(w):
    half = D_ROPE // 2
    return jnp.concatenate([-w[..., half:], w[..., :half]], axis=-1)


def _stage_mla_weights(w_dq, w_uq, w_dkv, w_uk, w_uv, w_o):
    w_kr = w_dkv[:, KV_LORA:]
    w1 = jnp.concatenate([w_dq, w_dkv[:, :KV_LORA], w_kr, _signed_partner(w_kr)], axis=1).astype(BF16)
    wq = w_uq.reshape(Q_LORA, N_HEADS, D_NOPE + D_ROPE)
    wq_rope = wq[..., D_NOPE:]
    wuq = jnp.concatenate([wq[..., :D_NOPE], wq_rope, _signed_partner(wq_rope)], axis=-1)
    wuq = wuq.reshape(Q_LORA, N_HEADS * HEAD_PAD).astype(BF16)
    wuk = w_uk.reshape(KV_LORA, N_HEADS * D_NOPE).astype(BF16)
    wuv = w_uv.reshape(KV_LORA, N_HEADS * D_V).astype(BF16)
    wukt = jnp.transpose(w_uk, (1, 2, 0)).astype(BF16)
    return w1, wuq, wuk, wuv, wukt, w_o.astype(BF16)


def _block_rows(m, target):
    bm = min(m, target)
    while m % bm:
        bm //= 2
    return bm


def _block_cols(n, target):
    return max(c for c in range(LANES, min(n, target) + 1, LANES) if n % c == 0)


def kernel(x_prompt, x_sample, cache_ckv, cache_krope, state_pool, state_conv, norm_g, mla_w_dq, mla_q_norm, mla_w_uq, mla_w_dkv, mla_kv_norm, mla_w_uk, mla_w_uv, mla_w_o, pool_w, pool_scale, ffn_w_up, ffn_conv_w, ffn_conv_b, ffn_w_down):
    nbp, sp, d = x_prompt.shape
    nbs, ss, _ = x_sample.shape
    past = cache_ckv.shape[2]
    depth = norm_g.shape[0]
    mp, ms = nbp * sp, nbs * ss
    assert d % (len(POOL_WINDOWS) * LANES) == 0 and sp % (2 * LANES) == 0 and sp % CHUNK == 0
    assert ss >= POOL_HALO and ss % POOL_HALO == 0
    assert mla_w_uq.shape[-1] == N_HEADS * (D_NOPE + D_ROPE) and mla_w_dkv.shape[-1] == KV_LORA + D_ROPE
    assert ffn_conv_w.shape[1] == CONV_WIDTH and state_pool.shape[2] == POOL_HIST
    xp = x_prompt.reshape(mp, d)
    xs = x_sample.reshape(ms, d)

    bm_proj = _block_rows(sp, 256)
    bm_post = _block_rows(sp, 512)
    dff = ffn_w_down.shape[1]
    bm_ffn = _block_rows(sp, 1024)
    bf = _block_cols(dff, 512)
    bm_pool = _block_rows(sp, 512)
    blk_attn = _block_rows(sp, 4096)
    bk_attn = _block_rows(blk_attn, 1024)
    bms = _block_rows(ms, 256)

    cos_p, sin_p = _rope_tables(jnp.arange(sp, dtype=jnp.int32).astype(F32), 1)
    cos_s, sin_s = _rope_tables((past + jnp.arange(ss, dtype=jnp.int32)).astype(F32), max(bms // ss, 1))

    w_up = ffn_w_up.astype(BF16)
    w_down = ffn_w_down.astype(BF16)
    cb = ffn_conv_b.reshape(depth, 1, -1)
    cwb = jnp.concatenate([ffn_conv_w, cb], axis=1).reshape(depth, CONV_WIDTH + 1, -1, bf).transpose(0, 2, 1, 3)

    outs = {k: [] for k in ("pool_p", "conv_p", "ckv_s", "kr_s", "pool_s", "conv_s")}
    n_mla = mla_w_dq.shape[0]
    kv_stack = None
    pend = None
    for i in range(depth):
        g = norm_g[i].reshape(4, 1, d)
        j = i // 2
        if i % 2 == 0:
            w1, wuq, wuk, wuv, wukt, wo = _stage_mla_weights(
                mla_w_dq[j], mla_w_uq[j], mla_w_dkv[j], mla_w_uk[j], mla_w_uv[j], mla_w_o[j])
            qn = mla_q_norm[j].reshape(1, Q_LORA)
            kvn = mla_kv_norm[j].reshape(1, KV_LORA)
            proj = _mla_project(xp, g[0], w1, qn, kvn, wuq, cos_p, sin_p, wuk, wuv, bm=bm_proj, resid=pend,
                                stack=(j, n_mla, kv_stack))
            q, ckv_all, kr_all, k, v = proj[:5]
            kv_stack = (ckv_all, kr_all)
            if pend is not None:
                xp = proj[5]
            o = _attention(q.reshape(nbp, sp, -1), k.reshape(nbp, sp, -1), v.reshape(nbp, sp, -1),
                           blk=blk_attn, bk=bk_attn)
            xp, hp = _attn_out(o.reshape(mp, -1), wo, xp, g[1], g[2], bm=bm_post)
            qs, ckv_s, kr_s = _mla_project(xs, g[0], w1, qn, kvn, wuq, cos_s, sin_s, bm=bms)
            qlat, qrope = _q_latent(qs, wukt)
            olat = _sample_attention(qlat, qrope, cache_ckv, cache_krope, ckv_s, kr_s, layer=j, ns=ss)
            os_ = _o_from_latent(olat, wuv)
            xs, hs = _attn_out(os_, wo, xs, g[1], g[2], bm=bms)
            outs["ckv_s"].append(ckv_s.reshape(nbs, ss, KV_LORA))
            outs["kr_s"].append(kr_s.reshape(nbs, ss, D_ROPE))
        else:
            wp = pool_w[j].astype(BF16)
            sc = pool_scale[j].reshape(1, d)
            xp, hp, tail_p = _pool_layer(xp, xp, g[0], wp, sc, g[1], g[2], bm=bm_pool,
                                         rows_per_seq=sp, pos0=0, halo_normed=False, resid=pend)
            outs["pool_p"].append(tail_p[:, POOL_HALO - POOL_HIST:])
            hist = jnp.pad(state_pool[j], ((0, 0), (POOL_HALO - POOL_HIST, 0), (0, 0))).reshape(-1, d)
            xs, hs, tail_s = _pool_layer(xs, hist, g[0], wp, sc, g[1], g[2], bm=ss,
                                         rows_per_seq=ss, pos0=past, halo_normed=True)
            outs["pool_s"].append(tail_s[:, POOL_HALO - POOL_HIST:])
        fp, cv_p = _ffn_prompt(hp, w_up, cwb, w_down, layer=i, bm=bm_ffn, rows_per_seq=sp)
        pend = (fp, g[3])
        xs, cv_s = _ffn_sample(hs, w_up, ffn_conv_w, cb, w_down, xs, g[3], state_conv, layer=i, bf=bf, ns=ss)
        outs["conv_p"].append(cv_p)
        outs["conv_s"].append(cv_s)
    xp = _residual_norm(pend[0], xp, pend[1], bm=bm_post)
    st = lambda k: jnp.stack(outs[k])
    return (xp.reshape(nbp, sp, d), xs.reshape(nbs, ss, d),
            kv_stack[0].reshape(n_mla, nbp, sp, KV_LORA), kv_stack[1].reshape(n_mla, nbp, sp, D_ROPE),
            st("pool_p"), st("conv_p"),
            st("ckv_s"), st("kr_s"), st("pool_s"), st("conv_s"))
```

```python
import functools

import numpy as np
import jax
import jax.numpy as jnp
from jax import lax
from jax.experimental import pallas as pl
from jax.experimental.pallas import tpu as pltpu

F32 = jnp.float32
BF16 = jnp.bfloat16

CHUNK = 64
N_HEADS = 16
Q_LORA = 512
KV_LORA = 512
D_NOPE = 128
D_ROPE = 64
D_V = 128
ROPE_BASE = 10000.0
ATTN_SCALE = (D_NOPE + D_ROPE) ** -0.5
Q_SCALE = ATTN_SCALE * float(np.log2(np.e))
POOL_WINDOWS = (2, 4, 8, 16)
POOL_HIST = max(POOL_WINDOWS) - 1
CONV_WIDTH = 3
EPS = 1e-6

LANES = 128
SUBLANES = 8
HEAD_PAD = 2 * LANES
POOL_HALO = 16
VMEM_LIMIT = 56 * 1024 * 1024
NEG_BIG = -1e30


def _cparams(sem):
    return pltpu.CompilerParams(dimension_semantics=sem, vmem_limit_bytes=VMEM_LIMIT)


def _rms(xf, g):
    ms = jnp.mean(xf * xf, axis=-1, keepdims=True)
    return xf * lax.rsqrt(ms + EPS) * g


def _dot(a, b):
    return jnp.dot(a, b, preferred_element_type=F32)


def _dot_nt(a, b):
    return lax.dot_general(a, b, (((1,), (1,)), ((), ())), preferred_element_type=F32)


def _rope_cols(r, cos_t, sin_t):
    return r * cos_t + pltpu.roll(r, D_ROPE, axis=1) * sin_t


def _proj_kernel(x_ref, g_ref, w1_ref, qn_ref, kvn_ref, wuq_ref, cos_ref, sin_ref, *rest,
                 with_kv, with_resid):
    n_in = (2 if with_kv else 0) + (2 if with_resid else 0)
    ins, outs = list(rest[:n_in]), list(rest[n_in:])
    if with_kv:
        wuk_ref, wuv_ref = ins[:2]
        ins = ins[2:]
    x = x_ref[...]
    if with_resid:
        f_ref, g3_ref = ins[:2]
        xo_ref = outs.pop()
        x = x + _rms(f_ref[...], g3_ref[...])
        xo_ref[...] = x
    if with_kv:
        q_ref, ckv_ref, kr_ref, k_ref, v_ref = outs
    else:
        q_ref, ckv_ref, kr_ref = outs
    h = _rms(x, g_ref[...]).astype(BF16)
    y = _dot(h, w1_ref[...])
    cq = _rms(y[:, :Q_LORA], qn_ref[...]).astype(BF16)
    ckv = _rms(y[:, Q_LORA:Q_LORA + KV_LORA], kvn_ref[...])
    ckv_ref[...] = ckv
    cos_t = cos_ref[...]
    sin_t = sin_ref[...]
    kr = _rope_cols(y[:, Q_LORA + KV_LORA:], cos_t, sin_t)
    kr_ref[...] = kr[:, :D_ROPE]
    q = _dot(cq, wuq_ref[...]) * Q_SCALE
    for hd in range(N_HEADS):
        b0 = hd * HEAD_PAD
        q_ref[:, b0:b0 + LANES] = q[:, b0:b0 + LANES].astype(BF16)
        q_ref[:, b0 + LANES:b0 + HEAD_PAD] = _rope_cols(
            q[:, b0 + LANES:b0 + HEAD_PAD], cos_t, sin_t).astype(BF16)
    if with_kv:
        ckv_b = ckv.astype(BF16)
        kr_b = kr.astype(BF16)
        kn = _dot(ckv_b, wuk_ref[...])
        for hd in range(N_HEADS):
            b0 = hd * HEAD_PAD
            k_ref[:, b0:b0 + LANES] = kn[:, hd * D_NOPE:(hd + 1) * D_NOPE].astype(BF16)
            k_ref[:, b0 + LANES:b0 + HEAD_PAD] = kr_b
        v_ref[...] = _dot(ckv_b, wuv_ref[...]).astype(BF16)


def _mla_project(x2d, g, w1, qn, kvn, wuq, cos_t, sin_t, wuk=None, wuv=None, *, bm, resid=None):
    m, d = x2d.shape
    with_kv = wuk is not None
    nt = cos_t.shape[0] // bm
    row = lambda i: (i, 0)
    const = lambda i: (0, 0)
    in_specs = [
        pl.BlockSpec((bm, d), row),
        pl.BlockSpec((1, d), const),
        pl.BlockSpec(w1.shape, const),
        pl.BlockSpec((1, Q_LORA), const),
        pl.BlockSpec((1, KV_LORA), const),
        pl.BlockSpec(wuq.shape, const),
        pl.BlockSpec((bm, LANES), lambda i: (i % nt, 0)),
        pl.BlockSpec((bm, LANES), lambda i: (i % nt, 0)),
    ]
    args = [x2d, g, w1, qn, kvn, wuq, cos_t, sin_t]
    out_shape = [
        jax.ShapeDtypeStruct((m, N_HEADS * HEAD_PAD), BF16),
        jax.ShapeDtypeStruct((m, KV_LORA), F32),
        jax.ShapeDtypeStruct((m, D_ROPE), F32),
    ]
    out_specs = [
        pl.BlockSpec((bm, N_HEADS * HEAD_PAD), row),
        pl.BlockSpec((bm, KV_LORA), row),
        pl.BlockSpec((bm, D_ROPE), row),
    ]
    if with_kv:
        in_specs += [pl.BlockSpec(wuk.shape, const), pl.BlockSpec(wuv.shape, const)]
        args += [wuk, wuv]
        out_shape += [
            jax.ShapeDtypeStruct((m, N_HEADS * HEAD_PAD), BF16),
            jax.ShapeDtypeStruct((m, N_HEADS * D_V), BF16),
        ]
        out_specs += [
            pl.BlockSpec((bm, N_HEADS * HEAD_PAD), row),
            pl.BlockSpec((bm, N_HEADS * D_V), row),
        ]
    if resid is not None:
        in_specs += [pl.BlockSpec((bm, d), row), pl.BlockSpec((1, d), const)]
        args += list(resid)
        out_shape.append(jax.ShapeDtypeStruct((m, d), F32))
        out_specs.append(pl.BlockSpec((bm, d), row))
    return pl.pallas_call(
        functools.partial(_proj_kernel, with_kv=with_kv, with_resid=resid is not None),
        grid=(m // bm,),
        in_specs=in_specs,
        out_specs=out_specs,
        out_shape=out_shape,
        compiler_params=_cparams(("arbitrary",)),
        name="mla_project",
    )(*args)


def _attn_kernel(q_ref, k_ref, v_ref, o_ref, m_scr, l_scr, acc_scr, *, blk, bk, sub):
    qi = pl.program_id(2)
    m_scr[...] = jnp.full(m_scr.shape, NEG_BIG, F32)
    l_scr[...] = jnp.zeros(l_scr.shape, F32)
    acc_scr[...] = jnp.zeros(acc_scr.shape, F32)

    def step(start, diag):
        scores = []
        for c in range(blk // sub):
            nk = bk if diag is None else min(bk, (c + 1) * sub - diag)
            if nk <= 0:
                continue
            k = k_ref[0, pl.ds(start, nk), :]
            st = _dot_nt(k, q_ref[0, c * sub:(c + 1) * sub, :])
            if diag is not None and diag + nk > c * sub + CHUNK:
                kc = (lax.broadcasted_iota(jnp.int32, st.shape, 0) + diag) // CHUNK
                qc = (lax.broadcasted_iota(jnp.int32, st.shape, 1) + c * sub) // CHUNK
                st = jnp.where(kc <= qc, st, NEG_BIG)
            scores.append((c, nk, st))
        for c, nk, st in scores:
            cs = slice(c * sub, (c + 1) * sub)
            v = v_ref[0, pl.ds(start, nk), :]
            m_prev = m_scr[:, cs]
            m_new = jnp.maximum(m_prev, jnp.max(st, axis=0, keepdims=True))
            alpha = jnp.exp2(m_prev - m_new)
            p = jnp.exp2(st - m_new)
            l_scr[:, cs] = alpha * l_scr[:, cs] + jnp.sum(p, axis=0, keepdims=True)
            pv = lax.dot_general(v, p.astype(BF16), (((0,), (0,)), ((), ())),
                                 preferred_element_type=F32)
            acc_scr[:, cs] = alpha * acc_scr[:, cs] + pv
            m_scr[:, cs] = m_new

    def body(ki, carry):
        step(pl.multiple_of(ki * bk, bk), None)
        return carry

    per_q = blk // bk
    lax.fori_loop(0, qi * per_q, body, 0)
    for d in range(per_q):
        step(pl.multiple_of(qi * blk + d * bk, bk), d * bk)
    o_ref[0] = jnp.transpose(acc_scr[...] / l_scr[...]).astype(o_ref.dtype)


def _attention(q, k, v, *, blk, bk):
    b, s, _ = q.shape
    return pl.pallas_call(
        functools.partial(_attn_kernel, blk=blk, bk=bk, sub=min(bk, HEAD_PAD)),
        grid=(b, N_HEADS, s // blk),
        in_specs=[
            pl.BlockSpec((1, blk, HEAD_PAD), lambda bi, h, qi: (bi, qi, h)),
            pl.BlockSpec((1, s, HEAD_PAD), lambda bi, h, qi: (bi, 0, h)),
            pl.BlockSpec((1, s, D_V), lambda bi, h, qi: (bi, 0, h)),
        ],
        out_specs=pl.BlockSpec((1, blk, D_V), lambda bi, h, qi: (bi, qi, h)),
        out_shape=jax.ShapeDtypeStruct((b, s, N_HEADS * D_V), BF16),
        scratch_shapes=[
            pltpu.VMEM((1, blk), F32),
            pltpu.VMEM((1, blk), F32),
            pltpu.VMEM((D_V, blk), F32),
        ],
        compiler_params=_cparams(("arbitrary", "arbitrary", "arbitrary")),
        name="prompt_attention",
    )(q, k, v)


def _qlat_kernel(q_ref, wukt_ref, qlat_ref, qrope_ref):
    q = q_ref[...]
    qlat_ref[0] = _dot(q[:, :D_NOPE], wukt_ref[0]).astype(BF16)
    qrope_ref[0] = q[:, LANES:]


def _q_latent(q, wukt):
    m = q.shape[0]
    return pl.pallas_call(
        _qlat_kernel,
        grid=(N_HEADS,),
        in_specs=[
            pl.BlockSpec((m, HEAD_PAD), lambda h: (0, h)),
            pl.BlockSpec((1, D_NOPE, KV_LORA), lambda h: (h, 0, 0)),
        ],
        out_specs=[
            pl.BlockSpec((1, m, KV_LORA), lambda h: (h, 0, 0)),
            pl.BlockSpec((1, m, LANES), lambda h: (h, 0, 0)),
        ],
        out_shape=[
            jax.ShapeDtypeStruct((N_HEADS, m, KV_LORA), BF16),
            jax.ShapeDtypeStruct((N_HEADS, m, LANES), BF16),
        ],
        compiler_params=_cparams(("arbitrary",)),
        name="sample_q_latent",
    )(q, wukt)


def _sattn_kernel(qlat_ref, qrope_ref, cckv_ref, ckr_ref, nckv_ref, nkr_ref, o_ref, *, past, ns):
    rows = N_HEADS * ns
    ql = qlat_ref[...].reshape(rows, KV_LORA)
    qr = qrope_ref[...].reshape(rows, LANES)[:, :D_ROPE]
    cc = cckv_ref[0].astype(BF16)
    ck = ckr_ref[0].astype(BF16)
    nc = nckv_ref[...].astype(BF16)
    nk = nkr_ref[...].astype(BF16)
    s1 = _dot_nt(ql, cc) + _dot_nt(qr, ck)
    s2 = _dot_nt(ql, nc) + _dot_nt(qr, nk)
    qt = lax.broadcasted_iota(jnp.int32, s2.shape, 0) % ns
    kt = lax.broadcasted_iota(jnp.int32, s2.shape, 1)
    s2 = jnp.where((past + kt) // CHUNK <= (past + qt) // CHUNK, s2, NEG_BIG)
    m = jnp.maximum(jnp.max(s1, axis=1, keepdims=True), jnp.max(s2, axis=1, keepdims=True))
    p1 = jnp.exp2(s1 - m)
    p2 = jnp.exp2(s2 - m)
    l = jnp.sum(p1, axis=1, keepdims=True) + jnp.sum(p2, axis=1, keepdims=True)
    o = (_dot(p1.astype(BF16), cc) + _dot(p2.astype(BF16), nc)) / l
    o_ref[...] = o.astype(BF16).reshape(N_HEADS, ns, KV_LORA)


def _sample_attention(qlat, qrope, cache_ckv, cache_kr, new_ckv, new_kr, *, layer, ns):
    _, nb, past, _ = cache_ckv.shape
    m = nb * ns
    return pl.pallas_call(
        functools.partial(_sattn_kernel, past=past, ns=ns),
        grid=(nb,),
        in_specs=[
            pl.BlockSpec((N_HEADS, ns, KV_LORA), lambda b: (0, b, 0)),
            pl.BlockSpec((N_HEADS, ns, LANES), lambda b: (0, b, 0)),
            pl.BlockSpec((None, 1, past, KV_LORA), lambda b: (layer, b, 0, 0)),
            pl.BlockSpec((None, 1, past, D_ROPE), lambda b: (layer, b, 0, 0)),
            pl.BlockSpec((ns, KV_LORA), lambda b: (b, 0)),
            pl.BlockSpec((ns, D_ROPE), lambda b: (b, 0)),
        ],
        out_specs=pl.BlockSpec((N_HEADS, ns, KV_LORA), lambda b: (0, b, 0)),
        out_shape=jax.ShapeDtypeStruct((N_HEADS, m, KV_LORA), BF16),
        compiler_params=_cparams(("arbitrary",)),
        name="sample_attention",
    )(qlat, qrope, cache_ckv, cache_kr, new_ckv, new_kr)


def _ouv_kernel(olat_ref, wuv_ref, o_ref):
    o_ref[...] = _dot(olat_ref[0], wuv_ref[...]).astype(BF16)


def _o_from_latent(olat, wuv):
    m = olat.shape[1]
    return pl.pallas_call(
        _ouv_kernel,
        grid=(N_HEADS,),
        in_specs=[
            pl.BlockSpec((1, m, KV_LORA), lambda h: (h, 0, 0)),
            pl.BlockSpec((KV_LORA, D_V), lambda h: (0, h)),
        ],
        out_specs=pl.BlockSpec((m, D_V), lambda h: (0, h)),
        out_shape=jax.ShapeDtypeStruct((m, N_HEADS * D_V), BF16),
        compiler_params=_cparams(("arbitrary",)),
        name="sample_o_from_latent",
    )(olat, wuv)


def _post_kernel(o_ref, w_ref, x_ref, g1_ref, g2_ref, xo_ref, h2_ref):
    xn = x_ref[...] + _rms(_dot(o_ref[...], w_ref[...]), g1_ref[...])
    xo_ref[...] = xn
    h2_ref[...] = _rms(xn, g2_ref[...]).astype(BF16)


def _attn_out(o2d, w_o, x2d, g1, g2, *, bm):
    m, d = x2d.shape
    row = lambda i: (i, 0)
    const = lambda i: (0, 0)
    return pl.pallas_call(
        _post_kernel,
        grid=(m // bm,),
        in_specs=[
            pl.BlockSpec((bm, o2d.shape[1]), row),
            pl.BlockSpec(w_o.shape, const),
            pl.BlockSpec((bm, d), row),
            pl.BlockSpec((1, d), const),
            pl.BlockSpec((1, d), const),
        ],
        out_specs=[pl.BlockSpec((bm, d), row), pl.BlockSpec((bm, d), row)],
        out_shape=[jax.ShapeDtypeStruct((m, d), F32), jax.ShapeDtypeStruct((m, d), BF16)],
        compiler_params=_cparams(("arbitrary",)),
        name="attn_out_post",
    )(o2d, w_o, x2d, g1, g2)


def _pool_kernel(x_ref, halo_ref, g0_ref, w_ref, sc_ref, g1_ref, g2_ref, *rest,
                 bm, blocks_per_seq, pos0, halo_normed, with_resid):
    i = pl.program_id(0)
    x = x_ref[...]
    halo = halo_ref[...]
    if with_resid:
        f_ref, fhalo_ref, g3_ref, xo_ref, h2_ref, tail_ref = rest
        x = x + _rms(f_ref[...], g3_ref[...])
        halo = halo + _rms(fhalo_ref[...], g3_ref[...])
    else:
        xo_ref, h2_ref, tail_ref = rest
    g0 = g0_ref[...]
    h = _rms(x, g0)
    if not halo_normed:
        halo = _rms(halo, g0)
        halo = jnp.where(i % blocks_per_seq == 0, 0.0, halo)
    cat = jnp.concatenate([halo, h], axis=0)
    t = (i % blocks_per_seq) * bm + lax.broadcasted_iota(jnp.int32, (bm, 1), 0)
    posf = (t + pos0).astype(F32)
    gd = x.shape[1] // len(POOL_WINDOWS)
    ys = []
    for g, w in enumerate(POOL_WINDOWS):
        c = cat[:, g * gd:(g + 1) * gd]
        acc = c
        span = 1
        while span < w:
            acc = acc + pltpu.roll(acc, span, axis=0)
            span *= 2
        cnt = jnp.minimum(jnp.float32(w), posf + 1.0)
        mean = acc[POOL_HALO:] / cnt
        dlt = (mean - h[:, g * gd:(g + 1) * gd]).astype(BF16)
        ys.append(_dot(dlt, w_ref[g]))
    y = jnp.concatenate(ys, axis=1) * sc_ref[...]
    xn = x + _rms(y, g1_ref[...])
    xo_ref[...] = xn
    h2_ref[...] = _rms(xn, g2_ref[...]).astype(BF16)
    tail_ref[0] = h[bm - POOL_HALO:]


def _pool_layer(x2d, halo_src, g0, w_pool, scale, g1, g2, *, bm, rows_per_seq, pos0, halo_normed, resid=None):
    m, d = x2d.shape
    bps = rows_per_seq // bm
    nseq = m // rows_per_seq
    row = lambda i: (i, 0)
    const = lambda i: (0, 0)
    if halo_normed:
        halo_map = lambda i: (i, 0)
    else:
        hb = bm // POOL_HALO
        halo_map = lambda i: (jnp.maximum(i * hb - 1, 0), 0)
    in_specs = [
        pl.BlockSpec((bm, d), row),
        pl.BlockSpec((POOL_HALO, d), halo_map),
        pl.BlockSpec((1, d), const),
        pl.BlockSpec(w_pool.shape, lambda i: (0, 0, 0)),
        pl.BlockSpec((1, d), const),
        pl.BlockSpec((1, d), const),
        pl.BlockSpec((1, d), const),
    ]
    args = [x2d, halo_src, g0, w_pool, scale, g1, g2]
    if resid is not None:
        assert not halo_normed
        in_specs += [pl.BlockSpec((bm, d), row), pl.BlockSpec((POOL_HALO, d), halo_map),
                     pl.BlockSpec((1, d), const)]
        args += [resid[0], resid[0], resid[1]]
    return pl.pallas_call(
        functools.partial(_pool_kernel, bm=bm, blocks_per_seq=bps, pos0=pos0, halo_normed=halo_normed,
                          with_resid=resid is not None),
        grid=(m // bm,),
        in_specs=in_specs,
        out_specs=[
            pl.BlockSpec((bm, d), row),
            pl.BlockSpec((bm, d), row),
            pl.BlockSpec((1, POOL_HALO, d), lambda i: (i // bps, 0, 0)),
        ],
        out_shape=[
            jax.ShapeDtypeStruct((m, d), F32),
            jax.ShapeDtypeStruct((m, d), BF16),
            jax.ShapeDtypeStruct((nseq, POOL_HALO, d), F32),
        ],
        compiler_params=_cparams(("arbitrary",)),
        name="pool_layer",
    )(*args)


def _conv3(u, prev1, prev2, taps, bias):
    return bias + taps[0:1, :] * prev2 + taps[1:2, :] * prev1 + taps[2:3, :] * u


def _ffn_tail(j, nf, act, wd_ref, x_ref, g3_ref, xo_ref, acc_scr):
    part = _dot(act.astype(BF16), wd_ref[...])

    @pl.when(j == 0)
    def _():
        acc_scr[...] = part

    @pl.when(j > 0)
    def _():
        acc_scr[...] += part

    @pl.when(j == nf - 1)
    def _():
        xo_ref[...] = x_ref[...] + _rms(acc_scr[...], g3_ref[...])


def _ffn_prompt_kernel(h_ref, wg_ref, wv_ref, cwb_ref, wd_ref, f_ref, tail_ref, cg_scr, cv_scr, act_scr,
                       *, bm, blocks_per_seq, nf):
    t = pl.program_id(0)
    cur = jnp.minimum(t, pl.num_programs(0) - 2)
    i = cur // nf
    j = cur % nf
    prev = jnp.maximum(t - 1, 0)

    @pl.when(t == 0)
    def _():
        act_scr[...] = jnp.zeros(act_scr.shape, BF16)

    @pl.when(prev % nf == 0)
    def _():
        f_ref[...] = jnp.zeros(f_ref.shape, F32)

    f_ref[...] += _dot(act_scr[(t + 1) % 2], wd_ref[...])

    h = h_ref[...]
    first = i % blocks_per_seq == 0

    def branch(w_ref, col, carry_scr):
        u = _dot(h, w_ref[...])
        p = cwb_ref[col]
        taps, bias = p[:CONV_WIDTH], p[CONV_WIDTH:]
        prev8 = jnp.where(first, 0.0, carry_scr[j])
        c = _conv3(u, pltpu.roll(u, 1, axis=0), pltpu.roll(u, 2, axis=0), taps, bias)
        head = jnp.concatenate([prev8, u[:SUBLANES]], axis=0)
        c_head = _conv3(head, pltpu.roll(head, 1, axis=0), pltpu.roll(head, 2, axis=0),
                        taps, bias)[SUBLANES:]
        c = jnp.concatenate([c_head, c[SUBLANES:]], axis=0)
        last8 = u[bm - SUBLANES:]
        carry_scr[j] = last8
        tail_ref[0, col] = last8
        return c

    gate = branch(wg_ref, j, cg_scr)
    val = branch(wv_ref, nf + j, cv_scr)
    act_scr[t % 2] = (gate * jax.nn.sigmoid(gate) * val).astype(BF16)


def _resid_kernel(f_ref, x_ref, g_ref, xo_ref):
    xo_ref[...] = x_ref[...] + _rms(f_ref[...], g_ref[...])


def _residual_norm(f2d, x2d, g, *, bm):
    m, d = x2d.shape
    row = lambda i: (i, 0)
    return pl.pallas_call(
        _resid_kernel,
        grid=(m // bm,),
        in_specs=[pl.BlockSpec((bm, d), row), pl.BlockSpec((bm, d), row), pl.BlockSpec((1, d), lambda i: (0, 0))],
        out_specs=pl.BlockSpec((bm, d), row),
        out_shape=jax.ShapeDtypeStruct((m, d), F32),
        compiler_params=_cparams(("arbitrary",)),
        name="residual_norm",
    )(f2d, x2d, g)


def _ffn_sample_kernel(h_ref, wg_ref, wv_ref, cwg_ref, cwv_ref, cbg_ref, cbv_ref, wd_ref, x_ref, g3_ref,
                       stg_ref, stv_ref, xo_ref, ug_ref, uv_ref, acc_scr, *, ns, nf):
    j = pl.program_id(0)
    h = h_ref[...]
    m = h.shape[0]
    t = lax.broadcasted_iota(jnp.int32, (m, 1), 0) % ns

    def branch(w_ref, cw_ref, cb_ref, st_ref, u_ref):
        u = _dot(h, w_ref[...])
        u_ref[...] = u
        st = st_ref[...]

        def per_row(k):
            row = st[:, k:k + 1, :]
            return jnp.broadcast_to(row, (m // ns, ns, row.shape[2])).reshape(m, row.shape[2])

        older, newer = per_row(0), per_row(1)
        prev1 = jnp.where(t < 1, newer, pltpu.roll(u, 1, axis=0))
        prev2 = jnp.where(t < 1, older, jnp.where(t < 2, newer, pltpu.roll(u, 2, axis=0)))
        return _conv3(u, prev1, prev2, cw_ref[...], cb_ref[...])

    gate = branch(wg_ref, cwg_ref, cbg_ref, stg_ref, ug_ref)
    val = branch(wv_ref, cwv_ref, cbv_ref, stv_ref, uv_ref)
    act = gate * jax.nn.sigmoid(gate) * val
    _ffn_tail(j, nf, act, wd_ref, x_ref, g3_ref, xo_ref, acc_scr)


def _ffn_prompt(h2d, w_up, cwb, w_down, *, layer, bm, rows_per_seq):
    m, d = h2d.shape
    dff = w_down.shape[1]
    bf = cwb.shape[3]
    nf = dff // bf
    bps = rows_per_seq // bm
    nseq = m // rows_per_seq
    n = (m // bm) * nf
    cur = lambda t: jnp.minimum(t, n - 1)
    prv = lambda t: jnp.maximum(t - 1, 0)
    f, tails = pl.pallas_call(
        functools.partial(_ffn_prompt_kernel, bm=bm, blocks_per_seq=bps, nf=nf),
        grid=(n + 1,),
        in_specs=[
            pl.BlockSpec((bm, d), lambda t: (cur(t) // nf, 0)),
            pl.BlockSpec((None, d, bf), lambda t: (layer, 0, cur(t) % nf)),
            pl.BlockSpec((None, d, bf), lambda t: (layer, 0, nf + cur(t) % nf)),
            pl.BlockSpec((None, 2 * nf, CONV_WIDTH + 1, bf), lambda t: (layer, 0, 0, 0)),
            pl.BlockSpec((None, bf, d), lambda t: (layer, prv(t) % nf, 0)),
        ],
        out_specs=[
            pl.BlockSpec((bm, d), lambda t: (prv(t) // nf, 0)),
            pl.BlockSpec((1, 2 * nf, SUBLANES, bf), lambda t: (cur(t) // nf, 0, 0, 0)),
        ],
        out_shape=[
            jax.ShapeDtypeStruct((m, d), F32),
            jax.ShapeDtypeStruct((m // bm, 2 * nf, SUBLANES, bf), F32),
        ],
        scratch_shapes=[
            pltpu.VMEM((nf, SUBLANES, bf), F32),
            pltpu.VMEM((nf, SUBLANES, bf), F32),
            pltpu.VMEM((2, bm, bf), BF16),
        ],
        compiler_params=_cparams(("arbitrary",)),
        name="ffn_prompt",
    )(h2d, w_up, w_up, cwb, w_down)
    k = CONV_WIDTH - 1
    tails = tails.reshape(nseq, bps, 2 * nf, SUBLANES, bf)[:, bps - 1, :, SUBLANES - k:]
    return f, tails.transpose(0, 2, 1, 3).reshape(nseq, k, 2 * dff)


def _ffn_sample(h2d, w_up, cw, cb, w_down, x2d, g3, state, *, layer, bf, ns):
    m, d = x2d.shape
    dff = w_down.shape[1]
    nf = dff // bf
    nb = m // ns
    k = CONV_WIDTH - 1
    assert state.shape[1:] == (nb, k, 2 * dff) and k == 2
    full = lambda j: (0, 0)
    colg = lambda j: (0, j)
    colv = lambda j: (0, nf + j)
    xo, ug, uv = pl.pallas_call(
        functools.partial(_ffn_sample_kernel, ns=ns, nf=nf),
        grid=(nf,),
        in_specs=[
            pl.BlockSpec((m, d), full),
            pl.BlockSpec((None, d, bf), lambda j: (layer, 0, j)),
            pl.BlockSpec((None, d, bf), lambda j: (layer, 0, nf + j)),
            pl.BlockSpec((None, CONV_WIDTH, bf), lambda j: (layer, 0, j)),
            pl.BlockSpec((None, CONV_WIDTH, bf), lambda j: (layer, 0, nf + j)),
            pl.BlockSpec((None, 1, bf), lambda j: (layer, 0, j)),
            pl.BlockSpec((None, 1, bf), lambda j: (layer, 0, nf + j)),
            pl.BlockSpec((None, bf, d), lambda j: (layer, j, 0)),
            pl.BlockSpec((m, d), full),
            pl.BlockSpec((1, d), full),
            pl.BlockSpec((None, nb, k, bf), lambda j: (layer, 0, 0, j)),
            pl.BlockSpec((None, nb, k, bf), lambda j: (layer, 0, 0, nf + j)),
        ],
        out_specs=[
            pl.BlockSpec((m, d), full),
            pl.BlockSpec((m, bf), colg),
            pl.BlockSpec((m, bf), colg),
        ],
        out_shape=[
            jax.ShapeDtypeStruct((m, d), F32),
            jax.ShapeDtypeStruct((m, dff), F32),
            jax.ShapeDtypeStruct((m, dff), F32),
        ],
        scratch_shapes=[pltpu.VMEM((m, d), F32)],
        compiler_params=_cparams(("arbitrary",)),
        name="ffn_sample",
    )(h2d, w_up, w_up, cw, cw, cb, cb, w_down, x2d, g3, state, state)
    u = jnp.concatenate([ug, uv], axis=-1).reshape(nb, ns, 2 * dff)
    return xo, u[:, ns - k:]


def _rope_tables(pos, reps):
    half = D_ROPE // 2
    inv = ROPE_BASE ** (-jnp.arange(half, dtype=F32) / half)
    ang = pos[:, None] * inv[None, :]
    c, s = jnp.cos(ang), jnp.sin(ang)
    z = jnp.zeros((pos.shape[0], LANES - D_ROPE), F32)
    cos_t = jnp.concatenate([c, c, z], axis=1)
    sin_t = jnp.concatenate([s, s, z], axis=1)
    return jnp.tile(cos_t, (reps, 1)), jnp.tile(sin_t, (reps, 1))


def _signed_partner(w):
    half = D_ROPE // 2
    return jnp.concatenate([-w[..., half:], w[..., :half]], axis=-1)


def _stage_mla_weights(w_dq, w_uq, w_dkv, w_uk, w_uv, w_o):
    w_kr = w_dkv[:, KV_LORA:]
    w1 = jnp.concatenate([w_dq, w_dkv[:, :KV_LORA], w_kr, _signed_partner(w_kr)], axis=1).astype(BF16)
    wq = w_uq.reshape(Q_LORA, N_HEADS, D_NOPE + D_ROPE)
    wq_rope = wq[..., D_NOPE:]
    wuq = jnp.concatenate([wq[..., :D_NOPE], wq_rope, _signed_partner(wq_rope)], axis=-1)
    wuq = wuq.reshape(Q_LORA, N_HEADS * HEAD_PAD).astype(BF16)
    wuk = w_uk.reshape(KV_LORA, N_HEADS * D_NOPE).astype(BF16)
    wuv = w_uv.reshape(KV_LORA, N_HEADS * D_V).astype(BF16)
    wukt = jnp.transpose(w_uk, (1, 2, 0)).astype(BF16)
    return w1, wuq, wuk, wuv, wukt, w_o.astype(BF16)


def _block_rows(m, target):
    bm = min(m, target)
    while m % bm:
        bm //= 2
    return bm


def _block_cols(n, target):
    return max(c for c in range(LANES, min(n, target) + 1, LANES) if n % c == 0)


def kernel(x_prompt, x_sample, cache_ckv, cache_krope, state_pool, state_conv, norm_g, mla_w_dq, mla_q_norm, mla_w_uq, mla_w_dkv, mla_kv_norm, mla_w_uk, mla_w_uv, mla_w_o, pool_w, pool_scale, ffn_w_up, ffn_conv_w, ffn_conv_b, ffn_w_down):
    nbp, sp, d = x_prompt.shape
    nbs, ss, _ = x_sample.shape
    past = cache_ckv.shape[2]
    depth = norm_g.shape[0]
    mp, ms = nbp * sp, nbs * ss
    assert d % (len(POOL_WINDOWS) * LANES) == 0 and sp % (2 * LANES) == 0 and sp % CHUNK == 0
    assert ss >= POOL_HALO and ss % POOL_HALO == 0
    assert mla_w_uq.shape[-1] == N_HEADS * (D_NOPE + D_ROPE) and mla_w_dkv.shape[-1] == KV_LORA + D_ROPE
    assert ffn_conv_w.shape[1] == CONV_WIDTH and state_pool.shape[2] == POOL_HIST
    xp = x_prompt.reshape(mp, d)
    xs = x_sample.reshape(ms, d)

    bm_proj = _block_rows(sp, 256)
    bm_post = _block_rows(sp, 512)
    dff = ffn_w_down.shape[1]
    bm_ffn = _block_rows(sp, 1024)
    bf = _block_cols(dff, 512)
    bm_pool = _block_rows(sp, 512)
    blk_attn = _block_rows(sp, 4096)
    bk_attn = _block_rows(blk_attn, 512)
    bms = _block_rows(ms, 256)

    cos_p, sin_p = _rope_tables(jnp.arange(sp, dtype=jnp.int32).astype(F32), 1)
    cos_s, sin_s = _rope_tables((past + jnp.arange(ss, dtype=jnp.int32)).astype(F32), max(bms // ss, 1))

    w_up = ffn_w_up.astype(BF16)
    w_down = ffn_w_down.astype(BF16)
    cb = ffn_conv_b.reshape(depth, 1, -1)
    cwb = jnp.concatenate([ffn_conv_w, cb], axis=1).reshape(depth, CONV_WIDTH + 1, -1, bf).transpose(0, 2, 1, 3)

    outs = {k: [] for k in ("ckv_p", "kr_p", "pool_p", "conv_p", "ckv_s", "kr_s", "pool_s", "conv_s")}
    pend = None
    for i in range(depth):
        g = norm_g[i].reshape(4, 1, d)
        j = i // 2
        if i % 2 == 0:
            w1, wuq, wuk, wuv, wukt, wo = _stage_mla_weights(
                mla_w_dq[j], mla_w_uq[j], mla_w_dkv[j], mla_w_uk[j], mla_w_uv[j], mla_w_o[j])
            qn = mla_q_norm[j].reshape(1, Q_LORA)
            kvn = mla_kv_norm[j].reshape(1, KV_LORA)
            proj = _mla_project(xp, g[0], w1, qn, kvn, wuq, cos_p, sin_p, wuk, wuv, bm=bm_proj, resid=pend)
            q, ckv, kr, k, v = proj[:5]
            if pend is not None:
                xp = proj[5]
            o = _attention(q.reshape(nbp, sp, -1), k.reshape(nbp, sp, -1), v.reshape(nbp, sp, -1),
                           blk=blk_attn, bk=bk_attn)
            xp, hp = _attn_out(o.reshape(mp, -1), wo, xp, g[1], g[2], bm=bm_post)
            outs["ckv_p"].append(ckv.reshape(nbp, sp, KV_LORA))
            outs["kr_p"].append(kr.reshape(nbp, sp, D_ROPE))
            qs, ckv_s, kr_s = _mla_project(xs, g[0], w1, qn, kvn, wuq, cos_s, sin_s, bm=bms)
            qlat, qrope = _q_latent(qs, wukt)
            olat = _sample_attention(qlat, qrope, cache_ckv, cache_krope, ckv_s, kr_s, layer=j, ns=ss)
            os_ = _o_from_latent(olat, wuv)
            xs, hs = _attn_out(os_, wo, xs, g[1], g[2], bm=bms)
            outs["ckv_s"].append(ckv_s.reshape(nbs, ss, KV_LORA))
            outs["kr_s"].append(kr_s.reshape(nbs, ss, D_ROPE))
        else:
            wp = pool_w[j].astype(BF16)
            sc = pool_scale[j].reshape(1, d)
            xp, hp, tail_p = _pool_layer(xp, xp, g[0], wp, sc, g[1], g[2], bm=bm_pool,
                                         rows_per_seq=sp, pos0=0, halo_normed=False, resid=pend)
            outs["pool_p"].append(tail_p[:, POOL_HALO - POOL_HIST:])
            hist = jnp.pad(state_pool[j], ((0, 0), (POOL_HALO - POOL_HIST, 0), (0, 0))).reshape(-1, d)
            xs, hs, tail_s = _pool_layer(xs, hist, g[0], wp, sc, g[1], g[2], bm=ss,
                                         rows_per_seq=ss, pos0=past, halo_normed=True)
            outs["pool_s"].append(tail_s[:, POOL_HALO - POOL_HIST:])
        fp, cv_p = _ffn_prompt(hp, w_up, cwb, w_down, layer=i, bm=bm_ffn, rows_per_seq=sp)
        pend = (fp, g[3])
        xs, cv_s = _ffn_sample(hs, w_up, ffn_conv_w, cb, w_down, xs, g[3], state_conv, layer=i, bf=bf, ns=ss)
        outs["conv_p"].append(cv_p)
        outs["conv_s"].append(cv_s)
    xp = _residual_norm(pend[0], xp, pend[1], bm=bm_post)
    st = lambda k: jnp.stack(outs[k])
    return (xp.reshape(nbp, sp, d), xs.reshape(nbs, ss, d),
            st("ckv_p"), st("kr_p"), st("pool_p"), st("conv_p"),
            st("ckv_s"), st("kr_s"), st("pool_s"), st("conv_s"))
```

```python
import functools

import numpy as np
import jax
import jax.numpy as jnp
from jax import lax
from jax.experimental import pallas as pl
from jax.experimental.pallas import tpu as pltpu

F32 = jnp.float32
BF16 = jnp.bfloat16

CHUNK = 64
N_HEADS = 16
Q_LORA = 512
KV_LORA = 512
D_NOPE = 128
D_ROPE = 64
D_V = 128
ROPE_BASE = 10000.0
ATTN_SCALE = (D_NOPE + D_ROPE) ** -0.5
Q_SCALE = ATTN_SCALE * float(np.log2(np.e))
POOL_WINDOWS = (2, 4, 8, 16)
POOL_HIST = max(POOL_WINDOWS) - 1
CONV_WIDTH = 3
EPS = 1e-6

LANES = 128
SUBLANES = 8
HEAD_PAD = 2 * LANES
POOL_HALO = 16
VMEM_LIMIT = 56 * 1024 * 1024
NEG_BIG = -1e30


def _cparams(sem):
    return pltpu.CompilerParams(dimension_semantics=sem, vmem_limit_bytes=VMEM_LIMIT)


def _rms(xf, g):
    ms = jnp.mean(xf * xf, axis=-1, keepdims=True)
    return xf * lax.rsqrt(ms + EPS) * g


def _dot(a, b):
    return jnp.dot(a, b, preferred_element_type=F32)


def _dot_nt(a, b):
    return lax.dot_general(a, b, (((1,), (1,)), ((), ())), preferred_element_type=F32)


def _rope_cols(r, cos_t, sin_t):
    return r * cos_t + pltpu.roll(r, D_ROPE, axis=1) * sin_t


def _proj_kernel(x_ref, g_ref, w1_ref, qn_ref, kvn_ref, wuq_ref, cos_ref, sin_ref, *rest,
                 with_kv, with_resid):
    n_in = (2 if with_kv else 0) + (2 if with_resid else 0)
    ins, outs = list(rest[:n_in]), list(rest[n_in:])
    if with_kv:
        wuk_ref, wuv_ref = ins[:2]
        ins = ins[2:]
    x = x_ref[...]
    if with_resid:
        f_ref, g3_ref = ins[:2]
        xo_ref = outs.pop()
        x = x + _rms(f_ref[...], g3_ref[...])
        xo_ref[...] = x
    if with_kv:
        q_ref, ckv_ref, kr_ref, k_ref, v_ref = outs
    else:
        q_ref, ckv_ref, kr_ref = outs
    h = _rms(x, g_ref[...]).astype(BF16)
    y = _dot(h, w1_ref[...])
    cq = _rms(y[:, :Q_LORA], qn_ref[...]).astype(BF16)
    ckv = _rms(y[:, Q_LORA:Q_LORA + KV_LORA], kvn_ref[...])
    ckv_ref[...] = ckv
    cos_t = cos_ref[...]
    sin_t = sin_ref[...]
    kr = _rope_cols(y[:, Q_LORA + KV_LORA:], cos_t, sin_t)
    kr_ref[...] = kr[:, :D_ROPE]
    q = _dot(cq, wuq_ref[...]) * Q_SCALE
    for hd in range(N_HEADS):
        b0 = hd * HEAD_PAD
        q_ref[:, b0:b0 + LANES] = q[:, b0:b0 + LANES].astype(BF16)
        q_ref[:, b0 + LANES:b0 + HEAD_PAD] = _rope_cols(
            q[:, b0 + LANES:b0 + HEAD_PAD], cos_t, sin_t).astype(BF16)
    if with_kv:
        ckv_b = ckv.astype(BF16)
        kr_b = kr.astype(BF16)
        kn = _dot(ckv_b, wuk_ref[...])
        for hd in range(N_HEADS):
            b0 = hd * HEAD_PAD
            k_ref[:, b0:b0 + LANES] = kn[:, hd * D_NOPE:(hd + 1) * D_NOPE].astype(BF16)
            k_ref[:, b0 + LANES:b0 + HEAD_PAD] = kr_b
        v_ref[...] = _dot(ckv_b, wuv_ref[...]).astype(BF16)


def _mla_project(x2d, g, w1, qn, kvn, wuq, cos_t, sin_t, wuk=None, wuv=None, *, bm, resid=None):
    m, d = x2d.shape
    with_kv = wuk is not None
    nt = cos_t.shape[0] // bm
    row = lambda i: (i, 0)
    const = lambda i: (0, 0)
    in_specs = [
        pl.BlockSpec((bm, d), row),
        pl.BlockSpec((1, d), const),
        pl.BlockSpec(w1.shape, const),
        pl.BlockSpec((1, Q_LORA), const),
        pl.BlockSpec((1, KV_LORA), const),
        pl.BlockSpec(wuq.shape, const),
        pl.BlockSpec((bm, LANES), lambda i: (i % nt, 0)),
        pl.BlockSpec((bm, LANES), lambda i: (i % nt, 0)),
    ]
    args = [x2d, g, w1, qn, kvn, wuq, cos_t, sin_t]
    out_shape = [
        jax.ShapeDtypeStruct((m, N_HEADS * HEAD_PAD), BF16),
        jax.ShapeDtypeStruct((m, KV_LORA), F32),
        jax.ShapeDtypeStruct((m, D_ROPE), F32),
    ]
    out_specs = [
        pl.BlockSpec((bm, N_HEADS * HEAD_PAD), row),
        pl.BlockSpec((bm, KV_LORA), row),
        pl.BlockSpec((bm, D_ROPE), row),
    ]
    if with_kv:
        in_specs += [pl.BlockSpec(wuk.shape, const), pl.BlockSpec(wuv.shape, const)]
        args += [wuk, wuv]
        out_shape += [
            jax.ShapeDtypeStruct((m, N_HEADS * HEAD_PAD), BF16),
            jax.ShapeDtypeStruct((m, N_HEADS * D_V), BF16),
        ]
        out_specs += [
            pl.BlockSpec((bm, N_HEADS * HEAD_PAD), row),
            pl.BlockSpec((bm, N_HEADS * D_V), row),
        ]
    if resid is not None:
        in_specs += [pl.BlockSpec((bm, d), row), pl.BlockSpec((1, d), const)]
        args += list(resid)
        out_shape.append(jax.ShapeDtypeStruct((m, d), F32))
        out_specs.append(pl.BlockSpec((bm, d), row))
    return pl.pallas_call(
        functools.partial(_proj_kernel, with_kv=with_kv, with_resid=resid is not None),
        grid=(m // bm,),
        in_specs=in_specs,
        out_specs=out_specs,
        out_shape=out_shape,
        compiler_params=_cparams(("arbitrary",)),
        name="mla_project",
    )(*args)


def _attn_kernel(q_ref, k_ref, v_ref, o_ref, m_scr, l_scr, acc_scr, *, blk, bk, sub):
    qi = pl.program_id(2)
    m_scr[...] = jnp.full(m_scr.shape, NEG_BIG, F32)
    l_scr[...] = jnp.zeros(l_scr.shape, F32)
    acc_scr[...] = jnp.zeros(acc_scr.shape, F32)

    def step(start, diag):
        scores = []
        for c in range(blk // sub):
            nk = bk if diag is None else min(bk, (c + 1) * sub - diag)
            if nk <= 0:
                continue
            k = k_ref[0, pl.ds(start, nk), :]
            st = _dot_nt(k, q_ref[0, c * sub:(c + 1) * sub, :])
            if diag is not None and diag + nk > c * sub + CHUNK:
                kc = (lax.broadcasted_iota(jnp.int32, st.shape, 0) + diag) // CHUNK
                qc = (lax.broadcasted_iota(jnp.int32, st.shape, 1) + c * sub) // CHUNK
                st = jnp.where(kc <= qc, st, NEG_BIG)
            scores.append((c, nk, st))
        for c, nk, st in scores:
            cs = slice(c * sub, (c + 1) * sub)
            v = v_ref[0, pl.ds(start, nk), :]
            m_prev = m_scr[:, cs]
            m_new = jnp.maximum(m_prev, jnp.max(st, axis=0, keepdims=True))
            alpha = jnp.exp2(m_prev - m_new)
            p = jnp.exp2(st - m_new)
            l_scr[:, cs] = alpha * l_scr[:, cs] + jnp.sum(p, axis=0, keepdims=True)
            pv = lax.dot_general(v, p.astype(BF16), (((0,), (0,)), ((), ())),
                                 preferred_element_type=F32)
            acc_scr[:, cs] = alpha * acc_scr[:, cs] + pv
            m_scr[:, cs] = m_new

    def body(ki, carry):
        step(pl.multiple_of(ki * bk, bk), None)
        return carry

    per_q = blk // bk
    lax.fori_loop(0, qi * per_q, body, 0)
    for d in range(per_q):
        step(pl.multiple_of(qi * blk + d * bk, bk), d * bk)
    o_ref[0] = jnp.transpose(acc_scr[...] / l_scr[...]).astype(o_ref.dtype)


def _attention(q, k, v, *, blk, bk):
    b, s, _ = q.shape
    return pl.pallas_call(
        functools.partial(_attn_kernel, blk=blk, bk=bk, sub=min(bk, HEAD_PAD)),
        grid=(b, N_HEADS, s // blk),
        in_specs=[
            pl.BlockSpec((1, blk, HEAD_PAD), lambda bi, h, qi: (bi, qi, h)),
            pl.BlockSpec((1, s, HEAD_PAD), lambda bi, h, qi: (bi, 0, h)),
            pl.BlockSpec((1, s, D_V), lambda bi, h, qi: (bi, 0, h)),
        ],
        out_specs=pl.BlockSpec((1, blk, D_V), lambda bi, h, qi: (bi, qi, h)),
        out_shape=jax.ShapeDtypeStruct((b, s, N_HEADS * D_V), BF16),
        scratch_shapes=[
            pltpu.VMEM((1, blk), F32),
            pltpu.VMEM((1, blk), F32),
            pltpu.VMEM((D_V, blk), F32),
        ],
        compiler_params=_cparams(("arbitrary", "arbitrary", "arbitrary")),
        name="prompt_attention",
    )(q, k, v)


def _qlat_kernel(q_ref, wukt_ref, qlat_ref, qrope_ref):
    q = q_ref[...]
    qlat_ref[0] = _dot(q[:, :D_NOPE], wukt_ref[0]).astype(BF16)
    qrope_ref[0] = q[:, LANES:]


def _q_latent(q, wukt):
    m = q.shape[0]
    return pl.pallas_call(
        _qlat_kernel,
        grid=(N_HEADS,),
        in_specs=[
            pl.BlockSpec((m, HEAD_PAD), lambda h: (0, h)),
            pl.BlockSpec((1, D_NOPE, KV_LORA), lambda h: (h, 0, 0)),
        ],
        out_specs=[
            pl.BlockSpec((1, m, KV_LORA), lambda h: (h, 0, 0)),
            pl.BlockSpec((1, m, LANES), lambda h: (h, 0, 0)),
        ],
        out_shape=[
            jax.ShapeDtypeStruct((N_HEADS, m, KV_LORA), BF16),
            jax.ShapeDtypeStruct((N_HEADS, m, LANES), BF16),
        ],
        compiler_params=_cparams(("arbitrary",)),
        name="sample_q_latent",
    )(q, wukt)


def _sattn_kernel(qlat_ref, qrope_ref, cckv_ref, ckr_ref, nckv_ref, nkr_ref, o_ref, *, past, ns):
    rows = N_HEADS * ns
    ql = qlat_ref[...].reshape(rows, KV_LORA)
    qr = qrope_ref[...].reshape(rows, LANES)[:, :D_ROPE]
    cc = cckv_ref[0].astype(BF16)
    ck = ckr_ref[0].astype(BF16)
    nc = nckv_ref[...].astype(BF16)
    nk = nkr_ref[...].astype(BF16)
    s1 = _dot_nt(ql, cc) + _dot(qr, ck)
    s2 = _dot_nt(ql, nc) + _dot_nt(qr, nk)
    qt = lax.broadcasted_iota(jnp.int32, s2.shape, 0) % ns
    kt = lax.broadcasted_iota(jnp.int32, s2.shape, 1)
    s2 = jnp.where((past + kt) // CHUNK <= (past + qt) // CHUNK, s2, NEG_BIG)
    m = jnp.maximum(jnp.max(s1, axis=1, keepdims=True), jnp.max(s2, axis=1, keepdims=True))
    p1 = jnp.exp2(s1 - m)
    p2 = jnp.exp2(s2 - m)
    l = jnp.sum(p1, axis=1, keepdims=True) + jnp.sum(p2, axis=1, keepdims=True)
    o = (_dot(p1.astype(BF16), cc) + _dot(p2.astype(BF16), nc)) / l
    o_ref[...] = o.astype(BF16).reshape(N_HEADS, ns, KV_LORA)


def _sample_attention(qlat, qrope, cache_ckv, cache_kr, new_ckv, new_kr, *, layer, ns):
    _, nb, past, _ = cache_ckv.shape
    m = nb * ns
    return pl.pallas_call(
        functools.partial(_sattn_kernel, past=past, ns=ns),
        grid=(nb,),
        in_specs=[
            pl.BlockSpec((N_HEADS, ns, KV_LORA), lambda b: (0, b, 0)),
            pl.BlockSpec((N_HEADS, ns, LANES), lambda b: (0, b, 0)),
            pl.BlockSpec((None, 1, past, KV_LORA), lambda b: (layer, b, 0, 0)),
            pl.BlockSpec((None, 1, D_ROPE, past), lambda b: (layer, b, 0, 0)),
            pl.BlockSpec((ns, KV_LORA), lambda b: (b, 0)),
            pl.BlockSpec((ns, D_ROPE), lambda b: (b, 0)),
        ],
        out_specs=pl.BlockSpec((N_HEADS, ns, KV_LORA), lambda b: (0, b, 0)),
        out_shape=jax.ShapeDtypeStruct((N_HEADS, m, KV_LORA), BF16),
        compiler_params=_cparams(("arbitrary",)),
        name="sample_attention",
    )(qlat, qrope, cache_ckv, cache_kr, new_ckv, new_kr)


def _ouv_kernel(olat_ref, wuv_ref, o_ref):
    o_ref[...] = _dot(olat_ref[0], wuv_ref[...]).astype(BF16)


def _o_from_latent(olat, wuv):
    m = olat.shape[1]
    return pl.pallas_call(
        _ouv_kernel,
        grid=(N_HEADS,),
        in_specs=[
            pl.BlockSpec((1, m, KV_LORA), lambda h: (h, 0, 0)),
            pl.BlockSpec((KV_LORA, D_V), lambda h: (0, h)),
        ],
        out_specs=pl.BlockSpec((m, D_V), lambda h: (0, h)),
        out_shape=jax.ShapeDtypeStruct((m, N_HEADS * D_V), BF16),
        compiler_params=_cparams(("arbitrary",)),
        name="sample_o_from_latent",
    )(olat, wuv)


def _post_kernel(o_ref, w_ref, x_ref, g1_ref, g2_ref, xo_ref, h2_ref):
    xn = x_ref[...] + _rms(_dot(o_ref[...], w_ref[...]), g1_ref[...])
    xo_ref[...] = xn
    h2_ref[...] = _rms(xn, g2_ref[...]).astype(BF16)


def _attn_out(o2d, w_o, x2d, g1, g2, *, bm):
    m, d = x2d.shape
    row = lambda i: (i, 0)
    const = lambda i: (0, 0)
    return pl.pallas_call(
        _post_kernel,
        grid=(m // bm,),
        in_specs=[
            pl.BlockSpec((bm, o2d.shape[1]), row),
            pl.BlockSpec(w_o.shape, const),
            pl.BlockSpec((bm, d), row),
            pl.BlockSpec((1, d), const),
            pl.BlockSpec((1, d), const),
        ],
        out_specs=[pl.BlockSpec((bm, d), row), pl.BlockSpec((bm, d), row)],
        out_shape=[jax.ShapeDtypeStruct((m, d), F32), jax.ShapeDtypeStruct((m, d), BF16)],
        compiler_params=_cparams(("arbitrary",)),
        name="attn_out_post",
    )(o2d, w_o, x2d, g1, g2)


def _pool_kernel(x_ref, halo_ref, g0_ref, w_ref, sc_ref, g1_ref, g2_ref, *rest,
                 bm, blocks_per_seq, pos0, halo_normed, with_resid):
    i = pl.program_id(0)
    x = x_ref[...]
    halo = halo_ref[...]
    if with_resid:
        f_ref, fhalo_ref, g3_ref, xo_ref, h2_ref, tail_ref = rest
        x = x + _rms(f_ref[...], g3_ref[...])
        halo = halo + _rms(fhalo_ref[...], g3_ref[...])
    else:
        xo_ref, h2_ref, tail_ref = rest
    g0 = g0_ref[...]
    h = _rms(x, g0)
    if not halo_normed:
        halo = _rms(halo, g0)
        halo = jnp.where(i % blocks_per_seq == 0, 0.0, halo)
    cat = jnp.concatenate([halo, h], axis=0)
    t = (i % blocks_per_seq) * bm + lax.broadcasted_iota(jnp.int32, (bm, 1), 0)
    posf = (t + pos0).astype(F32)
    gd = x.shape[1] // len(POOL_WINDOWS)
    ys = []
    for g, w in enumerate(POOL_WINDOWS):
        c = cat[:, g * gd:(g + 1) * gd]
        acc = c
        span = 1
        while span < w:
            acc = acc + pltpu.roll(acc, span, axis=0)
            span *= 2
        cnt = jnp.minimum(jnp.float32(w), posf + 1.0)
        mean = acc[POOL_HALO:] / cnt
        dlt = (mean - h[:, g * gd:(g + 1) * gd]).astype(BF16)
        ys.append(_dot(dlt, w_ref[g]))
    y = jnp.concatenate(ys, axis=1) * sc_ref[...]
    xn = x + _rms(y, g1_ref[...])
    xo_ref[...] = xn
    h2_ref[...] = _rms(xn, g2_ref[...]).astype(BF16)
    tail_ref[0] = h[bm - POOL_HALO:]


def _pool_layer(x2d, halo_src, g0, w_pool, scale, g1, g2, *, bm, rows_per_seq, pos0, halo_normed, resid=None):
    m, d = x2d.shape
    bps = rows_per_seq // bm
    nseq = m // rows_per_seq
    row = lambda i: (i, 0)
    const = lambda i: (0, 0)
    if halo_normed:
        halo_map = lambda i: (i, 0)
    else:
        hb = bm // POOL_HALO
        halo_map = lambda i: (jnp.maximum(i * hb - 1, 0), 0)
    in_specs = [
        pl.BlockSpec((bm, d), row),
        pl.BlockSpec((POOL_HALO, d), halo_map),
        pl.BlockSpec((1, d), const),
        pl.BlockSpec(w_pool.shape, lambda i: (0, 0, 0)),
        pl.BlockSpec((1, d), const),
        pl.BlockSpec((1, d), const),
        pl.BlockSpec((1, d), const),
    ]
    args = [x2d, halo_src, g0, w_pool, scale, g1, g2]
    if resid is not None:
        assert not halo_normed
        in_specs += [pl.BlockSpec((bm, d), row), pl.BlockSpec((POOL_HALO, d), halo_map),
                     pl.BlockSpec((1, d), const)]
        args += [resid[0], resid[0], resid[1]]
    return pl.pallas_call(
        functools.partial(_pool_kernel, bm=bm, blocks_per_seq=bps, pos0=pos0, halo_normed=halo_normed,
                          with_resid=resid is not None),
        grid=(m // bm,),
        in_specs=in_specs,
        out_specs=[
            pl.BlockSpec((bm, d), row),
            pl.BlockSpec((bm, d), row),
            pl.BlockSpec((1, POOL_HALO, d), lambda i: (i // bps, 0, 0)),
        ],
        out_shape=[
            jax.ShapeDtypeStruct((m, d), F32),
            jax.ShapeDtypeStruct((m, d), BF16),
            jax.ShapeDtypeStruct((nseq, POOL_HALO, d), F32),
        ],
        compiler_params=_cparams(("arbitrary",)),
        name="pool_layer",
    )(*args)


def _conv3(u, prev1, prev2, taps, bias):
    return bias + taps[0:1, :] * prev2 + taps[1:2, :] * prev1 + taps[2:3, :] * u


def _ffn_tail(j, nf, act, wd_ref, x_ref, g3_ref, xo_ref, acc_scr):
    part = _dot(act.astype(BF16), wd_ref[...])

    @pl.when(j == 0)
    def _():
        acc_scr[...] = part

    @pl.when(j > 0)
    def _():
        acc_scr[...] += part

    @pl.when(j == nf - 1)
    def _():
        xo_ref[...] = x_ref[...] + _rms(acc_scr[...], g3_ref[...])


def _ffn_prompt_kernel(h_ref, wg_ref, wv_ref, cwb_ref, wd_ref, f_ref, tail_ref, cg_scr, cv_scr, act_scr,
                       *, bm, blocks_per_seq, nf):
    t = pl.program_id(0)
    cur = jnp.minimum(t, pl.num_programs(0) - 2)
    i = cur // nf
    j = cur % nf
    prev = jnp.maximum(t - 1, 0)

    @pl.when(t == 0)
    def _():
        act_scr[...] = jnp.zeros(act_scr.shape, BF16)

    @pl.when(prev % nf == 0)
    def _():
        f_ref[...] = jnp.zeros(f_ref.shape, F32)

    f_ref[...] += _dot(act_scr[(t + 1) % 2], wd_ref[...])

    h = h_ref[...]
    first = i % blocks_per_seq == 0

    def branch(w_ref, col, carry_scr):
        u = _dot(h, w_ref[...])
        p = cwb_ref[col]
        taps, bias = p[:CONV_WIDTH], p[CONV_WIDTH:]
        prev8 = jnp.where(first, 0.0, carry_scr[j])
        c = _conv3(u, pltpu.roll(u, 1, axis=0), pltpu.roll(u, 2, axis=0), taps, bias)
        head = jnp.concatenate([prev8, u[:SUBLANES]], axis=0)
        c_head = _conv3(head, pltpu.roll(head, 1, axis=0), pltpu.roll(head, 2, axis=0),
                        taps, bias)[SUBLANES:]
        c = jnp.concatenate([c_head, c[SUBLANES:]], axis=0)
        last8 = u[bm - SUBLANES:]
        carry_scr[j] = last8
        tail_ref[0, col] = last8
        return c

    gate = branch(wg_ref, j, cg_scr)
    val = branch(wv_ref, nf + j, cv_scr)
    act_scr[t % 2] = (gate * jax.nn.sigmoid(gate) * val).astype(BF16)


def _resid_kernel(f_ref, x_ref, g_ref, xo_ref):
    xo_ref[...] = x_ref[...] + _rms(f_ref[...], g_ref[...])


def _residual_norm(f2d, x2d, g, *, bm):
    m, d = x2d.shape
    row = lambda i: (i, 0)
    return pl.pallas_call(
        _resid_kernel,
        grid=(m // bm,),
        in_specs=[pl.BlockSpec((bm, d), row), pl.BlockSpec((bm, d), row), pl.BlockSpec((1, d), lambda i: (0, 0))],
        out_specs=pl.BlockSpec((bm, d), row),
        out_shape=jax.ShapeDtypeStruct((m, d), F32),
        compiler_params=_cparams(("arbitrary",)),
        name="residual_norm",
    )(f2d, x2d, g)


def _ffn_sample_kernel(h_ref, wg_ref, wv_ref, cwg_ref, cwv_ref, cbg_ref, cbv_ref, wd_ref, x_ref, g3_ref,
                       stg_ref, stv_ref, xo_ref, ug_ref, uv_ref, acc_scr, *, ns, nf):
    j = pl.program_id(0)
    h = h_ref[...]
    m = h.shape[0]
    t = lax.broadcasted_iota(jnp.int32, (m, 1), 0) % ns

    def branch(w_ref, cw_ref, cb_ref, st_ref, u_ref):
        u = _dot(h, w_ref[...])
        u_ref[...] = u
        st = st_ref[...]

        def per_row(k):
            row = st[:, k:k + 1, :]
            return jnp.broadcast_to(row, (m // ns, ns, row.shape[2])).reshape(m, row.shape[2])

        older, newer = per_row(0), per_row(1)
        prev1 = jnp.where(t < 1, newer, pltpu.roll(u, 1, axis=0))
        prev2 = jnp.where(t < 1, older, jnp.where(t < 2, newer, pltpu.roll(u, 2, axis=0)))
        return _conv3(u, prev1, prev2, cw_ref[...], cb_ref[...])

    gate = branch(wg_ref, cwg_ref, cbg_ref, stg_ref, ug_ref)
    val = branch(wv_ref, cwv_ref, cbv_ref, stv_ref, uv_ref)
    act = gate * jax.nn.sigmoid(gate) * val
    _ffn_tail(j, nf, act, wd_ref, x_ref, g3_ref, xo_ref, acc_scr)


def _ffn_prompt(h2d, w_up, cwb, w_down, *, layer, bm, rows_per_seq):
    m, d = h2d.shape
    dff = w_down.shape[1]
    bf = cwb.shape[3]
    nf = dff // bf
    bps = rows_per_seq // bm
    nseq = m // rows_per_seq
    n = (m // bm) * nf
    cur = lambda t: jnp.minimum(t, n - 1)
    prv = lambda t: jnp.maximum(t - 1, 0)
    f, tails = pl.pallas_call(
        functools.partial(_ffn_prompt_kernel, bm=bm, blocks_per_seq=bps, nf=nf),
        grid=(n + 1,),
        in_specs=[
            pl.BlockSpec((bm, d), lambda t: (cur(t) // nf, 0)),
            pl.BlockSpec((None, d, bf), lambda t: (layer, 0, cur(t) % nf)),
            pl.BlockSpec((None, d, bf), lambda t: (layer, 0, nf + cur(t) % nf)),
            pl.BlockSpec((None, 2 * nf, CONV_WIDTH + 1, bf), lambda t: (layer, 0, 0, 0)),
            pl.BlockSpec((None, bf, d), lambda t: (layer, prv(t) % nf, 0)),
        ],
        out_specs=[
            pl.BlockSpec((bm, d), lambda t: (prv(t) // nf, 0)),
            pl.BlockSpec((1, 2 * nf, SUBLANES, bf), lambda t: (cur(t) // nf, 0, 0, 0)),
        ],
        out_shape=[
            jax.ShapeDtypeStruct((m, d), F32),
            jax.ShapeDtypeStruct((m // bm, 2 * nf, SUBLANES, bf), F32),
        ],
        scratch_shapes=[
            pltpu.VMEM((nf, SUBLANES, bf), F32),
            pltpu.VMEM((nf, SUBLANES, bf), F32),
            pltpu.VMEM((2, bm, bf), BF16),
        ],
        compiler_params=_cparams(("arbitrary",)),
        name="ffn_prompt",
    )(h2d, w_up, w_up, cwb, w_down)
    k = CONV_WIDTH - 1
    tails = tails.reshape(nseq, bps, 2 * nf, SUBLANES, bf)[:, bps - 1, :, SUBLANES - k:]
    return f, tails.transpose(0, 2, 1, 3).reshape(nseq, k, 2 * dff)


def _ffn_sample(h2d, w_up, cw, cb, w_down, x2d, g3, state, *, layer, bf, ns):
    m, d = x2d.shape
    dff = w_down.shape[1]
    nf = dff // bf
    nb = m // ns
    k = CONV_WIDTH - 1
    assert state.shape[1:] == (nb, k, 2 * dff) and k == 2
    full = lambda j: (0, 0)
    colg = lambda j: (0, j)
    colv = lambda j: (0, nf + j)
    xo, ug, uv = pl.pallas_call(
        functools.partial(_ffn_sample_kernel, ns=ns, nf=nf),
        grid=(nf,),
        in_specs=[
            pl.BlockSpec((m, d), full),
            pl.BlockSpec((None, d, bf), lambda j: (layer, 0, j)),
            pl.BlockSpec((None, d, bf), lambda j: (layer, 0, nf + j)),
            pl.BlockSpec((None, CONV_WIDTH, bf), lambda j: (layer, 0, j)),
            pl.BlockSpec((None, CONV_WIDTH, bf), lambda j: (layer, 0, nf + j)),
            pl.BlockSpec((None, 1, bf), lambda j: (layer, 0, j)),
            pl.BlockSpec((None, 1, bf), lambda j: (layer, 0, nf + j)),
            pl.BlockSpec((None, bf, d), lambda j: (layer, j, 0)),
            pl.BlockSpec((m, d), full),
            pl.BlockSpec((1, d), full),
            pl.BlockSpec((None, nb, k, bf), lambda j: (layer, 0, 0, j)),
            pl.BlockSpec((None, nb, k, bf), lambda j: (layer, 0, 0, nf + j)),
        ],
        out_specs=[
            pl.BlockSpec((m, d), full),
            pl.BlockSpec((m, bf), colg),
            pl.BlockSpec((m, bf), colg),
        ],
        out_shape=[
            jax.ShapeDtypeStruct((m, d), F32),
            jax.ShapeDtypeStruct((m, dff), F32),
            jax.ShapeDtypeStruct((m, dff), F32),
        ],
        scratch_shapes=[pltpu.VMEM((m, d), F32)],
        compiler_params=_cparams(("arbitrary",)),
        name="ffn_sample",
    )(h2d, w_up, w_up, cw, cw, cb, cb, w_down, x2d, g3, state, state)
    u = jnp.concatenate([ug, uv], axis=-1).reshape(nb, ns, 2 * dff)
    return xo, u[:, ns - k:]


def _rope_tables(pos, reps):
    half = D_ROPE // 2
    inv = ROPE_BASE ** (-jnp.arange(half, dtype=F32) / half)
    ang = pos[:, None] * inv[None, :]
    c, s = jnp.cos(ang), jnp.sin(ang)
    z = jnp.zeros((pos.shape[0], LANES - D_ROPE), F32)
    cos_t = jnp.concatenate([c, c, z], axis=1)
    sin_t = jnp.concatenate([s, s, z], axis=1)
    return jnp.tile(cos_t, (reps, 1)), jnp.tile(sin_t, (reps, 1))


def _signed_partner(w):
    half = D_ROPE // 2
    return jnp.concatenate([-w[..., half:], w[..., :half]], axis=-1)


def _stage_mla_weights(w_dq, w_uq, w_dkv, w_uk, w_uv, w_o):
    w_kr = w_dkv[:, KV_LORA:]
    w1 = jnp.concatenate([w_dq, w_dkv[:, :KV_LORA], w_kr, _signed_partner(w_kr)], axis=1).astype(BF16)
    wq = w_uq.reshape(Q_LORA, N_HEADS, D_NOPE + D_ROPE)
    wq_rope = wq[..., D_NOPE:]
    wuq = jnp.concatenate([wq[..., :D_NOPE], wq_rope, _signed_partner(wq_rope)], axis=-1)
    wuq = wuq.reshape(Q_LORA, N_HEADS * HEAD_PAD).astype(BF16)
    wuk = w_uk.reshape(KV_LORA, N_HEADS * D_NOPE).astype(BF16)
    wuv = w_uv.reshape(KV_LORA, N_HEADS * D_V).astype(BF16)
    wukt = jnp.transpose(w_uk, (1, 2, 0)).astype(BF16)
    return w1, wuq, wuk, wuv, wukt, w_o.astype(BF16)


def _block_rows(m, target):
    bm = min(m, target)
    while m % bm:
        bm //= 2
    return bm


def _block_cols(n, target):
    return max(c for c in range(LANES, min(n, target) + 1, LANES) if n % c == 0)


def kernel(x_prompt, x_sample, cache_ckv, cache_krope, state_pool, state_conv, norm_g, mla_w_dq, mla_q_norm, mla_w_uq, mla_w_dkv, mla_kv_norm, mla_w_uk, mla_w_uv, mla_w_o, pool_w, pool_scale, ffn_w_up, ffn_conv_w, ffn_conv_b, ffn_w_down):
    nbp, sp, d = x_prompt.shape
    nbs, ss, _ = x_sample.shape
    past = cache_ckv.shape[2]
    depth = norm_g.shape[0]
    mp, ms = nbp * sp, nbs * ss
    assert d % (len(POOL_WINDOWS) * LANES) == 0 and sp % (2 * LANES) == 0 and sp % CHUNK == 0
    assert ss >= POOL_HALO and ss % POOL_HALO == 0
    assert mla_w_uq.shape[-1] == N_HEADS * (D_NOPE + D_ROPE) and mla_w_dkv.shape[-1] == KV_LORA + D_ROPE
    assert ffn_conv_w.shape[1] == CONV_WIDTH and state_pool.shape[2] == POOL_HIST
    xp = x_prompt.reshape(mp, d)
    xs = x_sample.reshape(ms, d)

    bm_proj = _block_rows(sp, 256)
    bm_post = _block_rows(sp, 512)
    dff = ffn_w_down.shape[1]
    bm_ffn = _block_rows(sp, 1024)
    bf = _block_cols(dff, 512)
    bm_pool = _block_rows(sp, 512)
    blk_attn = _block_rows(sp, 4096)
    bk_attn = _block_rows(blk_attn, 512)
    bms = _block_rows(ms, 256)

    cos_p, sin_p = _rope_tables(jnp.arange(sp, dtype=jnp.int32).astype(F32), 1)
    cos_s, sin_s = _rope_tables((past + jnp.arange(ss, dtype=jnp.int32)).astype(F32), max(bms // ss, 1))

    cache_kr_t = jnp.swapaxes(cache_krope, 2, 3)
    w_up = ffn_w_up.astype(BF16)
    w_down = ffn_w_down.astype(BF16)
    cb = ffn_conv_b.reshape(depth, 1, -1)
    cwb = jnp.concatenate([ffn_conv_w, cb], axis=1).reshape(depth, CONV_WIDTH + 1, -1, bf).transpose(0, 2, 1, 3)

    outs = {k: [] for k in ("ckv_p", "kr_p", "pool_p", "conv_p", "ckv_s", "kr_s", "pool_s", "conv_s")}
    pend = None
    for i in range(depth):
        g = norm_g[i].reshape(4, 1, d)
        j = i // 2
        if i % 2 == 0:
            w1, wuq, wuk, wuv, wukt, wo = _stage_mla_weights(
                mla_w_dq[j], mla_w_uq[j], mla_w_dkv[j], mla_w_uk[j], mla_w_uv[j], mla_w_o[j])
            qn = mla_q_norm[j].reshape(1, Q_LORA)
            kvn = mla_kv_norm[j].reshape(1, KV_LORA)
            proj = _mla_project(xp, g[0], w1, qn, kvn, wuq, cos_p, sin_p, wuk, wuv, bm=bm_proj, resid=pend)
            q, ckv, kr, k, v = proj[:5]
            if pend is not None:
                xp = proj[5]
            o = _attention(q.reshape(nbp, sp, -1), k.reshape(nbp, sp, -1), v.reshape(nbp, sp, -1),
                           blk=blk_attn, bk=bk_attn)
            xp, hp = _attn_out(o.reshape(mp, -1), wo, xp, g[1], g[2], bm=bm_post)
            outs["ckv_p"].append(ckv.reshape(nbp, sp, KV_LORA))
            outs["kr_p"].append(kr.reshape(nbp, sp, D_ROPE))
            qs, ckv_s, kr_s = _mla_project(xs, g[0], w1, qn, kvn, wuq, cos_s, sin_s, bm=bms)
            qlat, qrope = _q_latent(qs, wukt)
            olat = _sample_attention(qlat, qrope, cache_ckv, cache_kr_t, ckv_s, kr_s, layer=j, ns=ss)
            os_ = _o_from_latent(olat, wuv)
            xs, hs = _attn_out(os_, wo, xs, g[1], g[2], bm=bms)
            outs["ckv_s"].append(ckv_s.reshape(nbs, ss, KV_LORA))
            outs["kr_s"].append(kr_s.reshape(nbs, ss, D_ROPE))
        else:
            wp = pool_w[j].astype(BF16)
            sc = pool_scale[j].reshape(1, d)
            xp, hp, tail_p = _pool_layer(xp, xp, g[0], wp, sc, g[1], g[2], bm=bm_pool,
                                         rows_per_seq=sp, pos0=0, halo_normed=False, resid=pend)
            outs["pool_p"].append(tail_p[:, POOL_HALO - POOL_HIST:])
            hist = jnp.pad(state_pool[j], ((0, 0), (POOL_HALO - POOL_HIST, 0), (0, 0))).reshape(-1, d)
            xs, hs, tail_s = _pool_layer(xs, hist, g[0], wp, sc, g[1], g[2], bm=ss,
                                         rows_per_seq=ss, pos0=past, halo_normed=True)
            outs["pool_s"].append(tail_s[:, POOL_HALO - POOL_HIST:])
        fp, cv_p = _ffn_prompt(hp, w_up, cwb, w_down, layer=i, bm=bm_ffn, rows_per_seq=sp)
        pend = (fp, g[3])
        xs, cv_s = _ffn_sample(hs, w_up, ffn_conv_w, cb, w_down, xs, g[3], state_conv, layer=i, bf=bf, ns=ss)
        outs["conv_p"].append(cv_p)
        outs["conv_s"].append(cv_s)
    xp = _residual_norm(pend[0], xp, pend[1], bm=bm_post)
    st = lambda k: jnp.stack(outs[k])
    return (xp.reshape(nbp, sp, d), xs.reshape(nbs, ss, d),
            st("ckv_p"), st("kr_p"), st("pool_p"), st("conv_p"),
            st("ckv_s"), st("kr_s"), st("pool_s"), st("conv_s"))
```
